```python
import jax, jax.numpy as jnp
from jax import lax
import numpy as np

D_MODEL = 2048
BATCH = 4
SEQ = 2048
DEPTH = 2

CTX_LEN = 256
GRID_W = 64
EPS = 1e-6

POOL_WINDOWS = (2, 4, 8, 16)
POOL_W = D_MODEL // 2
POOL_GROUP = POOL_W // len(POOL_WINDOWS)
CONV_W = D_MODEL // 2
CONV_K = 3
EVEN_IN = 2 * POOL_W + 4 * CONV_W

HG_DK = 128
HG_HEADS = D_MODEL // HG_DK
HG_DV = D_MODEL // HG_HEADS
HG_K = HG_HEADS * HG_DK
HG_V = HG_HEADS * HG_DV
ODD_IN = 3 * HG_K + 2 * HG_V
CHUNK = 64

N_EVEN = (DEPTH + 1) // 2
N_ODD = DEPTH // 2

kernel_name = "hybrid_pool_conv_hgrn2_diffusion_block"


def rms_norm(x, g):
    xf = x.astype(jnp.float32)
    y = xf * lax.rsqrt(jnp.mean(xf * xf, axis=-1, keepdims=True) + EPS)
    return (y * g.astype(jnp.float32)).astype(x.dtype)


def box_mean(x, w, axis):
    n = x.shape[axis]
    cs = jnp.cumsum(x.astype(jnp.float32), axis=axis)
    pad = [(0, 0)] * x.ndim
    pad[axis] = (1, 0)
    cs = jnp.pad(cs, pad)
    t = jnp.arange(n)
    lo = jnp.maximum(t - w // 2, 0)
    hi = jnp.minimum(t + w // 2 - 1, n - 1)
    s = jnp.take(cs, hi + 1, axis=axis) - jnp.take(cs, lo, axis=axis)
    cnt = (hi - lo + 1).astype(jnp.float32).reshape((n,) + (1,) * (x.ndim - axis - 1))
    return (s / cnt).astype(x.dtype)


def multiscale_pool(v, pool_w, pool_scale, on_grid):
    b, n, _ = v.shape
    outs = []
    for gi, w in enumerate(POOL_WINDOWS):
        vg = v[..., gi * POOL_GROUP:(gi + 1) * POOL_GROUP]
        if on_grid:
            rows = n // GRID_W
            m = box_mean(box_mean(vg.reshape(b, rows, GRID_W, POOL_GROUP), w, 1), w, 2)
            m = m.reshape(b, n, POOL_GROUP)
        else:
            m = box_mean(vg, w, 1)
        outs.append(m - vg)
    pooled = jnp.stack(outs, axis=2)
    mixed = jnp.einsum('bngi,gio->bngo', pooled, pool_w).reshape(b, n, POOL_W)
    return mixed * pool_scale


def depthwise_conv3(u, w, bias):
    n = u.shape[1]
    up = jnp.pad(u, ((0, 0), (1, 1), (0, 0)))
    return up[:, :n] * w[0] + up[:, 1:n + 1] * w[1] + up[:, 2:] * w[2] + bias


def pool_conv_mixer(h, w_in, pool_w, pool_scale, conv_w, conv_b, w_out, on_grid):
    idx = [int(i) for i in np.cumsum([POOL_W, POOL_W, CONV_W, CONV_W, CONV_W])]
    a_v, a_g, b_x, b_b, b_c, b_g = jnp.split(h @ w_in, idx, axis=-1)
    a_out = multiscale_pool(a_v, pool_w, pool_scale, on_grid) * jax.nn.silu(a_g)
    b_out = b_b * depthwise_conv3(b_c * b_x, conv_w, conv_b) * jax.nn.silu(b_g)
    return jnp.concatenate([a_out, b_out], axis=-1) @ w_out


def _heads(t, d):
    return t.reshape(t.shape[0], t.shape[1], HG_HEADS, d)


def _chunks(t):
    b, n, hh, d = t.shape
    return t.reshape(b, n // CHUNK, CHUNK, hh, d)


def _rev(t, reverse):
    return jnp.flip(t, axis=1) if reverse else t


def hgrn2_gates(z, lb):
    zf = z.astype(jnp.float32)
    f = lb + (1.0 - lb) * jax.nn.sigmoid(zf)
    k = (1.0 - lb) * jax.nn.sigmoid(-zf)
    return _heads(k, HG_DK), _heads(jnp.log(f), HG_DK)


def chunk_state_scan(kc, vc, bc, s0, keep_starts):
    b_last = bc[:, :, -1]
    k_dec = kc * jnp.exp(b_last[:, :, None] - bc)
    ds = jnp.einsum('bnchk,bnchv->bnhkv', k_dec, vc)
    decay = jnp.exp(b_last)

    def step(s, inp):
        d, dsn = inp
        return d[..., None] * s + dsn, (s if keep_starts else None)

    s_fin, starts = lax.scan(step, s0, (jnp.moveaxis(decay, 1, 0), jnp.moveaxis(ds, 1, 0)))
    return s_fin, starts


def hgrn2_chunk_scan(q, k, v, logf, s0):
    qc, kc, vc = _chunks(q), _chunks(k), _chunks(v)
    bc = jnp.cumsum(_chunks(logf), axis=2)
    s_fin, starts = chunk_state_scan(kc, vc, bc, s0, True)
    starts = jnp.moveaxis(starts, 0, 1)
    q_dec = qc * jnp.exp(bc)
    k_inv = kc * jnp.exp(-bc)
    inter = jnp.einsum('bnchk,bnhkv->bnchv', q_dec, starts)
    scores = jnp.einsum('bnchk,bnshk->bnhcs', q_dec, k_inv)
    mask = jnp.tril(jnp.ones((CHUNK, CHUNK), dtype=bool))
    scores = jnp.where(mask, scores, 0.0)
    intra = jnp.einsum('bnhcs,bnshv->bnchv', scores, vc)
    o = (inter + intra).reshape(q.shape[0], q.shape[1], HG_HEADS, HG_DV)
    return o, s_fin


def hgrn2_final_state(k, v, logf, s0):
    bc = jnp.cumsum(_chunks(logf), axis=2)
    s_fin, _ = chunk_state_scan(_chunks(k), _chunks(v), bc, s0, False)
    return s_fin


def hgrn2_readout(o, g, onorm_g, w_out):
    b, n = o.shape[:2]
    o = o * lax.rsqrt(jnp.mean(o * o, axis=-1, keepdims=True) + EPS)
    o = o.reshape(b, n, HG_V) * onorm_g.astype(jnp.float32) * jax.nn.silu(g.astype(jnp.float32))
    return o.astype(g.dtype) @ w_out


def hgrn2_mixer(h_lat, h_ctx, w_in, onorm_g, w_out, lb_fwd, lb_bwd, ctx_out):
    idx = [int(i) for i in np.cumsum([HG_K, HG_K, HG_V, HG_K])]
    zf_l, zb_l, i_l, q_l, g_l = jnp.split(h_lat @ w_in, idx, axis=-1)
    if ctx_out:
        zf_c, zb_c, i_c, q_c, g_c = jnp.split(h_ctx @ w_in, idx, axis=-1)
        q_c = _heads(q_c.astype(jnp.float32), HG_DK)
    else:
        zf_c, zb_c, i_c = jnp.split(h_ctx @ w_in[:, :idx[2]], idx[:2], axis=-1)
    q_l = _heads(q_l.astype(jnp.float32), HG_DK)
    v_l = _heads(i_l.astype(jnp.float32), HG_DV)
    v_c = _heads(i_c.astype(jnp.float32), HG_DV)
    s0 = jnp.zeros((h_lat.shape[0], HG_HEADS, HG_DK, HG_DV), jnp.float32)
    outs_l, outs_c = [], []
    for lb, z_l, z_c, reverse in ((lb_fwd, zf_l, zf_c, False), (lb_bwd, zb_l, zb_c, True)):
        k_l, lf_l = hgrn2_gates(z_l, lb)
        k_c, lf_c = hgrn2_gates(z_c, lb)
        if ctx_out:
            o_c, s_ctx = hgrn2_chunk_scan(_rev(q_c, reverse), _rev(k_c, reverse),
                                          _rev(v_c, reverse), _rev(lf_c, reverse), s0)
            outs_c.append(_rev(o_c, reverse))
        else:
            s_ctx = hgrn2_final_state(_rev(k_c, reverse), _rev(v_c, reverse),
                                      _rev(lf_c, reverse), s0)
        o_l, _ = hgrn2_chunk_scan(_rev(q_l, reverse), _rev(k_l, reverse),
                                  _rev(v_l, reverse), _rev(lf_l, reverse), s_ctx)
        outs_l.append(_rev(o_l, reverse))
    y_l = hgrn2_readout(outs_l[0] + outs_l[1], g_l, onorm_g, w_out)
    y_c = hgrn2_readout(outs_c[0] + outs_c[1], g_c, onorm_g, w_out) if ctx_out else None
    return y_l, y_c


def setup_inputs(seed: int = 0) -> dict:
    key = jax.random.key(seed)
    ks = jax.random.split(key, 18)
    D = D_MODEL

    def nrm(k, shape, s):
        return jax.random.normal(k, shape, jnp.float32) * s

    return {
        "x": nrm(ks[0], (BATCH, SEQ, D), 1.0),
        "c": nrm(ks[1], (BATCH, D), 1.0),
        "ctx": nrm(ks[2], (BATCH, CTX_LEN, D), 1.0),
        "c_ctx": nrm(ks[3], (D,), 1.0),
        "ada_w": nrm(ks[4], (DEPTH, D, 3 * D), 0.5 * D ** -0.5),
        "ada_b": nrm(ks[5], (DEPTH, 3 * D), 0.01),
        "pre_g": 1.0 + nrm(ks[6], (DEPTH, D), 0.05),
        "post_g": 1.0 + nrm(ks[7], (DEPTH, D), 0.05),
        "ev_w_in": nrm(ks[8], (N_EVEN, D, EVEN_IN), D ** -0.5),
        "ev_pool_w": nrm(ks[9], (N_EVEN, len(POOL_WINDOWS), POOL_GROUP, POOL_GROUP), POOL_GROUP ** -0.5),
        "ev_pool_scale": 1.0 + nrm(ks[10], (N_EVEN, POOL_W), 0.1),
        "ev_conv_w": nrm(ks[11], (N_EVEN, CONV_K, CONV_W), CONV_K ** -0.5),
        "ev_conv_b": nrm(ks[12], (N_EVEN, CONV_W), 0.01),
        "ev_w_out": nrm(ks[13], (N_EVEN, POOL_W + CONV_W, D), (POOL_W + CONV_W) ** -0.5),
        "od_w_in": nrm(ks[14], (N_ODD, D, ODD_IN), D ** -0.5),
        "od_onorm_g": 1.0 + nrm(ks[15], (N_ODD, HG_V), 0.05),
        "od_w_out": nrm(ks[16], (N_ODD, HG_V, D), HG_V ** -0.5),
        "lb_logits": nrm(ks[17], (2, DEPTH + 1, HG_K), 0.1),
    }


def reference(x, c, ctx, c_ctx, ada_w, ada_b, pre_g, post_g, ev_w_in, ev_pool_w,
              ev_pool_scale, ev_conv_w, ev_conv_b, ev_w_out, od_w_in, od_onorm_g,
              od_w_out, lb_logits):
    lb_table = jnp.cumsum(jax.nn.softmax(lb_logits.astype(jnp.float32), axis=1), axis=1)
    s_lat = jax.nn.silu(c)
    s_ctx = jax.nn.silu(c_ctx)
    for layer in range(DEPTH):
        last = layer == DEPTH - 1
        shift, scale, gate = jnp.split((s_lat @ ada_w[layer] + ada_b[layer])[:, None, :], 3, axis=-1)
        h = rms_norm(x, pre_g[layer]) * (1.0 + scale) + shift
        need_ctx = (layer % 2 == 1) or not last
        if need_ctx:
            shift_c, scale_c, gate_c = jnp.split(s_ctx @ ada_w[layer] + ada_b[layer], 3)
            hc = rms_norm(ctx, pre_g[layer]) * (1.0 + scale_c) + shift_c
        if layer % 2 == 0:
            e = layer // 2
            y = pool_conv_mixer(h, ev_w_in[e], ev_pool_w[e], ev_pool_scale[e], ev_conv_w[e],
                                ev_conv_b[e], ev_w_out[e], True)
            if not last:
                yc = pool_conv_mixer(hc, ev_w_in[e], ev_pool_w[e], ev_pool_scale[e], ev_conv_w[e],
                                     ev_conv_b[e], ev_w_out[e], False)
        else:
            o = layer // 2
            y, yc = hgrn2_mixer(h, hc, od_w_in[o], od_onorm_g[o], od_w_out[o],
                                lb_table[0, layer], lb_table[1, layer], not last)
        x = x + gate * rms_norm(y, post_g[layer])
        if not last:
            ctx = ctx + gate_c * rms_norm(yc, post_g[layer])
    return x
```

```python
import functools

import numpy as np
import jax
import jax.numpy as jnp
from jax import lax
from jax.experimental import pallas as pl
from jax.experimental.pallas import tpu as pltpu

F32 = jnp.float32
BF16 = jnp.bfloat16

D_MODEL = 2048
BATCH = 4
SEQ = 2048
CTX_LEN = 256
GRID_W = 64
EPS = 1e-6

POOL_WINDOWS = (2, 4, 8, 16)
POOL_W = D_MODEL // 2
POOL_GROUP = POOL_W // len(POOL_WINDOWS)
CONV_W = D_MODEL // 2
EVEN_IN = 2 * POOL_W + 4 * CONV_W

HG_DK = 128
HG_HEADS = D_MODEL // HG_DK
HG_DV = D_MODEL // HG_HEADS
HG_K = HG_HEADS * HG_DK
HG_V = HG_HEADS * HG_DV
ODD_IN = 3 * HG_K + 2 * HG_V
CHUNK = 64

MOD_ROWS = 8
CTX_ROW = BATCH
VMEM_LIMIT = 56 * 1024 * 1024


def _silu(v):
    return v * jax.nn.sigmoid(v)


def _ada_kernel(cc_ref, w_ref, b_ref, o_ref):
    s = _silu(cc_ref[...])
    o_ref[...] = jnp.dot(s, w_ref[...], preferred_element_type=F32) + b_ref[...]


def _ada_table(cc, ada_w, ada_b):
    depth = ada_w.shape[0]
    tn = 1024
    return pl.pallas_call(
        _ada_kernel,
        grid=(depth, 3 * D_MODEL // tn),
        in_specs=[
            pl.BlockSpec((MOD_ROWS, D_MODEL), lambda l, j: (0, 0)),
            pl.BlockSpec((None, D_MODEL, tn), lambda l, j: (l, 0, j)),
            pl.BlockSpec((None, 1, tn), lambda l, j: (l, 0, j)),
        ],
        out_specs=pl.BlockSpec((None, MOD_ROWS, tn), lambda l, j: (l, 0, j)),
        out_shape=jax.ShapeDtypeStruct((depth, MOD_ROWS, 3 * D_MODEL), F32),
        compiler_params=pltpu.CompilerParams(
            dimension_semantics=("arbitrary", "arbitrary"), vmem_limit_bytes=VMEM_LIMIT),
        name="ada_table",
    )(cc, ada_w, ada_b.reshape(depth, 1, 3 * D_MODEL))


def _inproj_kernel(x_ref, shift_ref, scale_ref, g_ref, w_ref, o_ref, h_ref):
    @pl.when(pl.program_id(2) == 0)
    def _():
        x = x_ref[...]
        ms = jnp.mean(x * x, axis=-1, keepdims=True)
        y = x * lax.rsqrt(ms + EPS) * g_ref[...]
        h_ref[...] = (y * (1.0 + scale_ref[...]) + shift_ref[...]).astype(BF16)

    o_ref[...] = jnp.dot(h_ref[...], w_ref[...],
                         preferred_element_type=F32).astype(o_ref.dtype)


def _inproj(x3, mod_row_of_batch, shift, scale, g, w, n_out, out_dtype, tm, tn):
    nb, rows, _ = x3.shape
    mod_map = lambda b, i, j: (mod_row_of_batch(b), 0, 0)
    return pl.pallas_call(
        _inproj_kernel,
        grid=(nb, rows // tm, n_out // tn),
        in_specs=[
            pl.BlockSpec((None, tm, D_MODEL), lambda b, i, j: (b, i, 0)),
            pl.BlockSpec((None, 1, D_MODEL), mod_map),
            pl.BlockSpec((None, 1, D_MODEL), mod_map),
            pl.BlockSpec((1, D_MODEL), lambda b, i, j: (0, 0)),
            pl.BlockSpec((D_MODEL, tn), lambda b, i, j: (0, j)),
        ],
        out_specs=pl.BlockSpec((None, tm, tn), lambda b, i, j: (b, i, j)),
        out_shape=jax.ShapeDtypeStruct((nb, rows, n_out), out_dtype),
        scratch_shapes=[pltpu.VMEM((tm, D_MODEL), BF16)],
        compiler_params=pltpu.CompilerParams(
            dimension_semantics=("arbitrary", "arbitrary", "arbitrary"),
            vmem_limit_bytes=VMEM_LIMIT),
        name="inproj",
    )(x3, shift, scale, g, w)


def _pool_kernel(p_ref, icnt_ref, v_ref, gate_ref, w_ref, sc_ref, o_ref):
    v = v_ref[...]
    box = jnp.dot(p_ref[...], v, preferred_element_type=F32)
    pooled = box * icnt_ref[...] - v.astype(F32)
    mixed = jnp.dot(pooled.astype(BF16), w_ref[...], preferred_element_type=F32)
    o_ref[...] = (mixed * sc_ref[...] * _silu(gate_ref[...].astype(F32))).astype(o_ref.dtype)


def _pool_mixer(proj, p01, icnt, pool_w, pool_scale):
    nb, n, _ = proj.shape
    ng = len(POOL_WINDOWS)
    g = POOL_GROUP
    return pl.pallas_call(
        _pool_kernel,
        grid=(ng, nb),
        in_specs=[
            pl.BlockSpec((None, n, n), lambda gi, b: (gi, 0, 0)),
            pl.BlockSpec((None, n, 1), lambda gi, b: (gi, 0, 0)),
            pl.BlockSpec((None, n, g), lambda gi, b: (b, 0, gi)),
            pl.BlockSpec((None, n, g), lambda gi, b: (b, 0, ng + gi)),
            pl.BlockSpec((None, g, g), lambda gi, b: (gi, 0, 0)),
            pl.BlockSpec((1, g), lambda gi, b: (0, gi)),
        ],
        out_specs=pl.BlockSpec((None, n, g), lambda gi, b: (b, 0, gi)),
        out_shape=jax.ShapeDtypeStruct((nb, n, POOL_W), BF16),
        compiler_params=pltpu.CompilerParams(
            dimension_semantics=("arbitrary", "arbitrary"), vmem_limit_bytes=VMEM_LIMIT),
        name="pool_mixer",
    )(p01, icnt, proj, proj, pool_w, pool_scale)


def _conv_kernel(x_ref, b_ref, c_ref, gate_ref, cw_ref, cb_ref, o_ref):
    n = x_ref.shape[0]
    u = c_ref[...].astype(F32) * x_ref[...].astype(F32)
    row = lax.broadcasted_iota(jnp.int32, u.shape, 0)
    u_prev = jnp.where(row == 0, 0.0, pltpu.roll(u, 1, axis=0))
    u_next = jnp.where(row == n - 1, 0.0, pltpu.roll(u, n - 1, axis=0))
    cw = cw_ref[...]
    conv = u_prev * cw[0:1, :] + u * cw[1:2, :] + u_next * cw[2:3, :] + cb_ref[...]
    o_ref[...] = (b_ref[...].astype(F32) * conv
                  * _silu(gate_ref[...].astype(F32))).astype(o_ref.dtype)


def _conv_mixer(proj, conv_w, conv_b):
    nb, n, _ = proj.shape
    tc = 256
    nblk = CONV_W // tc
    base = 2 * POOL_W // tc

    def seg(k):
        return pl.BlockSpec((None, n, tc), lambda b, j: (b, 0, base + k * nblk + j))

    return pl.pallas_call(
        _conv_kernel,
        grid=(nb, nblk),
        in_specs=[
            seg(0), seg(1), seg(2), seg(3),
            pl.BlockSpec((3, tc), lambda b, j: (0, j)),
            pl.BlockSpec((1, tc), lambda b, j: (0, j)),
        ],
        out_specs=pl.BlockSpec((None, n, tc), lambda b, j: (b, 0, j)),
        out_shape=jax.ShapeDtypeStruct((nb, n, CONV_W), BF16),
        compiler_params=pltpu.CompilerParams(
            dimension_semantics=("arbitrary", "arbitrary"), vmem_limit_bytes=VMEM_LIMIT),
        name="conv_mixer",
    )(proj, proj, proj, proj, conv_w, conv_b)


def _outproj_kernel(n_in, *refs):
    a_refs = refs[:n_in]
    w_refs = refs[n_in:2 * n_in]
    x_ref, gate_ref, pg_ref, o_ref = refs[2 * n_in:]
    y = jnp.dot(a_refs[0][...], w_refs[0][...], preferred_element_type=F32)
    for a_ref, w_ref in zip(a_refs[1:], w_refs[1:]):
        y = y + jnp.dot(a_ref[...], w_ref[...], preferred_element_type=F32)
    ms = jnp.mean(y * y, axis=-1, keepdims=True)
    r = y * lax.rsqrt(ms + EPS) * pg_ref[...]
    o_ref[...] = x_ref[...] + gate_ref[...] * r


def _outproj(acts, ws, x3, mod_row_of_batch, gate, pg, tm):
    nb, rows, _ = x3.shape
    n_in = len(acts)
    in_specs = [pl.BlockSpec((None, tm, a.shape[-1]), lambda b, i: (b, i, 0)) for a in acts]
    in_specs += [pl.BlockSpec(w.shape, lambda b, i: (0, 0)) for w in ws]
    in_specs += [
        pl.BlockSpec((None, tm, D_MODEL), lambda b, i: (b, i, 0)),
        pl.BlockSpec((None, 1, D_MODEL), lambda b, i: (mod_row_of_batch(b), 0, 0)),
        pl.BlockSpec((1, D_MODEL), lambda b, i: (0, 0)),
    ]
    return pl.pallas_call(
        functools.partial(_outproj_kernel, n_in),
        grid=(nb, rows // tm),
        in_specs=in_specs,
        out_specs=pl.BlockSpec((None, tm, D_MODEL), lambda b, i: (b, i, 0)),
        out_shape=jax.ShapeDtypeStruct((nb, rows, D_MODEL), F32),
        compiler_params=pltpu.CompilerParams(
            dimension_semantics=("arbitrary", "arbitrary"), vmem_limit_bytes=VMEM_LIMIT),
        name="outproj",
    )(*acts, *ws, x3, gate, pg)


def _split3_dot(t01, v):
    hi = v.astype(BF16)
    r1 = v - hi.astype(F32)
    mid = r1.astype(BF16)
    lo = (r1 - mid.astype(F32)).astype(BF16)
    return (jnp.dot(t01, hi, preferred_element_type=F32)
            + jnp.dot(t01, mid, preferred_element_type=F32)
            + jnp.dot(t01, lo, preferred_element_type=F32))


_NT = (((1,), (1,)), ((), ()))
_TN = (((0,), (0,)), ((), ()))


def _scan_kernel(zf_ref, zb_ref, i_ref, q_ref, g_ref, czf_ref, czb_ref, ci_ref,
                 lbf_ref, lbb_ref, og_ref, o_ref, stf_ref, stb_ref, oacc_ref):
    c = CHUNK
    n_lat = zf_ref.shape[0] // c
    n_ctx = czf_ref.shape[0] // c
    row = lax.broadcasted_iota(jnp.int32, (c, c), 0)
    col = lax.broadcasted_iota(jnp.int32, (c, c), 1)
    m_fwd = col <= row
    m_bwd = col >= row
    t_fwd = m_fwd.astype(BF16)
    t_bwd = m_bwd.astype(BF16)
    lb_f = lbf_ref[...]
    lb_b = lbb_ref[...]

    def gates(z, lb):
        f = lb + (1.0 - lb) * jax.nn.sigmoid(z)
        k = (1.0 - lb) * jax.nn.sigmoid(-z)
        return k, jnp.log(f)

    def decays(z, lb, t01, last):
        k, lf = gates(z, lb)
        bc = _split3_dot(t01, lf)
        b_last = bc[last:last + 1, :]
        return k, bc, b_last

    def new_state(st_ref, k, bc, b_last, v):
        k_dec = (k * jnp.exp(b_last - bc)).astype(BF16)
        ds_t = lax.dot_general(v, k_dec, _TN, preferred_element_type=F32)
        st_ref[...] = st_ref[...] * jnp.exp(b_last) + ds_t

    def ctx_step(st_ref, z_ref, lb, t01, last, j):
        sl = pl.ds(j * c, c)
        k, bc, b_last = decays(z_ref[sl, :], lb, t01, last)
        new_state(st_ref, k, bc, b_last, ci_ref[sl, :].astype(BF16))

    def lat_step(st_ref, z_ref, lb, t01, mask, last, start):
        sl = pl.ds(start, c)
        k, bc, b_last = decays(z_ref[sl, :], lb, t01, last)
        v = i_ref[sl, :].astype(BF16)
        q_dec = (q_ref[sl, :] * jnp.exp(bc)).astype(BF16)
        k_inv = (k * jnp.exp(-bc)).astype(BF16)
        inter = lax.dot_general(q_dec, st_ref[...].astype(BF16), _NT,
                                preferred_element_type=F32)
        sc = lax.dot_general(q_dec, k_inv, _NT, preferred_element_type=F32)
        sc = jnp.where(mask, sc, 0.0).astype(BF16)
        intra = jnp.dot(sc, v, preferred_element_type=F32)
        oacc_ref[sl, :] += inter + intra
        new_state(st_ref, k, bc, b_last, v)

    stf_ref[...] = jnp.zeros_like(stf_ref)
    stb_ref[...] = jnp.zeros_like(stb_ref)
    oacc_ref[...] = jnp.zeros_like(oacc_ref)

    for j in range(n_ctx):
        ctx_step(stf_ref, czf_ref, lb_f, t_fwd, c - 1, j)
        ctx_step(stb_ref, czb_ref, lb_b, t_bwd, 0, n_ctx - 1 - j)

    def body(n, carry):
        lat_step(stf_ref, zf_ref, lb_f, t_fwd, m_fwd, c - 1, pl.multiple_of(n * c, c))
        lat_step(stb_ref, zb_ref, lb_b, t_bwd, m_bwd, 0,
                 pl.multiple_of((n_lat - 1 - n) * c, c))
        return carry

    lax.fori_loop(0, n_lat, body, 0)

    o = oacc_ref[...]
    o = o * lax.rsqrt(jnp.mean(o * o, axis=-1, keepdims=True) + EPS)
    o_ref[...] = (o * og_ref[...] * _silu(g_ref[...])).astype(o_ref.dtype)


def _hgrn2_scan(proj_lat, proj_ctx, lb_f, lb_b, onorm_g):
    nb, n, _ = proj_lat.shape
    nc = proj_ctx.shape[1]
    hh = HG_HEADS

    def lat(k):
        return pl.BlockSpec((None, n, HG_DK), lambda b, h: (b, 0, k * hh + h))

    def ctx(k):
        return pl.BlockSpec((None, nc, HG_DK), lambda b, h: (b, 0, k * hh + h))

    vec = pl.BlockSpec((1, HG_DK), lambda b, h: (0, h))
    return pl.pallas_call(
        _scan_kernel,
        grid=(nb, hh),
        in_specs=[lat(0), lat(1), lat(2), lat(3), lat(4), ctx(0), ctx(1), ctx(2),
                  vec, vec, vec],
        out_specs=pl.BlockSpec((None, n, HG_DV), lambda b, h: (b, 0, h)),
        out_shape=jax.ShapeDtypeStruct((nb, n, HG_V), BF16),
        scratch_shapes=[pltpu.VMEM((HG_DV, HG_DK), F32),
                        pltpu.VMEM((HG_DV, HG_DK), F32),
                        pltpu.VMEM((n, HG_DV), F32)],
        compiler_params=pltpu.CompilerParams(
            dimension_semantics=("arbitrary", "arbitrary"), vmem_limit_bytes=VMEM_LIMIT),
        name="hgrn2_scan",
    )(proj_lat, proj_lat, proj_lat, proj_lat, proj_lat, proj_ctx, proj_ctx, proj_ctx,
      lb_f, lb_b, onorm_g)


def _window(n, w):
    t = np.arange(n)
    lo = np.maximum(t - w // 2, 0)
    hi = np.minimum(t + w // 2 - 1, n - 1)
    s = np.arange(n)
    inside = (s[None, :] >= lo[:, None]) & (s[None, :] <= hi[:, None])
    return inside, (hi - lo + 1)


def _pool_operators(n, on_grid):
    mats, icnts = [], []
    for w in POOL_WINDOWS:
        if on_grid:
            rows = n // GRID_W
            in_r, cnt_r = _window(rows, w)
            in_c, cnt_c = _window(GRID_W, w)
            m = (in_r[:, None, :, None] & in_c[None, :, None, :]).reshape(n, n)
            cnt = (cnt_r[:, None] * cnt_c[None, :]).reshape(n)
        else:
            m, cnt = _window(n, w)
        mats.append(m)
        icnts.append(1.0 / cnt.astype(np.float64))
    p01 = jnp.asarray(np.stack(mats).astype(np.float32), dtype=BF16)
    icnt = jnp.asarray(np.stack(icnts).astype(np.float32)[..., None])
    return p01, icnt


def kernel(x, c, ctx, c_ctx, ada_w, ada_b, pre_g, post_g, ev_w_in, ev_pool_w, ev_pool_scale,
           ev_conv_w, ev_conv_b, ev_w_out, od_w_in, od_onorm_g, od_w_out, lb_logits):
    nb, n, d = x.shape
    nc = ctx.shape[1]
    lat_row = lambda b: b
    ctx_row = lambda b: CTX_ROW

    lb_table = jnp.cumsum(jax.nn.softmax(lb_logits.astype(F32), axis=1), axis=1)

    cc = jnp.zeros((MOD_ROWS, d), F32).at[:nb].set(c).at[CTX_ROW].set(c_ctx)
    mod = _ada_table(cc, ada_w, ada_b)
    mod = mod.reshape(2, MOD_ROWS, 3, 1, d)
    shift = [mod[l, :, 0] for l in range(2)]
    scale = [mod[l, :, 1] for l in range(2)]
    gate = [mod[l, :, 2] for l in range(2)]

    ctx_flat = ctx.reshape(1, nb * nc, d)

    w_in0 = ev_w_in[0].astype(BF16)
    w_out0 = ev_w_out[0].astype(BF16)
    pool_w = ev_pool_w[0].astype(BF16)
    pool_scale = ev_pool_scale[0].reshape(1, POOL_W)
    conv_w = ev_conv_w[0]
    conv_b = ev_conv_b[0].reshape(1, CONV_W)
    pre0 = pre_g[0].reshape(1, d)
    post0 = post_g[0].reshape(1, d)

    def even_mixer(tokens3, mod_row, on_grid, tm_in, tm_out):
        b3, n3, _ = tokens3.shape
        proj = _inproj(tokens3, mod_row, shift[0], scale[0], pre0, w_in0, EVEN_IN, BF16,
                       tm_in, 1024)
        n_tok = n if on_grid else nc
        proj = proj.reshape(nb, n_tok, EVEN_IN)
        p01, icnt = _pool_operators(n_tok, on_grid)
        a_out = _pool_mixer(proj, p01, icnt, pool_w, pool_scale)
        b_out = _conv_mixer(proj, conv_w, conv_b)
        acts = [a_out.reshape(b3, n3, POOL_W), b_out.reshape(b3, n3, CONV_W)]
        return _outproj(acts, [w_out0[:POOL_W], w_out0[POOL_W:]], tokens3, mod_row,
                        gate[0], post0, tm_out)

    x1 = even_mixer(x, lat_row, True, 1024, 512)
    ctx1 = even_mixer(ctx_flat, ctx_row, False, 1024, 512)

    w_in1 = od_w_in[0].astype(BF16)
    w_out1 = od_w_out[0].astype(BF16)
    pre1 = pre_g[1].reshape(1, d)
    post1 = post_g[1].reshape(1, d)
    proj_lat = _inproj(x1, lat_row, shift[1], scale[1], pre1, w_in1, ODD_IN, F32, 1024, 1024)
    proj_ctx = _inproj(ctx1, ctx_row, shift[1], scale[1], pre1, w_in1, 3 * HG_K, F32,
                       1024, 1024)
    proj_ctx = proj_ctx.reshape(nb, nc, 3 * HG_K)
    o = _hgrn2_scan(proj_lat, proj_ctx, lb_table[0, 1].reshape(1, HG_K),
                    lb_table[1, 1].reshape(1, HG_K), od_onorm_g[0].reshape(1, HG_V))
    return _outproj([o], [w_out1], x1, lat_row, gate[1], post1, 512)
```

```python
import functools

import numpy as np
import jax
import jax.numpy as jnp
from jax import lax
from jax.experimental import pallas as pl
from jax.experimental.pallas import tpu as pltpu

F32 = jnp.float32
BF16 = jnp.bfloat16

D_MODEL = 2048
BATCH = 4
SEQ = 2048
CTX_LEN = 256
GRID_W = 64
EPS = 1e-6

POOL_WINDOWS = (2, 4, 8, 16)
POOL_W = D_MODEL // 2
POOL_GROUP = POOL_W // len(POOL_WINDOWS)
CONV_W = D_MODEL // 2
EVEN_IN = 2 * POOL_W + 4 * CONV_W

HG_DK = 128
HG_HEADS = D_MODEL // HG_DK
HG_DV = D_MODEL // HG_HEADS
HG_K = HG_HEADS * HG_DK
HG_V = HG_HEADS * HG_DV
ODD_IN = 3 * HG_K + 2 * HG_V
CHUNK = 64

MOD_ROWS = 8
CTX_ROW = BATCH
VMEM_LIMIT = 56 * 1024 * 1024


def _silu(v):
    return v * jax.nn.sigmoid(v)


def _ada_kernel(cc_ref, w_ref, b_ref, o_ref):
    s = _silu(cc_ref[...])
    o_ref[...] = jnp.dot(s, w_ref[...], preferred_element_type=F32) + b_ref[...]


def _ada_table(cc, ada_w, ada_b):
    depth = ada_w.shape[0]
    tn = 1024
    return pl.pallas_call(
        _ada_kernel,
        grid=(depth, 3 * D_MODEL // tn),
        in_specs=[
            pl.BlockSpec((MOD_ROWS, D_MODEL), lambda l, j: (0, 0)),
            pl.BlockSpec((None, D_MODEL, tn), lambda l, j: (l, 0, j)),
            pl.BlockSpec((None, 1, tn), lambda l, j: (l, 0, j)),
        ],
        out_specs=pl.BlockSpec((None, MOD_ROWS, tn), lambda l, j: (l, 0, j)),
        out_shape=jax.ShapeDtypeStruct((depth, MOD_ROWS, 3 * D_MODEL), F32),
        compiler_params=pltpu.CompilerParams(
            dimension_semantics=("arbitrary", "arbitrary"), vmem_limit_bytes=VMEM_LIMIT),
        name="ada_table",
    )(cc, ada_w, ada_b.reshape(depth, 1, 3 * D_MODEL))


def _inproj_kernel(n_seg, x_ref, shift_ref, scale_ref, g_ref, *refs):
    w_refs = refs[:n_seg]
    o_refs = refs[n_seg:2 * n_seg]
    h_ref = refs[2 * n_seg]

    @pl.when(pl.program_id(2) == 0)
    def _():
        x = x_ref[...]
        ms = jnp.mean(x * x, axis=-1, keepdims=True)
        y = x * lax.rsqrt(ms + EPS) * g_ref[...]
        h_ref[...] = (y * (1.0 + scale_ref[...]) + shift_ref[...]).astype(BF16)

    for w_ref, o_ref in zip(w_refs, o_refs):
        o_ref[...] = jnp.dot(h_ref[...], w_ref[...],
                             preferred_element_type=F32).astype(o_ref.dtype)


def _inproj(x3, mod_row_of_batch, shift, scale, g, w, seg_width, seg_dtypes, tm, tn):
    nb, rows, _ = x3.shape
    n_seg = len(seg_dtypes)
    n_tiles = seg_width // tn
    mod_map = lambda b, i, j: (mod_row_of_batch(b), 0, 0)
    w_specs = [pl.BlockSpec((D_MODEL, tn), functools.partial(
        lambda k, b, i, j: (0, k * n_tiles + j), k)) for k in range(n_seg)]
    return pl.pallas_call(
        functools.partial(_inproj_kernel, n_seg),
        grid=(nb, rows // tm, n_tiles),
        in_specs=[
            pl.BlockSpec((None, tm, D_MODEL), lambda b, i, j: (b, i, 0)),
            pl.BlockSpec((None, 1, D_MODEL), mod_map),
            pl.BlockSpec((None, 1, D_MODEL), mod_map),
            pl.BlockSpec((1, D_MODEL), lambda b, i, j: (0, 0)),
        ] + w_specs,
        out_specs=[pl.BlockSpec((None, tm, tn), lambda b, i, j: (b, i, j))] * n_seg,
        out_shape=[jax.ShapeDtypeStruct((nb, rows, seg_width), dt) for dt in seg_dtypes],
        scratch_shapes=[pltpu.VMEM((tm, D_MODEL), BF16)],
        compiler_params=pltpu.CompilerParams(
            dimension_semantics=("arbitrary", "arbitrary", "arbitrary"),
            vmem_limit_bytes=VMEM_LIMIT),
        name="inproj",
    )(x3, shift, scale, g, *([w] * n_seg))


def _pool_kernel(p_ref, icnt_ref, v_ref, gate_ref, w_ref, sc_ref, o_ref):
    v = v_ref[...]
    box = jnp.dot(p_ref[...], v, preferred_element_type=F32)
    pooled = box * icnt_ref[...] - v.astype(F32)
    mixed = jnp.dot(pooled.astype(BF16), w_ref[...], preferred_element_type=F32)
    o_ref[...] = (mixed * sc_ref[...] * _silu(gate_ref[...].astype(F32))).astype(o_ref.dtype)


def _pool_mixer(proj, p01, icnt, pool_w, pool_scale):
    nb, n, _ = proj.shape
    ng = len(POOL_WINDOWS)
    g = POOL_GROUP
    return pl.pallas_call(
        _pool_kernel,
        grid=(ng, nb),
        in_specs=[
            pl.BlockSpec((None, n, n), lambda gi, b: (gi, 0, 0)),
            pl.BlockSpec((None, n, 1), lambda gi, b: (gi, 0, 0)),
            pl.BlockSpec((None, n, g), lambda gi, b: (b, 0, gi)),
            pl.BlockSpec((None, n, g), lambda gi, b: (b, 0, ng + gi)),
            pl.BlockSpec((None, g, g), lambda gi, b: (gi, 0, 0)),
            pl.BlockSpec((1, g), lambda gi, b: (0, gi)),
        ],
        out_specs=pl.BlockSpec((None, n, g), lambda gi, b: (b, 0, gi)),
        out_shape=jax.ShapeDtypeStruct((nb, n, POOL_W), BF16),
        compiler_params=pltpu.CompilerParams(
            dimension_semantics=("arbitrary", "arbitrary"), vmem_limit_bytes=VMEM_LIMIT),
        name="pool_mixer",
    )(p01, icnt, proj, proj, pool_w, pool_scale)


def _conv_kernel(x_ref, b_ref, c_ref, gate_ref, cw_ref, cb_ref, o_ref):
    n = x_ref.shape[0]
    u = c_ref[...].astype(F32) * x_ref[...].astype(F32)
    row = lax.broadcasted_iota(jnp.int32, u.shape, 0)
    u_prev = jnp.where(row == 0, 0.0, pltpu.roll(u, 1, axis=0))
    u_next = jnp.where(row == n - 1, 0.0, pltpu.roll(u, n - 1, axis=0))
    cw = cw_ref[...]
    conv = u_prev * cw[0:1, :] + u * cw[1:2, :] + u_next * cw[2:3, :] + cb_ref[...]
    o_ref[...] = (b_ref[...].astype(F32) * conv
                  * _silu(gate_ref[...].astype(F32))).astype(o_ref.dtype)


def _conv_mixer(proj, conv_w, conv_b):
    nb, n, _ = proj.shape
    tc = 256
    nblk = CONV_W // tc
    base = 2 * POOL_W // tc

    def seg(k):
        return pl.BlockSpec((None, n, tc), lambda b, j: (b, 0, base + k * nblk + j))

    return pl.pallas_call(
        _conv_kernel,
        grid=(nb, nblk),
        in_specs=[
            seg(0), seg(1), seg(2), seg(3),
            pl.BlockSpec((3, tc), lambda b, j: (0, j)),
            pl.BlockSpec((1, tc), lambda b, j: (0, j)),
        ],
        out_specs=pl.BlockSpec((None, n, tc), lambda b, j: (b, 0, j)),
        out_shape=jax.ShapeDtypeStruct((nb, n, CONV_W), BF16),
        compiler_params=pltpu.CompilerParams(
            dimension_semantics=("arbitrary", "arbitrary"), vmem_limit_bytes=VMEM_LIMIT),
        name="conv_mixer",
    )(proj, proj, proj, proj, conv_w, conv_b)


def _outproj_kernel(n_in, *refs):
    a_refs = refs[:n_in]
    w_refs = refs[n_in:2 * n_in]
    x_ref, gate_ref, pg_ref, o_ref = refs[2 * n_in:]
    y = jnp.dot(a_refs[0][...], w_refs[0][...], preferred_element_type=F32)
    for a_ref, w_ref in zip(a_refs[1:], w_refs[1:]):
        y = y + jnp.dot(a_ref[...], w_ref[...], preferred_element_type=F32)
    ms = jnp.mean(y * y, axis=-1, keepdims=True)
    r = y * lax.rsqrt(ms + EPS) * pg_ref[...]
    o_ref[...] = x_ref[...] + gate_ref[...] * r


def _outproj(acts, ws, x3, mod_row_of_batch, gate, pg, tm):
    nb, rows, _ = x3.shape
    n_in = len(acts)
    in_specs = [pl.BlockSpec((None, tm, a.shape[-1]), lambda b, i: (b, i, 0)) for a in acts]
    in_specs += [pl.BlockSpec(w.shape, lambda b, i: (0, 0)) for w in ws]
    in_specs += [
        pl.BlockSpec((None, tm, D_MODEL), lambda b, i: (b, i, 0)),
        pl.BlockSpec((None, 1, D_MODEL), lambda b, i: (mod_row_of_batch(b), 0, 0)),
        pl.BlockSpec((1, D_MODEL), lambda b, i: (0, 0)),
    ]
    return pl.pallas_call(
        functools.partial(_outproj_kernel, n_in),
        grid=(nb, rows // tm),
        in_specs=in_specs,
        out_specs=pl.BlockSpec((None, tm, D_MODEL), lambda b, i: (b, i, 0)),
        out_shape=jax.ShapeDtypeStruct((nb, rows, D_MODEL), F32),
        compiler_params=pltpu.CompilerParams(
            dimension_semantics=("arbitrary", "arbitrary"), vmem_limit_bytes=VMEM_LIMIT),
        name="outproj",
    )(*acts, *ws, x3, gate, pg)


_NT = (((1,), (1,)), ((), ()))
_TN = (((0,), (0,)), ((), ()))
SUBLANES = 8
SCAN_HEADS = 2
SCAN_UNROLL = 2


def _chunk_cumsum(x, reverse):
    c, w = x.shape
    sub = lax.broadcasted_iota(jnp.int32, (SUBLANES, w), 0)
    nblk = c // SUBLANES
    out = [None] * nblk
    carry = None
    for j in (range(nblk - 1, -1, -1) if reverse else range(nblk)):
        y = x[j * SUBLANES:(j + 1) * SUBLANES, :]
        for s in (1, 2, 4):
            if reverse:
                y = y + jnp.where(sub < SUBLANES - s, pltpu.roll(y, SUBLANES - s, axis=0), 0.0)
            else:
                y = y + jnp.where(sub >= s, pltpu.roll(y, s, axis=0), 0.0)
        if carry is not None:
            y = y + carry
        edge = y[0:1, :] if reverse else y[SUBLANES - 1:SUBLANES, :]
        carry = jnp.broadcast_to(edge, y.shape)
        out[j] = y
    return jnp.concatenate(out, axis=0), carry


def _scan_kernel(zf_ref, zb_ref, v_ref, q_ref, g_ref, czf_ref, czb_ref, cv_ref,
                 lbf_ref, lbb_ref, og_ref, o_ref,
                 qd_s, ki_s, kd_s, dec_s, sc_s, ds_s, st_s, of_s, ob_s):
    c = CHUNK
    n_lat = zf_ref.shape[0] // c
    n_ctx = czf_ref.shape[0] // c
    n_heads = zf_ref.shape[1] // HG_DK
    row = lax.broadcasted_iota(jnp.int32, (c, c), 0)
    col = lax.broadcasted_iota(jnp.int32, (c, c), 1)
    dirs = ((zf_ref, czf_ref, lbf_ref[...], False, col <= row, of_s),
            (zb_ref, czb_ref, lbb_ref[...], True, col >= row, ob_s))

    def lanes(h):
        return slice(h * HG_DK, (h + 1) * HG_DK)

    def rows(chunk):
        return pl.ds(pl.multiple_of(chunk * c, c), c)

    def dec_rows(chunk):
        return pl.ds(pl.multiple_of(chunk * SUBLANES, SUBLANES), SUBLANES)

    def gate_decay(z, lb, reverse):
        sig = jax.nn.sigmoid(z)
        k = (1.0 - lb) * (1.0 - sig)
        bc, b_last = _chunk_cumsum(jnp.log(lb + (1.0 - lb) * sig), reverse)
        k_inv = k * jnp.exp(-bc)
        decay = jnp.exp(b_last)
        k_dec = (k_inv * decay[0:1, :]).astype(BF16)
        return bc, k_inv, k_dec, decay

    def update_state(d, h, v, k_dec, decay):
        ds_t = lax.dot_general(v, k_dec, _TN, preferred_element_type=F32)
        st_s[d, h] = st_s[d, h] * decay + ds_t

    def prep(d, chunk):
        z_ref, _, lb, reverse, _, _ = dirs[d]
        sl = rows(chunk)
        bc, k_inv, k_dec, decay = gate_decay(z_ref[sl, :], lb, reverse)
        qd_s[d, sl, :] = (q_ref[sl, :].astype(F32) * jnp.exp(bc)).astype(BF16)
        ki_s[d, sl, :] = k_inv.astype(BF16)
        kd_s[d, sl, :] = k_dec
        dec_s[d, dec_rows(chunk), :] = decay

    def state_rows(chunk):
        return pl.ds(pl.multiple_of(chunk * HG_DV, HG_DV), HG_DV)

    def local(d, chunk):
        mask = dirs[d][4]
        sl = rows(chunk)
        for h in range(n_heads):
            sc = lax.dot_general(qd_s[d, sl, lanes(h)], ki_s[d, sl, lanes(h)], _NT,
                                 preferred_element_type=F32)
            sc_s[d, h, sl, :] = jnp.where(mask, sc, 0.0).astype(BF16)
            ds_s[d, h, state_rows(chunk), :] = lax.dot_general(
                v_ref[sl, lanes(h)], kd_s[d, sl, lanes(h)], _TN, preferred_element_type=F32)

    def step(d, chunk):
        out_s = dirs[d][5]
        sl = rows(chunk)
        decay = dec_s[d, dec_rows(chunk), :]
        for h in range(n_heads):
            inter = lax.dot_general(qd_s[d, sl, lanes(h)], st_s[d, h].astype(BF16), _NT,
                                    preferred_element_type=F32)
            out_s[sl, lanes(h)] = inter + jnp.dot(sc_s[d, h, sl, :], v_ref[sl, lanes(h)],
                                                  preferred_element_type=F32)
            st_s[d, h] = (st_s[d, h] * decay[0:1, lanes(h)]
                          + ds_s[d, h, state_rows(chunk), :])

    st_s[...] = jnp.zeros_like(st_s)
    for j in range(n_ctx):
        for d in range(2):
            _, cz_ref, lb, reverse, _, _ = dirs[d]
            sl = pl.ds((n_ctx - 1 - j if reverse else j) * c, c)
            _, _, k_dec, decay = gate_decay(cz_ref[sl, :], lb, reverse)
            for h in range(n_heads):
                update_state(d, h, cv_ref[sl, lanes(h)], k_dec[:, lanes(h)],
                             decay[0:1, lanes(h)])

    def chunk_of(d, idx):
        return n_lat - 1 - idx if dirs[d][3] else idx

    def for_chunks(stage, first):
        for u in range(SCAN_UNROLL):
            idx = first + u
            idx = min(idx, n_lat - 1) if isinstance(idx, int) else jnp.minimum(idx, n_lat - 1)
            for d in range(2):
                stage(d, chunk_of(d, idx))

    for_chunks(prep, 0)
    for_chunks(prep, SCAN_UNROLL)
    for_chunks(local, 0)

    def body(it, carry):
        for_chunks(step, it * SCAN_UNROLL)
        for_chunks(local, (it + 1) * SCAN_UNROLL)
        for_chunks(prep, (it + 2) * SCAN_UNROLL)
        return carry

    lax.fori_loop(0, n_lat // SCAN_UNROLL, body, 0)

    og = og_ref[...]
    for h in range(n_heads):
        o = of_s[:, lanes(h)] + ob_s[:, lanes(h)]
        o = o * lax.rsqrt(jnp.mean(o * o, axis=-1, keepdims=True) + EPS)
        o_ref[:, lanes(h)] = (o * og[:, lanes(h)]
                              * _silu(g_ref[:, lanes(h)].astype(F32))).astype(o_ref.dtype)


def _hgrn2_scan(zf, zb, v, q, g, czf, czb, cv, lb_f, lb_b, onorm_g):
    nb, n, _ = zf.shape
    nc = czf.shape[1]
    w = SCAN_HEADS * HG_DK
    lat = pl.BlockSpec((None, n, w), lambda b, h: (b, 0, h))
    ctx = pl.BlockSpec((None, nc, w), lambda b, h: (b, 0, h))
    vec = pl.BlockSpec((1, w), lambda b, h: (0, h))
    return pl.pallas_call(
        _scan_kernel,
        grid=(nb, HG_HEADS // SCAN_HEADS),
        in_specs=[lat] * 5 + [ctx] * 3 + [vec] * 3,
        out_specs=lat,
        out_shape=jax.ShapeDtypeStruct((nb, n, HG_V), BF16),
        scratch_shapes=[pltpu.VMEM((2, n, w), BF16),
                        pltpu.VMEM((2, n, w), BF16),
                        pltpu.VMEM((2, n, w), BF16),
                        pltpu.VMEM((2, n // CHUNK * SUBLANES, w), F32),
                        pltpu.VMEM((2, SCAN_HEADS, n, CHUNK), BF16),
                        pltpu.VMEM((2, SCAN_HEADS, n // CHUNK * HG_DV, HG_DK), F32),
                        pltpu.VMEM((2, SCAN_HEADS, HG_DV, HG_DK), F32),
                        pltpu.VMEM((n, w), F32),
                        pltpu.VMEM((n, w), F32)],
        compiler_params=pltpu.CompilerParams(
            dimension_semantics=("arbitrary", "arbitrary"), vmem_limit_bytes=VMEM_LIMIT),
        name="hgrn2_scan",
    )(zf, zb, v, q, g, czf, czb, cv, lb_f, lb_b, onorm_g)


def _window(n, w):
    t = np.arange(n)
    lo = np.maximum(t - w // 2, 0)
    hi = np.minimum(t + w // 2 - 1, n - 1)
    s = np.arange(n)
    inside = (s[None, :] >= lo[:, None]) & (s[None, :] <= hi[:, None])
    return inside, (hi - lo + 1)


def _pool_operators(n, on_grid):
    mats, icnts = [], []
    for w in POOL_WINDOWS:
        if on_grid:
            rows = n // GRID_W
            in_r, cnt_r = _window(rows, w)
            in_c, cnt_c = _window(GRID_W, w)
            m = (in_r[:, None, :, None] & in_c[None, :, None, :]).reshape(n, n)
            cnt = (cnt_r[:, None] * cnt_c[None, :]).reshape(n)
        else:
            m, cnt = _window(n, w)
        mats.append(m)
        icnts.append(1.0 / cnt.astype(np.float64))
    p01 = jnp.asarray(np.stack(mats).astype(np.float32), dtype=BF16)
    icnt = jnp.asarray(np.stack(icnts).astype(np.float32)[..., None])
    return p01, icnt


def kernel(x, c, ctx, c_ctx, ada_w, ada_b, pre_g, post_g, ev_w_in, ev_pool_w, ev_pool_scale,
           ev_conv_w, ev_conv_b, ev_w_out, od_w_in, od_onorm_g, od_w_out, lb_logits):
    nb, n, d = x.shape
    nc = ctx.shape[1]
    lat_row = lambda b: b
    ctx_row = lambda b: CTX_ROW

    lb_table = jnp.cumsum(jax.nn.softmax(lb_logits.astype(F32), axis=1), axis=1)

    cc = jnp.zeros((MOD_ROWS, d), F32).at[:nb].set(c).at[CTX_ROW].set(c_ctx)
    mod = _ada_table(cc, ada_w, ada_b)
    mod = mod.reshape(2, MOD_ROWS, 3, 1, d)
    shift = [mod[l, :, 0] for l in range(2)]
    scale = [mod[l, :, 1] for l in range(2)]
    gate = [mod[l, :, 2] for l in range(2)]

    ctx_flat = ctx.reshape(1, nb * nc, d)

    w_in0 = ev_w_in[0].astype(BF16)
    w_out0 = ev_w_out[0].astype(BF16)
    pool_w = ev_pool_w[0].astype(BF16)
    pool_scale = ev_pool_scale[0].reshape(1, POOL_W)
    conv_w = ev_conv_w[0]
    conv_b = ev_conv_b[0].reshape(1, CONV_W)
    pre0 = pre_g[0].reshape(1, d)
    post0 = post_g[0].reshape(1, d)

    def even_mixer(tokens3, mod_row, on_grid, tm_in, tm_out):
        b3, n3, _ = tokens3.shape
        proj, = _inproj(tokens3, mod_row, shift[0], scale[0], pre0, w_in0, EVEN_IN, [BF16],
                        tm_in, 1024)
        n_tok = n if on_grid else nc
        proj = proj.reshape(nb, n_tok, EVEN_IN)
        p01, icnt = _pool_operators(n_tok, on_grid)
        a_out = _pool_mixer(proj, p01, icnt, pool_w, pool_scale)
        b_out = _conv_mixer(proj, conv_w, conv_b)
        acts = [a_out.reshape(b3, n3, POOL_W), b_out.reshape(b3, n3, CONV_W)]
        return _outproj(acts, [w_out0[:POOL_W], w_out0[POOL_W:]], tokens3, mod_row,
                        gate[0], post0, tm_out)

    x1 = even_mixer(x, lat_row, True, 1024, 512)
    ctx1 = even_mixer(ctx_flat, ctx_row, False, 1024, 512)

    w_in1 = od_w_in[0].astype(BF16)
    w_out1 = od_w_out[0].astype(BF16)
    pre1 = pre_g[1].reshape(1, d)
    post1 = post_g[1].reshape(1, d)
    tn1 = SCAN_HEADS * HG_DK
    zf, zb, v, q, g = _inproj(x1, lat_row, shift[1], scale[1], pre1, w_in1, HG_K,
                              [F32, F32, BF16, BF16, BF16], 1024, tn1)
    czf, czb, cv = [t.reshape(nb, nc, HG_K) for t in
                    _inproj(ctx1, ctx_row, shift[1], scale[1], pre1, w_in1, HG_K,
                            [F32, F32, BF16], 1024, tn1)]
    o = _hgrn2_scan(zf, zb, v, q, g, czf, czb, cv, lb_table[0, 1].reshape(1, HG_K),
                    lb_table[1, 1].reshape(1, HG_K), od_onorm_g[0].reshape(1, HG_V))
    return _outproj([o], [w_out1], x1, lat_row, gate[1], post1, 512)
```

```python
import functools

import numpy as np
import jax
import jax.numpy as jnp
from jax import lax
from jax.experimental import pallas as pl
from jax.experimental.pallas import tpu as pltpu

F32 = jnp.float32
BF16 = jnp.bfloat16

D_MODEL = 2048
BATCH = 4
SEQ = 2048
CTX_LEN = 256
GRID_W = 64
EPS = 1e-6

POOL_WINDOWS = (2, 4, 8, 16)
POOL_W = D_MODEL // 2
POOL_GROUP = POOL_W // len(POOL_WINDOWS)
CONV_W = D_MODEL // 2
EVEN_IN = 2 * POOL_W + 4 * CONV_W

HG_DK = 128
HG_HEADS = D_MODEL // HG_DK
HG_DV = D_MODEL // HG_HEADS
HG_K = HG_HEADS * HG_DK
HG_V = HG_HEADS * HG_DV
ODD_IN = 3 * HG_K + 2 * HG_V
CHUNK = 64

MOD_ROWS = 8
CTX_ROW = BATCH
VMEM_LIMIT = 56 * 1024 * 1024


def _silu(v):
    return v * jax.nn.sigmoid(v)


def _ada_kernel(cc_ref, w_ref, b_ref, o_ref):
    s = _silu(cc_ref[...])
    o_ref[...] = jnp.dot(s, w_ref[...], preferred_element_type=F32) + b_ref[...]


def _ada_table(cc, ada_w, ada_b):
    depth = ada_w.shape[0]
    tn = 1024
    return pl.pallas_call(
        _ada_kernel,
        grid=(depth, 3 * D_MODEL // tn),
        in_specs=[
            pl.BlockSpec((MOD_ROWS, D_MODEL), lambda l, j: (0, 0)),
            pl.BlockSpec((None, D_MODEL, tn), lambda l, j: (l, 0, j)),
            pl.BlockSpec((None, 1, tn), lambda l, j: (l, 0, j)),
        ],
        out_specs=pl.BlockSpec((None, MOD_ROWS, tn), lambda l, j: (l, 0, j)),
        out_shape=jax.ShapeDtypeStruct((depth, MOD_ROWS, 3 * D_MODEL), F32),
        compiler_params=pltpu.CompilerParams(
            dimension_semantics=("arbitrary", "arbitrary"), vmem_limit_bytes=VMEM_LIMIT),
        name="ada_table",
    )(cc, ada_w, ada_b.reshape(depth, 1, 3 * D_MODEL))


def _inproj_kernel(n_seg, x_ref, shift_ref, scale_ref, g_ref, *refs):
    w_refs = refs[:n_seg]
    o_refs = refs[n_seg:2 * n_seg]
    h_ref = refs[2 * n_seg]

    @pl.when(pl.program_id(2) == 0)
    def _():
        x = x_ref[...]
        ms = jnp.mean(x * x, axis=-1, keepdims=True)
        y = x * lax.rsqrt(ms + EPS) * g_ref[...]
        h_ref[...] = (y * (1.0 + scale_ref[...]) + shift_ref[...]).astype(BF16)

    for w_ref, o_ref in zip(w_refs, o_refs):
        o_ref[...] = jnp.dot(h_ref[...], w_ref[...],
                             preferred_element_type=F32).astype(o_ref.dtype)


def _inproj(x3, mod_row_of_batch, shift, scale, g, w, seg_width, seg_dtypes, tm, tn):
    nb, rows, _ = x3.shape
    n_seg = len(seg_dtypes)
    n_tiles = seg_width // tn
    mod_map = lambda b, i, j: (mod_row_of_batch(b), 0, 0)
    w_specs = [pl.BlockSpec((D_MODEL, tn), functools.partial(
        lambda k, b, i, j: (0, k * n_tiles + j), k)) for k in range(n_seg)]
    return pl.pallas_call(
        functools.partial(_inproj_kernel, n_seg),
        grid=(nb, rows // tm, n_tiles),
        in_specs=[
            pl.BlockSpec((None, tm, D_MODEL), lambda b, i, j: (b, i, 0)),
            pl.BlockSpec((None, 1, D_MODEL), mod_map),
            pl.BlockSpec((None, 1, D_MODEL), mod_map),
            pl.BlockSpec((1, D_MODEL), lambda b, i, j: (0, 0)),
        ] + w_specs,
        out_specs=[pl.BlockSpec((None, tm, tn), lambda b, i, j: (b, i, j))] * n_seg,
        out_shape=[jax.ShapeDtypeStruct((nb, rows, seg_width), dt) for dt in seg_dtypes],
        scratch_shapes=[pltpu.VMEM((tm, D_MODEL), BF16)],
        compiler_params=pltpu.CompilerParams(
            dimension_semantics=("arbitrary", "arbitrary", "arbitrary"),
            vmem_limit_bytes=VMEM_LIMIT),
        name="inproj",
    )(x3, shift, scale, g, *([w] * n_seg))


def _pool_kernel(p_ref, icnt_ref, v_ref, gate_ref, w_ref, sc_ref, o_ref):
    v = v_ref[...]
    box = jnp.dot(p_ref[...], v, preferred_element_type=F32)
    pooled = box * icnt_ref[...] - v.astype(F32)
    mixed = jnp.dot(pooled.astype(BF16), w_ref[...], preferred_element_type=F32)
    o_ref[...] = (mixed * sc_ref[...] * _silu(gate_ref[...].astype(F32))).astype(o_ref.dtype)


def _pool_mixer(proj, p01, icnt, pool_w, pool_scale):
    nb, n, _ = proj.shape
    ng = len(POOL_WINDOWS)
    g = POOL_GROUP
    return pl.pallas_call(
        _pool_kernel,
        grid=(ng, nb),
        in_specs=[
            pl.BlockSpec((None, n, n), lambda gi, b: (gi, 0, 0)),
            pl.BlockSpec((None, n, 1), lambda gi, b: (gi, 0, 0)),
            pl.BlockSpec((None, n, g), lambda gi, b: (b, 0, gi)),
            pl.BlockSpec((None, n, g), lambda gi, b: (b, 0, ng + gi)),
            pl.BlockSpec((None, g, g), lambda gi, b: (gi, 0, 0)),
            pl.BlockSpec((1, g), lambda gi, b: (0, gi)),
        ],
        out_specs=pl.BlockSpec((None, n, g), lambda gi, b: (b, 0, gi)),
        out_shape=jax.ShapeDtypeStruct((nb, n, POOL_W), BF16),
        compiler_params=pltpu.CompilerParams(
            dimension_semantics=("arbitrary", "arbitrary"), vmem_limit_bytes=VMEM_LIMIT),
        name="pool_mixer",
    )(p01, icnt, proj, proj, pool_w, pool_scale)


def _conv_kernel(x_ref, b_ref, c_ref, gate_ref, cw_ref, cb_ref, o_ref):
    n = x_ref.shape[0]
    u = c_ref[...].astype(F32) * x_ref[...].astype(F32)
    row = lax.broadcasted_iota(jnp.int32, u.shape, 0)
    u_prev = jnp.where(row == 0, 0.0, pltpu.roll(u, 1, axis=0))
    u_next = jnp.where(row == n - 1, 0.0, pltpu.roll(u, n - 1, axis=0))
    cw = cw_ref[...]
    conv = u_prev * cw[0:1, :] + u * cw[1:2, :] + u_next * cw[2:3, :] + cb_ref[...]
    o_ref[...] = (b_ref[...].astype(F32) * conv
                  * _silu(gate_ref[...].astype(F32))).astype(o_ref.dtype)


def _conv_mixer(proj, conv_w, conv_b):
    nb, n, _ = proj.shape
    tc = 256
    nblk = CONV_W // tc
    base = 2 * POOL_W // tc

    def seg(k):
        return pl.BlockSpec((None, n, tc), lambda b, j: (b, 0, base + k * nblk + j))

    return pl.pallas_call(
        _conv_kernel,
        grid=(nb, nblk),
        in_specs=[
            seg(0), seg(1), seg(2), seg(3),
            pl.BlockSpec((3, tc), lambda b, j: (0, j)),
            pl.BlockSpec((1, tc), lambda b, j: (0, j)),
        ],
        out_specs=pl.BlockSpec((None, n, tc), lambda b, j: (b, 0, j)),
        out_shape=jax.ShapeDtypeStruct((nb, n, CONV_W), BF16),
        compiler_params=pltpu.CompilerParams(
            dimension_semantics=("arbitrary", "arbitrary"), vmem_limit_bytes=VMEM_LIMIT),
        name="conv_mixer",
    )(proj, proj, proj, proj, conv_w, conv_b)


def _outproj_kernel(n_in, *refs):
    a_refs = refs[:n_in]
    w_refs = refs[n_in:2 * n_in]
    x_ref, gate_ref, pg_ref, o_ref = refs[2 * n_in:]
    y = jnp.dot(a_refs[0][...], w_refs[0][...], preferred_element_type=F32)
    for a_ref, w_ref in zip(a_refs[1:], w_refs[1:]):
        y = y + jnp.dot(a_ref[...], w_ref[...], preferred_element_type=F32)
    ms = jnp.mean(y * y, axis=-1, keepdims=True)
    r = y * lax.rsqrt(ms + EPS) * pg_ref[...]
    o_ref[...] = x_ref[...] + gate_ref[...] * r


def _outproj(acts, ws, x3, mod_row_of_batch, gate, pg, tm):
    nb, rows, _ = x3.shape
    n_in = len(acts)
    in_specs = [pl.BlockSpec((None, tm, a.shape[-1]), lambda b, i: (b, i, 0)) for a in acts]
    in_specs += [pl.BlockSpec(w.shape, lambda b, i: (0, 0)) for w in ws]
    in_specs += [
        pl.BlockSpec((None, tm, D_MODEL), lambda b, i: (b, i, 0)),
        pl.BlockSpec((None, 1, D_MODEL), lambda b, i: (mod_row_of_batch(b), 0, 0)),
        pl.BlockSpec((1, D_MODEL), lambda b, i: (0, 0)),
    ]
    return pl.pallas_call(
        functools.partial(_outproj_kernel, n_in),
        grid=(nb, rows // tm),
        in_specs=in_specs,
        out_specs=pl.BlockSpec((None, tm, D_MODEL), lambda b, i: (b, i, 0)),
        out_shape=jax.ShapeDtypeStruct((nb, rows, D_MODEL), F32),
        compiler_params=pltpu.CompilerParams(
            dimension_semantics=("arbitrary", "arbitrary"), vmem_limit_bytes=VMEM_LIMIT),
        name="outproj",
    )(*acts, *ws, x3, gate, pg)


_NT = (((1,), (1,)), ((), ()))
_TN = (((0,), (0,)), ((), ()))
SUBLANES = 8
SCAN_HEADS = 2
SCAN_UNROLL = 1
SCAN_CHUNK = 2 * CHUNK


def _chunk_cumsum(x, reverse, in_ref, out_ref):
    c, w = x.shape
    nblk = c // SUBLANES
    order = range(nblk - 1, -1, -1) if reverse else range(nblk)
    in_ref[...] = x
    pref = [None] * nblk
    acc = None
    for j in order:
        blk = in_ref[pl.ds(j, SUBLANES, stride=nblk), :]
        acc = blk if acc is None else acc + blk
        pref[j] = acc
    total = acc
    sub = lax.broadcasted_iota(jnp.int32, (SUBLANES, w), 0)
    incl = total
    for s in (1, 2, 4):
        if reverse:
            incl = incl + jnp.where(sub < SUBLANES - s,
                                    pltpu.roll(incl, SUBLANES - s, axis=0), 0.0)
        else:
            incl = incl + jnp.where(sub >= s, pltpu.roll(incl, s, axis=0), 0.0)
    before = incl - total
    for j in range(nblk):
        out_ref[pl.ds(j, SUBLANES, stride=nblk), :] = pref[j] + before
    edge = incl[0:1, :] if reverse else incl[SUBLANES - 1:SUBLANES, :]
    return out_ref[...], jnp.broadcast_to(edge, (SUBLANES, w))


def _scan_kernel(zf_ref, zb_ref, v_ref, q_ref, g_ref, czf_ref, czb_ref, cv_ref,
                 lbf_ref, lbb_ref, og_ref, o_ref,
                 qd_s, ki_s, kd_s, dec_s, mid_s, cs_s, sc_s, ds_s, st_s, of_s, ob_s):
    c = SCAN_CHUNK
    half = CHUNK
    n_lat = zf_ref.shape[0] // c
    n_ctx = czf_ref.shape[0] // c
    n_heads = zf_ref.shape[1] // HG_DK
    row = lax.broadcasted_iota(jnp.int32, (c, c), 0)
    col = lax.broadcasted_iota(jnp.int32, (c, c), 1)
    dirs = ((zf_ref, czf_ref, lbf_ref[...], False, col <= row, of_s),
            (zb_ref, czb_ref, lbb_ref[...], True, col >= row, ob_s))

    def lanes(h):
        return slice(h * HG_DK, (h + 1) * HG_DK)

    def rows(chunk):
        return pl.ds(pl.multiple_of(chunk * c, c), c)

    def dec_rows(chunk):
        return pl.ds(pl.multiple_of(chunk * SUBLANES, SUBLANES), SUBLANES)

    def gate_decay(z, lb, reverse, slot):
        bt = (0.5 * (1.0 - lb)) * jnp.tanh(0.5 * z)
        k = 0.5 * (1.0 - lb) - bt
        lf = jnp.log(0.5 * (1.0 + lb) + bt)
        lo, hi = lf[:half], lf[half:]
        first, second = (hi, lo) if reverse else (lo, hi)
        away, t_first = _chunk_cumsum(first, not reverse, cs_s.at[slot, 0], cs_s.at[slot, 1])
        toward, t_second = _chunk_cumsum(second, reverse, cs_s.at[slot, 2], cs_s.at[slot, 3])
        a_first = first - away
        a = jnp.concatenate([toward, a_first] if reverse else [a_first, toward], axis=0)
        k_inv = k * jnp.exp(-a)
        d_first = jnp.exp(t_first)
        d_second = jnp.exp(t_second)
        k_dec = (k_inv * d_second[0:1, :]).astype(BF16)
        return a, k_inv, k_dec, d_first, d_first * d_second

    def update_state(d, h, v, k_dec, decay):
        ds_t = lax.dot_general(v, k_dec, _TN, preferred_element_type=F32)
        st_s[d, h] = st_s[d, h] * decay + ds_t

    def cumsum_slot(u, d, h):
        return (u * 2 + d) * n_heads + h

    def prep(d, chunk, u):
        z_ref, _, lb, reverse, _, _ = dirs[d]
        sl = rows(chunk)
        for h in range(n_heads):
            a, k_inv, k_dec, d_first, decay = gate_decay(
                z_ref[sl, lanes(h)], lb[:, lanes(h)], reverse, cumsum_slot(u, d, h))
            qd_s[d, sl, lanes(h)] = (q_ref[sl, lanes(h)].astype(F32) * jnp.exp(a)).astype(BF16)
            ki_s[d, sl, lanes(h)] = k_inv.astype(BF16)
            kd_s[d, sl, lanes(h)] = k_dec
            dec_s[d, dec_rows(chunk), lanes(h)] = decay
            mid_s[d, dec_rows(chunk), lanes(h)] = d_first

    def state_rows(chunk):
        return pl.ds(pl.multiple_of(chunk * HG_DV, HG_DV), HG_DV)

    def local(d, chunk, _):
        mask = dirs[d][4]
        sl = rows(chunk)
        for h in range(n_heads):
            sc = lax.dot_general(qd_s[d, sl, lanes(h)], ki_s[d, sl, lanes(h)], _NT,
                                 preferred_element_type=F32)
            sc_s[d, h, sl, :] = jnp.where(mask, sc, 0.0).astype(BF16)
            ds_s[d, h, state_rows(chunk), :] = lax.dot_general(
                v_ref[sl, lanes(h)], kd_s[d, sl, lanes(h)], _TN, preferred_element_type=F32)

    def step(d, chunk, _):
        out_s = dirs[d][5]
        sl = rows(chunk)
        decay = dec_s[d, dec_rows(chunk), :]
        d_first = mid_s[d, dec_rows(chunk), :]
        for h in range(n_heads):
            st_mid = (st_s[d, h] * d_first[0:1, lanes(h)]).astype(BF16)
            inter = lax.dot_general(qd_s[d, sl, lanes(h)], st_mid, _NT,
                                    preferred_element_type=F32)
            out_s[sl, lanes(h)] = inter + jnp.dot(sc_s[d, h, sl, :], v_ref[sl, lanes(h)],
                                                  preferred_element_type=F32)
            st_s[d, h] = (st_s[d, h] * decay[0:1, lanes(h)]
                          + ds_s[d, h, state_rows(chunk), :])

    st_s[...] = jnp.zeros_like(st_s)
    for j in range(n_ctx):
        for d in range(2):
            _, cz_ref, lb, reverse, _, _ = dirs[d]
            sl = pl.ds((n_ctx - 1 - j if reverse else j) * c, c)
            for h in range(n_heads):
                _, _, k_dec, _, decay = gate_decay(cz_ref[sl, lanes(h)], lb[:, lanes(h)],
                                                   reverse, cumsum_slot(j % SCAN_UNROLL, d, h))
                update_state(d, h, cv_ref[sl, lanes(h)], k_dec, decay[0:1, :])

    def chunk_of(d, idx):
        return n_lat - 1 - idx if dirs[d][3] else idx

    def for_chunks(stage, first):
        for u in range(SCAN_UNROLL):
            idx = first + u
            idx = min(idx, n_lat - 1) if isinstance(idx, int) else jnp.minimum(idx, n_lat - 1)
            for d in range(2):
                stage(d, chunk_of(d, idx), u)

    for_chunks(prep, 0)
    for_chunks(prep, SCAN_UNROLL)
    for_chunks(local, 0)

    def body(it, carry):
        for_chunks(step, it * SCAN_UNROLL)
        for_chunks(local, (it + 1) * SCAN_UNROLL)
        for_chunks(prep, (it + 2) * SCAN_UNROLL)
        return carry

    lax.fori_loop(0, n_lat // SCAN_UNROLL, body, 0)

    og = og_ref[...]
    for h in range(n_heads):
        o = of_s[:, lanes(h)] + ob_s[:, lanes(h)]
        o = o * lax.rsqrt(jnp.mean(o * o, axis=-1, keepdims=True) + EPS)
        o_ref[:, lanes(h)] = (o * og[:, lanes(h)]
                              * _silu(g_ref[:, lanes(h)].astype(F32))).astype(o_ref.dtype)


def _hgrn2_scan(zf, zb, v, q, g, czf, czb, cv, lb_f, lb_b, onorm_g):
    nb, n, _ = zf.shape
    nc = czf.shape[1]
    w = SCAN_HEADS * HG_DK
    lat = pl.BlockSpec((None, n, w), lambda b, h: (b, 0, h))
    ctx = pl.BlockSpec((None, nc, w), lambda b, h: (b, 0, h))
    vec = pl.BlockSpec((1, w), lambda b, h: (0, h))
    return pl.pallas_call(
        _scan_kernel,
        grid=(nb, HG_HEADS // SCAN_HEADS),
        in_specs=[lat] * 5 + [ctx] * 3 + [vec] * 3,
        out_specs=lat,
        out_shape=jax.ShapeDtypeStruct((nb, n, HG_V), BF16),
        scratch_shapes=[pltpu.VMEM((2, n, w), BF16),
                        pltpu.VMEM((2, n, w), BF16),
                        pltpu.VMEM((2, n, w), BF16),
                        pltpu.VMEM((2, n // SCAN_CHUNK * SUBLANES, w), F32),
                        pltpu.VMEM((2, n // SCAN_CHUNK * SUBLANES, w), F32),
                        pltpu.VMEM((SCAN_UNROLL * 2 * SCAN_HEADS, 4, CHUNK, HG_DK), F32),
                        pltpu.VMEM((2, SCAN_HEADS, n, SCAN_CHUNK), BF16),
                        pltpu.VMEM((2, SCAN_HEADS, n // SCAN_CHUNK * HG_DV, HG_DK), F32),
                        pltpu.VMEM((2, SCAN_HEADS, HG_DV, HG_DK), F32),
                        pltpu.VMEM((n, w), F32),
                        pltpu.VMEM((n, w), F32)],
        compiler_params=pltpu.CompilerParams(
            dimension_semantics=("arbitrary", "arbitrary"), vmem_limit_bytes=VMEM_LIMIT),
        name="hgrn2_scan",
    )(zf, zb, v, q, g, czf, czb, cv, lb_f, lb_b, onorm_g)


def _window(n, w):
    t = np.arange(n)
    lo = np.maximum(t - w // 2, 0)
    hi = np.minimum(t + w // 2 - 1, n - 1)
    s = np.arange(n)
    inside = (s[None, :] >= lo[:, None]) & (s[None, :] <= hi[:, None])
    return inside, (hi - lo + 1)


def _pool_operators(n, on_grid):
    mats, icnts = [], []
    for w in POOL_WINDOWS:
        if on_grid:
            rows = n // GRID_W
            in_r, cnt_r = _window(rows, w)
            in_c, cnt_c = _window(GRID_W, w)
            m = (in_r[:, None, :, None] & in_c[None, :, None, :]).reshape(n, n)
            cnt = (cnt_r[:, None] * cnt_c[None, :]).reshape(n)
        else:
            m, cnt = _window(n, w)
        mats.append(m)
        icnts.append(1.0 / cnt.astype(np.float64))
    p01 = jnp.asarray(np.stack(mats).astype(np.float32), dtype=BF16)
    icnt = jnp.asarray(np.stack(icnts).astype(np.float32)[..., None])
    return p01, icnt


def kernel(x, c, ctx, c_ctx, ada_w, ada_b, pre_g, post_g, ev_w_in, ev_pool_w, ev_pool_scale,
           ev_conv_w, ev_conv_b, ev_w_out, od_w_in, od_onorm_g, od_w_out, lb_logits):
    nb, n, d = x.shape
    nc = ctx.shape[1]
    lat_row = lambda b: b
    ctx_row = lambda b: CTX_ROW

    lb_table = jnp.cumsum(jax.nn.softmax(lb_logits.astype(F32), axis=1), axis=1)

    cc = jnp.zeros((MOD_ROWS, d), F32).at[:nb].set(c).at[CTX_ROW].set(c_ctx)
    mod = _ada_table(cc, ada_w, ada_b)
    mod = mod.reshape(2, MOD_ROWS, 3, 1, d)
    shift = [mod[l, :, 0] for l in range(2)]
    scale = [mod[l, :, 1] for l in range(2)]
    gate = [mod[l, :, 2] for l in range(2)]

    ctx_flat = ctx.reshape(1, nb * nc, d)

    w_in0 = ev_w_in[0].astype(BF16)
    w_out0 = ev_w_out[0].astype(BF16)
    pool_w = ev_pool_w[0].astype(BF16)
    pool_scale = ev_pool_scale[0].reshape(1, POOL_W)
    conv_w = ev_conv_w[0]
    conv_b = ev_conv_b[0].reshape(1, CONV_W)
    pre0 = pre_g[0].reshape(1, d)
    post0 = post_g[0].reshape(1, d)

    def even_mixer(tokens3, mod_row, on_grid, tm_in, tm_out):
        b3, n3, _ = tokens3.shape
        proj, = _inproj(tokens3, mod_row, shift[0], scale[0], pre0, w_in0, EVEN_IN, [BF16],
                        tm_in, 1024)
        n_tok = n if on_grid else nc
        proj = proj.reshape(nb, n_tok, EVEN_IN)
        p01, icnt = _pool_operators(n_tok, on_grid)
        a_out = _pool_mixer(proj, p01, icnt, pool_w, pool_scale)
        b_out = _conv_mixer(proj, conv_w, conv_b)
        acts = [a_out.reshape(b3, n3, POOL_W), b_out.reshape(b3, n3, CONV_W)]
        return _outproj(acts, [w_out0[:POOL_W], w_out0[POOL_W:]], tokens3, mod_row,
                        gate[0], post0, tm_out)

    x1 = even_mixer(x, lat_row, True, 1024, 512)
    ctx1 = even_mixer(ctx_flat, ctx_row, False, 1024, 512)

    w_in1 = od_w_in[0].astype(BF16)
    w_out1 = od_w_out[0].astype(BF16)
    pre1 = pre_g[1].reshape(1, d)
    post1 = post_g[1].reshape(1, d)
    tn1 = SCAN_HEADS * HG_DK
    zf, zb, v, q, g = _inproj(x1, lat_row, shift[1], scale[1], pre1, w_in1, HG_K,
                              [F32, F32, BF16, BF16, BF16], 1024, tn1)
    czf, czb, cv = [t.reshape(nb, nc, HG_K) for t in
                    _inproj(ctx1, ctx_row, shift[1], scale[1], pre1, w_in1, HG_K,
                            [F32, F32, BF16], 1024, tn1)]
    o = _hgrn2_scan(zf, zb, v, q, g, czf, czb, cv, lb_table[0, 1].reshape(1, HG_K),
                    lb_table[1, 1].reshape(1, HG_K), od_onorm_g[0].reshape(1, HG_V))
    return _outproj([o], [w_out1], x1, lat_row, gate[1], post1, 512)
```

```python
import functools

import numpy as np
import jax
import jax.numpy as jnp
from jax import lax
from jax.experimental import pallas as pl
from jax.experimental.pallas import tpu as pltpu

F32 = jnp.float32
BF16 = jnp.bfloat16

D_MODEL = 2048
BATCH = 4
SEQ = 2048
CTX_LEN = 256
GRID_W = 64
EPS = 1e-6

POOL_WINDOWS = (2, 4, 8, 16)
POOL_W = D_MODEL // 2
POOL_GROUP = POOL_W // len(POOL_WINDOWS)
CONV_W = D_MODEL // 2
EVEN_IN = 2 * POOL_W + 4 * CONV_W

HG_DK = 128
HG_HEADS = D_MODEL // HG_DK
HG_DV = D_MODEL // HG_HEADS
HG_K = HG_HEADS * HG_DK
HG_V = HG_HEADS * HG_DV
ODD_IN = 3 * HG_K + 2 * HG_V
CHUNK = 64

MOD_ROWS = 8
CTX_ROW = BATCH
VMEM_LIMIT = 56 * 1024 * 1024
TM_NORM = 512
TM_IN = 1024
TM_OUT = 512


def _silu(v):
    return v * jax.nn.sigmoid(v)


def _ada_kernel(cc_ref, w_ref, b_ref, o_ref):
    s = _silu(cc_ref[...])
    o_ref[...] = jnp.dot(s, w_ref[...], preferred_element_type=F32) + b_ref[...]


def _ada_table(cc, ada_w, ada_b):
    depth = ada_w.shape[0]
    tn = 1024
    return pl.pallas_call(
        _ada_kernel,
        grid=(depth, 3 * D_MODEL // tn),
        in_specs=[
            pl.BlockSpec((MOD_ROWS, D_MODEL), lambda l, j: (0, 0)),
            pl.BlockSpec((None, D_MODEL, tn), lambda l, j: (l, 0, j)),
            pl.BlockSpec((None, 1, tn), lambda l, j: (l, 0, j)),
        ],
        out_specs=pl.BlockSpec((None, MOD_ROWS, tn), lambda l, j: (l, 0, j)),
        out_shape=jax.ShapeDtypeStruct((depth, MOD_ROWS, 3 * D_MODEL), F32),
        compiler_params=pltpu.CompilerParams(
            dimension_semantics=("arbitrary", "arbitrary"), vmem_limit_bytes=VMEM_LIMIT),
        name="ada_table",
    )(cc, ada_w, ada_b.reshape(depth, 1, 3 * D_MODEL))


def _norm_mod(x, g, scale, shift):
    ms = jnp.mean(x * x, axis=-1, keepdims=True)
    y = x * lax.rsqrt(ms + EPS) * g
    return y * (1.0 + scale) + shift


def _norm_kernel(x_ref, shift_ref, scale_ref, g_ref, o_ref):
    o_ref[...] = _norm_mod(x_ref[...], g_ref[...], scale_ref[...],
                           shift_ref[...]).astype(o_ref.dtype)


def _norm_tokens(x3, mod_row_of_batch, shift, scale, g, tm):
    nb, rows, _ = x3.shape
    mod_map = lambda b, i: (mod_row_of_batch(b), 0, 0)
    return pl.pallas_call(
        _norm_kernel,
        grid=(nb, rows // tm),
        in_specs=[
            pl.BlockSpec((None, tm, D_MODEL), lambda b, i: (b, i, 0)),
            pl.BlockSpec((None, 1, D_MODEL), mod_map),
            pl.BlockSpec((None, 1, D_MODEL), mod_map),
            pl.BlockSpec((1, D_MODEL), lambda b, i: (0, 0)),
        ],
        out_specs=pl.BlockSpec((None, tm, D_MODEL), lambda b, i: (b, i, 0)),
        out_shape=jax.ShapeDtypeStruct((nb, rows, D_MODEL), BF16),
        compiler_params=pltpu.CompilerParams(
            dimension_semantics=("arbitrary", "arbitrary"), vmem_limit_bytes=VMEM_LIMIT),
        name="norm_tokens",
    )(x3, shift, scale, g)


def _inproj_kernel(n_seg, h_ref, *refs):
    w_refs = refs[:n_seg]
    o_refs = refs[n_seg:2 * n_seg]
    wb_refs = refs[2 * n_seg:]

    @pl.when(pl.program_id(1) == 0)
    def _():
        for w_ref, wb_ref in zip(w_refs, wb_refs):
            wb_ref[...] = w_ref[...].astype(BF16)

    for wb_ref, o_ref in zip(wb_refs, o_refs):
        o_ref[...] = jnp.dot(h_ref[...], wb_ref[...],
                             preferred_element_type=F32).astype(o_ref.dtype)


def _inproj(h, w, seg_width, seg_dtypes, tm, tn):
    rows = h.shape[0]
    n_seg = len(seg_dtypes)
    n_tiles = seg_width // tn
    w_specs = [pl.BlockSpec((D_MODEL, tn), functools.partial(
        lambda k, j, i: (0, k * n_tiles + j), k)) for k in range(n_seg)]
    return pl.pallas_call(
        functools.partial(_inproj_kernel, n_seg),
        grid=(n_tiles, rows // tm),
        in_specs=[pl.BlockSpec((tm, D_MODEL), lambda j, i: (i, 0))] + w_specs,
        out_specs=[pl.BlockSpec((tm, tn), lambda j, i: (i, j))] * n_seg,
        out_shape=[jax.ShapeDtypeStruct((rows, seg_width), dt) for dt in seg_dtypes],
        scratch_shapes=[pltpu.VMEM((D_MODEL, tn), BF16)] * n_seg,
        compiler_params=pltpu.CompilerParams(
            dimension_semantics=("arbitrary", "arbitrary"), vmem_limit_bytes=VMEM_LIMIT),
        name="inproj",
    )(h, *([w] * n_seg))


def _pool_kernel(p_ref, icnt_ref, v_ref, gate_ref, w_ref, sc_ref, o_ref):
    v = v_ref[...]
    box = jnp.dot(p_ref[...], v, preferred_element_type=F32)
    pooled = box * icnt_ref[...] - v.astype(F32)
    mixed = jnp.dot(pooled.astype(BF16), w_ref[...], preferred_element_type=F32)
    o_ref[...] = (mixed * sc_ref[...] * _silu(gate_ref[...].astype(F32))).astype(o_ref.dtype)


def _pool_mixer(proj, p01, icnt, pool_w, pool_scale):
    nb, n, _ = proj.shape
    ng = len(POOL_WINDOWS)
    g = POOL_GROUP
    return pl.pallas_call(
        _pool_kernel,
        grid=(ng, nb),
        in_specs=[
            pl.BlockSpec((None, n, n), lambda gi, b: (gi, 0, 0)),
            pl.BlockSpec((None, n, 1), lambda gi, b: (gi, 0, 0)),
            pl.BlockSpec((None, n, g), lambda gi, b: (b, 0, gi)),
            pl.BlockSpec((None, n, g), lambda gi, b: (b, 0, ng + gi)),
            pl.BlockSpec((None, g, g), lambda gi, b: (gi, 0, 0)),
            pl.BlockSpec((1, g), lambda gi, b: (0, gi)),
        ],
        out_specs=pl.BlockSpec((None, n, g), lambda gi, b: (b, 0, gi)),
        out_shape=jax.ShapeDtypeStruct((nb, n, POOL_W), BF16),
        compiler_params=pltpu.CompilerParams(
            dimension_semantics=("arbitrary", "arbitrary"), vmem_limit_bytes=VMEM_LIMIT),
        name="pool_mixer",
    )(p01, icnt, proj, proj, pool_w, pool_scale)


def _conv_kernel(x_ref, b_ref, c_ref, gate_ref, cw_ref, cb_ref, o_ref):
    n = x_ref.shape[0]
    u = c_ref[...].astype(F32) * x_ref[...].astype(F32)
    row = lax.broadcasted_iota(jnp.int32, u.shape, 0)
    u_prev = jnp.where(row == 0, 0.0, pltpu.roll(u, 1, axis=0))
    u_next = jnp.where(row == n - 1, 0.0, pltpu.roll(u, n - 1, axis=0))
    cw = cw_ref[...]
    conv = u_prev * cw[0:1, :] + u * cw[1:2, :] + u_next * cw[2:3, :] + cb_ref[...]
    o_ref[...] = (b_ref[...].astype(F32) * conv
                  * _silu(gate_ref[...].astype(F32))).astype(o_ref.dtype)


def _conv_mixer(proj, conv_w, conv_b):
    nb, n, _ = proj.shape
    tc = 256
    nblk = CONV_W // tc
    base = 2 * POOL_W // tc

    def seg(k):
        return pl.BlockSpec((None, n, tc), lambda b, j: (b, 0, base + k * nblk + j))

    return pl.pallas_call(
        _conv_kernel,
        grid=(nb, nblk),
        in_specs=[
            seg(0), seg(1), seg(2), seg(3),
            pl.BlockSpec((3, tc), lambda b, j: (0, j)),
            pl.BlockSpec((1, tc), lambda b, j: (0, j)),
        ],
        out_specs=pl.BlockSpec((None, n, tc), lambda b, j: (b, 0, j)),
        out_shape=jax.ShapeDtypeStruct((nb, n, CONV_W), BF16),
        compiler_params=pltpu.CompilerParams(
            dimension_semantics=("arbitrary", "arbitrary"), vmem_limit_bytes=VMEM_LIMIT),
        name="conv_mixer",
    )(proj, proj, proj, proj, conv_w, conv_b)


def _outproj_kernel(n_in, emit_next, *refs):
    a_refs = refs[:n_in]
    w_refs = refs[n_in:2 * n_in]
    x_ref, gate_ref, pg_ref = refs[2 * n_in:2 * n_in + 3]
    rest = refs[2 * n_in + 3:]
    if emit_next:
        shift_ref, scale_ref, g_ref, o_ref, h_ref = rest[:5]
        wb_refs = rest[5:]
    else:
        o_ref = rest[0]
        wb_refs = rest[1:]

    @pl.when((pl.program_id(0) == 0) & (pl.program_id(1) == 0))
    def _():
        for w_ref, wb_ref in zip(w_refs, wb_refs):
            wb_ref[...] = w_ref[...].astype(BF16)

    y = jnp.dot(a_refs[0][...], wb_refs[0][...], preferred_element_type=F32)
    for a_ref, wb_ref in zip(a_refs[1:], wb_refs[1:]):
        y = y + jnp.dot(a_ref[...], wb_ref[...], preferred_element_type=F32)
    ms = jnp.mean(y * y, axis=-1, keepdims=True)
    r = y * lax.rsqrt(ms + EPS) * pg_ref[...]
    x_new = x_ref[...] + gate_ref[...] * r
    o_ref[...] = x_new
    if emit_next:
        h_ref[...] = _norm_mod(x_new, g_ref[...], scale_ref[...],
                               shift_ref[...]).astype(h_ref.dtype)


def _outproj(acts, w, x3, mod_row_of_batch, gate, pg, tm, next_norm=None):
    nb, rows, _ = x3.shape
    n_in = len(acts)
    k_in = acts[0].shape[-1]
    assert all(a.shape[-1] == k_in for a in acts) and n_in * k_in == w.shape[0]
    emit_next = next_norm is not None
    mod_spec = pl.BlockSpec((None, 1, D_MODEL), lambda b, i: (mod_row_of_batch(b), 0, 0))
    vec_spec = pl.BlockSpec((1, D_MODEL), lambda b, i: (0, 0))
    tok_spec = pl.BlockSpec((None, tm, D_MODEL), lambda b, i: (b, i, 0))
    in_specs = [pl.BlockSpec((None, tm, k_in), lambda b, i: (b, i, 0))] * n_in
    in_specs += [pl.BlockSpec((k_in, D_MODEL), functools.partial(lambda k, b, i: (k, 0), k),
                              pipeline_mode=pl.Buffered(1)) for k in range(n_in)]
    in_specs += [tok_spec, mod_spec, vec_spec]
    operands = [*acts, *([w] * n_in), x3, gate, pg]
    out_specs, out_shape = tok_spec, jax.ShapeDtypeStruct((nb, rows, D_MODEL), F32)
    if emit_next:
        in_specs += [mod_spec, mod_spec, vec_spec]
        operands += list(next_norm)
        out_specs = [tok_spec, tok_spec]
        out_shape = [out_shape, jax.ShapeDtypeStruct((nb, rows, D_MODEL), BF16)]
    return pl.pallas_call(
        functools.partial(_outproj_kernel, n_in, emit_next),
        grid=(nb, rows // tm),
        in_specs=in_specs,
        out_specs=out_specs,
        out_shape=out_shape,
        scratch_shapes=[pltpu.VMEM((k_in, D_MODEL), BF16)] * n_in,
        compiler_params=pltpu.CompilerParams(
            dimension_semantics=("arbitrary", "arbitrary"), vmem_limit_bytes=VMEM_LIMIT),
        name="outproj",
    )(*operands)


_NT = (((1,), (1,)), ((), ()))
_TN = (((0,), (0,)), ((), ()))
SUBLANES = 8
SCAN_HEADS = 2
SCAN_UNROLL = 1
SCAN_CHUNK = 2 * CHUNK


def _chunk_cumsum(x, reverse, in_ref, out_ref):
    c, w = x.shape
    nblk = c // SUBLANES
    order = range(nblk - 1, -1, -1) if reverse else range(nblk)
    in_ref[...] = x
    pref = [None] * nblk
    acc = None
    for j in order:
        blk = in_ref[pl.ds(j, SUBLANES, stride=nblk), :]
        acc = blk if acc is None else acc + blk
        pref[j] = acc
    total = acc
    sub = lax.broadcasted_iota(jnp.int32, (SUBLANES, w), 0)
    incl = total
    for s in (1, 2, 4):
        if reverse:
            incl = incl + jnp.where(sub < SUBLANES - s,
                                    pltpu.roll(incl, SUBLANES - s, axis=0), 0.0)
        else:
            incl = incl + jnp.where(sub >= s, pltpu.roll(incl, s, axis=0), 0.0)
    before = incl - total
    for j in range(nblk):
        out_ref[pl.ds(j, SUBLANES, stride=nblk), :] = pref[j] + before
    edge = incl[0:1, :] if reverse else incl[SUBLANES - 1:SUBLANES, :]
    return out_ref[...], jnp.broadcast_to(edge, (SUBLANES, w))


def _scan_kernel(zf_ref, zb_ref, v_ref, q_ref, g_ref, czf_ref, czb_ref, cv_ref,
                 lbf_ref, lbb_ref, og_ref, o_ref,
                 qd_s, ki_s, kd_s, dec_s, mid_s, cs_s, sc_s, ds_s, st_s, of_s, ob_s):
    c = SCAN_CHUNK
    half = CHUNK
    n_lat = zf_ref.shape[0] // c
    n_ctx = czf_ref.shape[0] // c
    n_heads = zf_ref.shape[1] // HG_DK
    row = lax.broadcasted_iota(jnp.int32, (c, c), 0)
    col = lax.broadcasted_iota(jnp.int32, (c, c), 1)
    dirs = ((zf_ref, czf_ref, lbf_ref[...], False, col <= row, of_s),
            (zb_ref, czb_ref, lbb_ref[...], True, col >= row, ob_s))

    def lanes(h):
        return slice(h * HG_DK, (h + 1) * HG_DK)

    def rows(chunk):
        return pl.ds(pl.multiple_of(chunk * c, c), c)

    def dec_rows(chunk):
        return pl.ds(pl.multiple_of(chunk * SUBLANES, SUBLANES), SUBLANES)

    def gate_decay(z, lb, reverse, slot):
        bt = (0.5 * (1.0 - lb)) * jnp.tanh(0.5 * z)
        k = 0.5 * (1.0 - lb) - bt
        lf = jnp.log(0.5 * (1.0 + lb) + bt)
        lo, hi = lf[:half], lf[half:]
        first, second = (hi, lo) if reverse else (lo, hi)
        away, t_first = _chunk_cumsum(first, not reverse, cs_s.at[slot, 0], cs_s.at[slot, 1])
        toward, t_second = _chunk_cumsum(second, reverse, cs_s.at[slot, 2], cs_s.at[slot, 3])
        a_first = first - away
        a = jnp.concatenate([toward, a_first] if reverse else [a_first, toward], axis=0)
        k_inv = k * jnp.exp(-a)
        d_first = jnp.exp(t_first)
        d_second = jnp.exp(t_second)
        k_dec = (k_inv * d_second[0:1, :]).astype(BF16)
        return a, k_inv, k_dec, d_first, d_first * d_second

    def update_state(d, h, v, k_dec, decay):
        ds_t = lax.dot_general(v, k_dec, _TN, preferred_element_type=F32)
        st_s[d, h] = st_s[d, h] * decay + ds_t

    def cumsum_slot(u, d, h):
        return (u * 2 + d) * n_heads + h

    def prep(d, chunk, u):
        z_ref, _, lb, reverse, _, _ = dirs[d]
        sl = rows(chunk)
        for h in range(n_heads):
            a, k_inv, k_dec, d_first, decay = gate_decay(
                z_ref[sl, lanes(h)], lb[:, lanes(h)], reverse, cumsum_slot(u, d, h))
            qd_s[d, sl, lanes(h)] = (q_ref[sl, lanes(h)].astype(F32) * jnp.exp(a)).astype(BF16)
            ki_s[d, sl, lanes(h)] = k_inv.astype(BF16)
            kd_s[d, sl, lanes(h)] = k_dec
            dec_s[d, dec_rows(chunk), lanes(h)] = decay
            mid_s[d, dec_rows(chunk), lanes(h)] = d_first

    def state_rows(chunk):
        return pl.ds(pl.multiple_of(chunk * HG_DV, HG_DV), HG_DV)

    def local(d, chunk, _):
        mask = dirs[d][4]
        sl = rows(chunk)
        for h in range(n_heads):
            sc = lax.dot_general(qd_s[d, sl, lanes(h)], ki_s[d, sl, lanes(h)], _NT,
                                 preferred_element_type=F32)
            sc_s[d, h, sl, :] = jnp.where(mask, sc, 0.0).astype(BF16)
            ds_s[d, h, state_rows(chunk), :] = lax.dot_general(
                v_ref[sl, lanes(h)], kd_s[d, sl, lanes(h)], _TN, preferred_element_type=F32)

    def step(d, chunk, _):
        out_s = dirs[d][5]
        sl = rows(chunk)
        decay = dec_s[d, dec_rows(chunk), :]
        d_first = mid_s[d, dec_rows(chunk), :]
        for h in range(n_heads):
            st_mid = (st_s[d, h] * d_first[0:1, lanes(h)]).astype(BF16)
            inter = lax.dot_general(qd_s[d, sl, lanes(h)], st_mid, _NT,
                                    preferred_element_type=F32)
            out_s[sl, lanes(h)] = inter + jnp.dot(sc_s[d, h, sl, :], v_ref[sl, lanes(h)],
                                                  preferred_element_type=F32)
            st_s[d, h] = (st_s[d, h] * decay[0:1, lanes(h)]
                          + ds_s[d, h, state_rows(chunk), :])

    st_s[...] = jnp.zeros_like(st_s)
    for j in range(n_ctx):
        for d in range(2):
            _, cz_ref, lb, reverse, _, _ = dirs[d]
            sl = pl.ds((n_ctx - 1 - j if reverse else j) * c, c)
            for h in range(n_heads):
                _, _, k_dec, _, decay = gate_decay(cz_ref[sl, lanes(h)], lb[:, lanes(h)],
                                                   reverse, cumsum_slot(j % SCAN_UNROLL, d, h))
                update_state(d, h, cv_ref[sl, lanes(h)], k_dec, decay[0:1, :])

    def chunk_of(d, idx):
        return n_lat - 1 - idx if dirs[d][3] else idx

    def for_chunks(stage, first):
        for u in range(SCAN_UNROLL):
            idx = first + u
            idx = min(idx, n_lat - 1) if isinstance(idx, int) else jnp.minimum(idx, n_lat - 1)
            for d in range(2):
                stage(d, chunk_of(d, idx), u)

    for_chunks(prep, 0)
    for_chunks(prep, SCAN_UNROLL)
    for_chunks(local, 0)

    def body(it, carry):
        for_chunks(step, it * SCAN_UNROLL)
        for_chunks(local, (it + 1) * SCAN_UNROLL)
        for_chunks(prep, (it + 2) * SCAN_UNROLL)
        return carry

    lax.fori_loop(0, n_lat // SCAN_UNROLL, body, 0)

    og = og_ref[...]
    for h in range(n_heads):
        o = of_s[:, lanes(h)] + ob_s[:, lanes(h)]
        o = o * lax.rsqrt(jnp.mean(o * o, axis=-1, keepdims=True) + EPS)
        o_ref[:, lanes(h)] = (o * og[:, lanes(h)]
                              * _silu(g_ref[:, lanes(h)].astype(F32))).astype(o_ref.dtype)


def _hgrn2_scan(zf, zb, v, q, g, czf, czb, cv, lb_f, lb_b, onorm_g):
    nb, n, _ = zf.shape
    nc = czf.shape[1]
    w = SCAN_HEADS * HG_DK
    lat = pl.BlockSpec((None, n, w), lambda b, h: (b, 0, h))
    ctx = pl.BlockSpec((None, nc, w), lambda b, h: (b, 0, h))
    vec = pl.BlockSpec((1, w), lambda b, h: (0, h))
    return pl.pallas_call(
        _scan_kernel,
        grid=(nb, HG_HEADS // SCAN_HEADS),
        in_specs=[lat] * 5 + [ctx] * 3 + [vec] * 3,
        out_specs=lat,
        out_shape=jax.ShapeDtypeStruct((nb, n, HG_V), BF16),
        scratch_shapes=[pltpu.VMEM((2, n, w), BF16),
                        pltpu.VMEM((2, n, w), BF16),
                        pltpu.VMEM((2, n, w), BF16),
                        pltpu.VMEM((2, n // SCAN_CHUNK * SUBLANES, w), F32),
                        pltpu.VMEM((2, n // SCAN_CHUNK * SUBLANES, w), F32),
                        pltpu.VMEM((SCAN_UNROLL * 2 * SCAN_HEADS, 4, CHUNK, HG_DK), F32),
                        pltpu.VMEM((2, SCAN_HEADS, n, SCAN_CHUNK), BF16),
                        pltpu.VMEM((2, SCAN_HEADS, n // SCAN_CHUNK * HG_DV, HG_DK), F32),
                        pltpu.VMEM((2, SCAN_HEADS, HG_DV, HG_DK), F32),
                        pltpu.VMEM((n, w), F32),
                        pltpu.VMEM((n, w), F32)],
        compiler_params=pltpu.CompilerParams(
            dimension_semantics=("arbitrary", "arbitrary"), vmem_limit_bytes=VMEM_LIMIT),
        name="hgrn2_scan",
    )(zf, zb, v, q, g, czf, czb, cv, lb_f, lb_b, onorm_g)


def _window(n, w):
    t = np.arange(n)
    lo = np.maximum(t - w // 2, 0)
    hi = np.minimum(t + w // 2 - 1, n - 1)
    s = np.arange(n)
    inside = (s[None, :] >= lo[:, None]) & (s[None, :] <= hi[:, None])
    return inside, (hi - lo + 1)


def _pool_operators(n, on_grid):
    mats, icnts = [], []
    for w in POOL_WINDOWS:
        if on_grid:
            rows = n // GRID_W
            in_r, cnt_r = _window(rows, w)
            in_c, cnt_c = _window(GRID_W, w)
            m = (in_r[:, None, :, None] & in_c[None, :, None, :]).reshape(n, n)
            cnt = (cnt_r[:, None] * cnt_c[None, :]).reshape(n)
        else:
            m, cnt = _window(n, w)
        mats.append(m)
        icnts.append(1.0 / cnt.astype(np.float64))
    p01 = jnp.asarray(np.stack(mats).astype(np.float32), dtype=BF16)
    icnt = jnp.asarray(np.stack(icnts).astype(np.float32)[..., None])
    return p01, icnt


def kernel(x, c, ctx, c_ctx, ada_w, ada_b, pre_g, post_g, ev_w_in, ev_pool_w, ev_pool_scale,
           ev_conv_w, ev_conv_b, ev_w_out, od_w_in, od_onorm_g, od_w_out, lb_logits):
    nb, n, d = x.shape
    nc = ctx.shape[1]
    lat_row = lambda b: b
    ctx_row = lambda b: CTX_ROW

    lb_table = jnp.cumsum(jax.nn.softmax(lb_logits.astype(F32), axis=1), axis=1)

    cc = jnp.zeros((MOD_ROWS, d), F32).at[:nb].set(c).at[CTX_ROW].set(c_ctx)
    mod = _ada_table(cc, ada_w, ada_b)
    mod = mod.reshape(2, MOD_ROWS, 3, 1, d)
    shift = [mod[l, :, 0] for l in range(2)]
    scale = [mod[l, :, 1] for l in range(2)]
    gate = [mod[l, :, 2] for l in range(2)]

    ctx_flat = ctx.reshape(1, nb * nc, d)

    pool_w = ev_pool_w[0].astype(BF16)
    pool_scale = ev_pool_scale[0].reshape(1, POOL_W)
    conv_w = ev_conv_w[0]
    conv_b = ev_conv_b[0].reshape(1, CONV_W)
    pre0 = pre_g[0].reshape(1, d)
    post0 = post_g[0].reshape(1, d)

    pre1 = pre_g[1].reshape(1, d)
    post1 = post_g[1].reshape(1, d)
    norm1 = (shift[1], scale[1], pre1)

    def even_mixer(tokens3, mod_row, on_grid):
        b3, n3, _ = tokens3.shape
        h = _norm_tokens(tokens3, mod_row, shift[0], scale[0], pre0, TM_NORM)
        proj, = _inproj(h.reshape(b3 * n3, d), ev_w_in[0], EVEN_IN, [BF16], TM_IN, 1024)
        n_tok = n if on_grid else nc
        proj = proj.reshape(nb, n_tok, EVEN_IN)
        p01, icnt = _pool_operators(n_tok, on_grid)
        a_out = _pool_mixer(proj, p01, icnt, pool_w, pool_scale)
        b_out = _conv_mixer(proj, conv_w, conv_b)
        acts = [a_out.reshape(b3, n3, POOL_W), b_out.reshape(b3, n3, CONV_W)]
        return _outproj(acts, ev_w_out[0], tokens3, mod_row, gate[0], post0, TM_OUT, norm1)

    x1, h1 = even_mixer(x, lat_row, True)
    _, hc1 = even_mixer(ctx_flat, ctx_row, False)

    tn1 = SCAN_HEADS * HG_DK
    zf, zb, v, q, g = [t.reshape(nb, n, HG_K) for t in
                       _inproj(h1.reshape(nb * n, d), od_w_in[0], HG_K,
                               [F32, F32, BF16, BF16, BF16], TM_IN, tn1)]
    czf, czb, cv = [t.reshape(nb, nc, HG_K) for t in
                    _inproj(hc1.reshape(nb * nc, d), od_w_in[0], HG_K,
                            [F32, F32, BF16], TM_IN, tn1)]
    o = _hgrn2_scan(zf, zb, v, q, g, czf, czb, cv, lb_table[0, 1].reshape(1, HG_K),
                    lb_table[1, 1].reshape(1, HG_K), od_onorm_g[0].reshape(1, HG_V))
    return _outproj([o], od_w_out[0], x1, lat_row, gate[1], post1, TM_OUT)
```

```python
import functools

import numpy as np
import jax
import jax.numpy as jnp
from jax import lax
from jax.experimental import pallas as pl
from jax.experimental.pallas import tpu as pltpu

F32 = jnp.float32
BF16 = jnp.bfloat16

D_MODEL = 2048
BATCH = 4
SEQ = 2048
CTX_LEN = 256
GRID_W = 64
EPS = 1e-6

POOL_WINDOWS = (2, 4, 8, 16)
POOL_W = D_MODEL // 2
POOL_GROUP = POOL_W // len(POOL_WINDOWS)
CONV_W = D_MODEL // 2
EVEN_IN = 2 * POOL_W + 4 * CONV_W

HG_DK = 128
HG_HEADS = D_MODEL // HG_DK
HG_DV = D_MODEL // HG_HEADS
HG_K = HG_HEADS * HG_DK
HG_V = HG_HEADS * HG_DV
ODD_IN = 3 * HG_K + 2 * HG_V
CHUNK = 64

MOD_ROWS = 8
CTX_ROW = BATCH
VMEM_LIMIT = 56 * 1024 * 1024
TM_NORM = 512
TM_IN = 1024
TM_OUT = 512


def _silu(v):
    return v * jax.nn.sigmoid(v)


def _ada_kernel(cc_ref, w_ref, b_ref, o_ref):
    s = _silu(cc_ref[...])
    o_ref[...] = jnp.dot(s, w_ref[...], preferred_element_type=F32) + b_ref[...]


def _ada_table(cc, ada_w, ada_b):
    depth = ada_w.shape[0]
    tn = 1024
    return pl.pallas_call(
        _ada_kernel,
        grid=(depth, 3 * D_MODEL // tn),
        in_specs=[
            pl.BlockSpec((MOD_ROWS, D_MODEL), lambda l, j: (0, 0)),
            pl.BlockSpec((None, D_MODEL, tn), lambda l, j: (l, 0, j)),
            pl.BlockSpec((None, 1, tn), lambda l, j: (l, 0, j)),
        ],
        out_specs=pl.BlockSpec((None, MOD_ROWS, tn), lambda l, j: (l, 0, j)),
        out_shape=jax.ShapeDtypeStruct((depth, MOD_ROWS, 3 * D_MODEL), F32),
        compiler_params=pltpu.CompilerParams(
            dimension_semantics=("arbitrary", "arbitrary"), vmem_limit_bytes=VMEM_LIMIT),
        name="ada_table",
    )(cc, ada_w, ada_b.reshape(depth, 1, 3 * D_MODEL))


def _norm_mod(x, g, scale, shift):
    ms = jnp.mean(x * x, axis=-1, keepdims=True)
    y = x * lax.rsqrt(ms + EPS) * g
    return y * (1.0 + scale) + shift


def _norm_kernel(x_ref, shift_ref, scale_ref, g_ref, o_ref):
    o_ref[...] = _norm_mod(x_ref[...], g_ref[...], scale_ref[...],
                           shift_ref[...]).astype(o_ref.dtype)


def _norm_tokens(x3, mod_row_of_batch, shift, scale, g, tm):
    nb, rows, _ = x3.shape
    mod_map = lambda b, i: (mod_row_of_batch(b), 0, 0)
    return pl.pallas_call(
        _norm_kernel,
        grid=(nb, rows // tm),
        in_specs=[
            pl.BlockSpec((None, tm, D_MODEL), lambda b, i: (b, i, 0)),
            pl.BlockSpec((None, 1, D_MODEL), mod_map),
            pl.BlockSpec((None, 1, D_MODEL), mod_map),
            pl.BlockSpec((1, D_MODEL), lambda b, i: (0, 0)),
        ],
        out_specs=pl.BlockSpec((None, tm, D_MODEL), lambda b, i: (b, i, 0)),
        out_shape=jax.ShapeDtypeStruct((nb, rows, D_MODEL), BF16),
        compiler_params=pltpu.CompilerParams(
            dimension_semantics=("arbitrary", "arbitrary"), vmem_limit_bytes=VMEM_LIMIT),
        name="norm_tokens",
    )(x3, shift, scale, g)


def _inproj_kernel(n_seg, h_ref, *refs):
    w_refs = refs[:n_seg]
    o_refs = refs[n_seg:2 * n_seg]
    wb_refs = refs[2 * n_seg:]

    @pl.when(pl.program_id(1) == 0)
    def _():
        for w_ref, wb_ref in zip(w_refs, wb_refs):
            wb_ref[...] = w_ref[...].astype(BF16)

    for wb_ref, o_ref in zip(wb_refs, o_refs):
        o_ref[...] = jnp.dot(h_ref[...], wb_ref[...],
                             preferred_element_type=F32).astype(o_ref.dtype)


def _inproj(h, w, seg_width, seg_dtypes, tm, tn):
    rows = h.shape[0]
    n_seg = len(seg_dtypes)
    n_tiles = seg_width // tn
    w_specs = [pl.BlockSpec((D_MODEL, tn), functools.partial(
        lambda k, j, i: (0, k * n_tiles + j), k)) for k in range(n_seg)]
    return pl.pallas_call(
        functools.partial(_inproj_kernel, n_seg),
        grid=(n_tiles, rows // tm),
        in_specs=[pl.BlockSpec((tm, D_MODEL), lambda j, i: (i, 0))] + w_specs,
        out_specs=[pl.BlockSpec((tm, tn), lambda j, i: (i, j))] * n_seg,
        out_shape=[jax.ShapeDtypeStruct((rows, seg_width), dt) for dt in seg_dtypes],
        scratch_shapes=[pltpu.VMEM((D_MODEL, tn), BF16)] * n_seg,
        compiler_params=pltpu.CompilerParams(
            dimension_semantics=("arbitrary", "arbitrary"), vmem_limit_bytes=VMEM_LIMIT),
        name="inproj",
    )(h, *([w] * n_seg))


POOL_SLAB = 256
POOL_ROWS = 128


def _pool_kernel(taps, pad, colop_ref, icnt_ref, v_ref, gate_ref, w_ref, sc_ref, o_ref,
                 box_s, pooled_s):
    n = v_ref.shape[0]
    if pad:
        box_s[0:pad, :] = jnp.zeros((pad, POOL_GROUP), F32)
        box_s[pad + n:pad + n + pad, :] = jnp.zeros((pad, POOL_GROUP), F32)
    for gi, (lo, hi) in enumerate(taps):
        lanes = slice(gi * POOL_GROUP, (gi + 1) * POOL_GROUP)
        for s in range(n // POOL_SLAB):
            tok = slice(s * POOL_SLAB, (s + 1) * POOL_SLAB)
            box_s[pad + s * POOL_SLAB:pad + (s + 1) * POOL_SLAB, :] = jnp.dot(
                colop_ref[gi], v_ref[tok, lanes], preferred_element_type=F32)

        def block(i, carry, gi=gi, lo=lo, hi=hi, lanes=lanes):
            r0 = pl.multiple_of(i * POOL_ROWS, POOL_ROWS)
            tok = pl.ds(r0, POOL_ROWS)
            acc = box_s[pl.ds(pad + lo * GRID_W + r0, POOL_ROWS), :]
            for dlt in range(lo + 1, hi + 1):
                acc = acc + box_s[pl.ds(pad + dlt * GRID_W + r0, POOL_ROWS), :]
            pooled = acc * icnt_ref[gi, tok, :] - v_ref[tok, lanes].astype(F32)
            pooled_s[tok, :] = pooled.astype(BF16)
            return carry

        lax.fori_loop(0, n // POOL_ROWS, block, 0, unroll=2)
        mixed = jnp.dot(pooled_s[...], w_ref[gi], preferred_element_type=F32)
        o_ref[:, lanes] = (mixed * sc_ref[:, lanes]
                           * _silu(gate_ref[:, lanes].astype(F32))).astype(o_ref.dtype)


def _pool_mixer(proj, colop, icnt, taps, pool_w, pool_scale):
    nb, n, _ = proj.shape
    ng = len(POOL_WINDOWS)
    pad = max(max(-lo, hi) for lo, hi in taps) * GRID_W
    pad = -(-pad // POOL_ROWS) * POOL_ROWS
    return pl.pallas_call(
        functools.partial(_pool_kernel, taps, pad),
        grid=(nb,),
        in_specs=[
            pl.BlockSpec((ng, POOL_SLAB, POOL_SLAB), lambda b: (0, 0, 0)),
            pl.BlockSpec((ng, n, 1), lambda b: (0, 0, 0)),
            pl.BlockSpec((None, n, POOL_W), lambda b: (b, 0, 0)),
            pl.BlockSpec((None, n, POOL_W), lambda b: (b, 0, 1)),
            pl.BlockSpec((ng, POOL_GROUP, POOL_GROUP), lambda b: (0, 0, 0)),
            pl.BlockSpec((1, POOL_W), lambda b: (0, 0)),
        ],
        out_specs=pl.BlockSpec((None, n, POOL_W), lambda b: (b, 0, 0)),
        out_shape=jax.ShapeDtypeStruct((nb, n, POOL_W), BF16),
        scratch_shapes=[pltpu.VMEM((n + 2 * pad, POOL_GROUP), F32),
                        pltpu.VMEM((n, POOL_GROUP), BF16)],
        compiler_params=pltpu.CompilerParams(
            dimension_semantics=("arbitrary",), vmem_limit_bytes=VMEM_LIMIT),
        name="pool_mixer",
    )(colop, icnt, proj, proj, pool_w, pool_scale)


def _conv_kernel(x_ref, b_ref, c_ref, gate_ref, cw_ref, cb_ref, o_ref):
    n = x_ref.shape[0]
    u = c_ref[...].astype(F32) * x_ref[...].astype(F32)
    row = lax.broadcasted_iota(jnp.int32, u.shape, 0)
    u_prev = jnp.where(row == 0, 0.0, pltpu.roll(u, 1, axis=0))
    u_next = jnp.where(row == n - 1, 0.0, pltpu.roll(u, n - 1, axis=0))
    cw = cw_ref[...]
    conv = u_prev * cw[0:1, :] + u * cw[1:2, :] + u_next * cw[2:3, :] + cb_ref[...]
    o_ref[...] = (b_ref[...].astype(F32) * conv
                  * _silu(gate_ref[...].astype(F32))).astype(o_ref.dtype)


def _conv_mixer(proj, conv_w, conv_b):
    nb, n, _ = proj.shape
    tc = 256
    nblk = CONV_W // tc
    base = 2 * POOL_W // tc

    def seg(k):
        return pl.BlockSpec((None, n, tc), lambda b, j: (b, 0, base + k * nblk + j))

    return pl.pallas_call(
        _conv_kernel,
        grid=(nb, nblk),
        in_specs=[
            seg(0), seg(1), seg(2), seg(3),
            pl.BlockSpec((3, tc), lambda b, j: (0, j)),
            pl.BlockSpec((1, tc), lambda b, j: (0, j)),
        ],
        out_specs=pl.BlockSpec((None, n, tc), lambda b, j: (b, 0, j)),
        out_shape=jax.ShapeDtypeStruct((nb, n, CONV_W), BF16),
        compiler_params=pltpu.CompilerParams(
            dimension_semantics=("arbitrary", "arbitrary"), vmem_limit_bytes=VMEM_LIMIT),
        name="conv_mixer",
    )(proj, proj, proj, proj, conv_w, conv_b)


def _outproj_kernel(n_in, emit_next, *refs):
    a_refs = refs[:n_in]
    w_refs = refs[n_in:2 * n_in]
    x_ref, gate_ref, pg_ref = refs[2 * n_in:2 * n_in + 3]
    rest = refs[2 * n_in + 3:]
    if emit_next:
        shift_ref, scale_ref, g_ref, o_ref, h_ref = rest[:5]
        wb_refs = rest[5:]
    else:
        o_ref = rest[0]
        wb_refs = rest[1:]

    @pl.when((pl.program_id(0) == 0) & (pl.program_id(1) == 0))
    def _():
        for w_ref, wb_ref in zip(w_refs, wb_refs):
            wb_ref[...] = w_ref[...].astype(BF16)

    y = jnp.dot(a_refs[0][...], wb_refs[0][...], preferred_element_type=F32)
    for a_ref, wb_ref in zip(a_refs[1:], wb_refs[1:]):
        y = y + jnp.dot(a_ref[...], wb_ref[...], preferred_element_type=F32)
    ms = jnp.mean(y * y, axis=-1, keepdims=True)
    r = y * lax.rsqrt(ms + EPS) * pg_ref[...]
    x_new = x_ref[...] + gate_ref[...] * r
    o_ref[...] = x_new
    if emit_next:
        h_ref[...] = _norm_mod(x_new, g_ref[...], scale_ref[...],
                               shift_ref[...]).astype(h_ref.dtype)


def _outproj(acts, w, x3, mod_row_of_batch, gate, pg, tm, next_norm=None):
    nb, rows, _ = x3.shape
    n_in = len(acts)
    k_in = acts[0].shape[-1]
    assert all(a.shape[-1] == k_in for a in acts) and n_in * k_in == w.shape[0]
    emit_next = next_norm is not None
    mod_spec = pl.BlockSpec((None, 1, D_MODEL), lambda b, i: (mod_row_of_batch(b), 0, 0))
    vec_spec = pl.BlockSpec((1, D_MODEL), lambda b, i: (0, 0))
    tok_spec = pl.BlockSpec((None, tm, D_MODEL), lambda b, i: (b, i, 0))
    in_specs = [pl.BlockSpec((None, tm, k_in), lambda b, i: (b, i, 0))] * n_in
    in_specs += [pl.BlockSpec((k_in, D_MODEL), functools.partial(lambda k, b, i: (k, 0), k),
                              pipeline_mode=pl.Buffered(1)) for k in range(n_in)]
    in_specs += [tok_spec, mod_spec, vec_spec]
    operands = [*acts, *([w] * n_in), x3, gate, pg]
    out_specs, out_shape = tok_spec, jax.ShapeDtypeStruct((nb, rows, D_MODEL), F32)
    if emit_next:
        in_specs += [mod_spec, mod_spec, vec_spec]
        operands += list(next_norm)
        out_specs = [tok_spec, tok_spec]
        out_shape = [out_shape, jax.ShapeDtypeStruct((nb, rows, D_MODEL), BF16)]
    return pl.pallas_call(
        functools.partial(_outproj_kernel, n_in, emit_next),
        grid=(nb, rows // tm),
        in_specs=in_specs,
        out_specs=out_specs,
        out_shape=out_shape,
        scratch_shapes=[pltpu.VMEM((k_in, D_MODEL), BF16)] * n_in,
        compiler_params=pltpu.CompilerParams(
            dimension_semantics=("arbitrary", "arbitrary"), vmem_limit_bytes=VMEM_LIMIT),
        name="outproj",
    )(*operands)


_NT = (((1,), (1,)), ((), ()))
_TN = (((0,), (0,)), ((), ()))
SUBLANES = 8
SCAN_HEADS = 2
SCAN_UNROLL = 1
SCAN_CHUNK = 2 * CHUNK


def _chunk_cumsum(x, reverse, in_ref, out_ref):
    c, w = x.shape
    nblk = c // SUBLANES
    order = range(nblk - 1, -1, -1) if reverse else range(nblk)
    in_ref[...] = x
    pref = [None] * nblk
    acc = None
    for j in order:
        blk = in_ref[pl.ds(j, SUBLANES, stride=nblk), :]
        acc = blk if acc is None else acc + blk
        pref[j] = acc
    total = acc
    sub = lax.broadcasted_iota(jnp.int32, (SUBLANES, w), 0)
    incl = total
    for s in (1, 2, 4):
        if reverse:
            incl = incl + jnp.where(sub < SUBLANES - s,
                                    pltpu.roll(incl, SUBLANES - s, axis=0), 0.0)
        else:
            incl = incl + jnp.where(sub >= s, pltpu.roll(incl, s, axis=0), 0.0)
    before = incl - total
    for j in range(nblk):
        out_ref[pl.ds(j, SUBLANES, stride=nblk), :] = pref[j] + before
    edge = incl[0:1, :] if reverse else incl[SUBLANES - 1:SUBLANES, :]
    return out_ref[...], jnp.broadcast_to(edge, (SUBLANES, w))


def _scan_kernel(zf_ref, zb_ref, v_ref, q_ref, g_ref, czf_ref, czb_ref, cv_ref,
                 lbf_ref, lbb_ref, og_ref, o_ref,
                 qd_s, ki_s, kd_s, dec_s, mid_s, cs_s, sc_s, ds_s, st_s, of_s, ob_s):
    c = SCAN_CHUNK
    half = CHUNK
    n_lat = zf_ref.shape[0] // c
    n_ctx = czf_ref.shape[0] // c
    n_heads = zf_ref.shape[1] // HG_DK
    row = lax.broadcasted_iota(jnp.int32, (c, c), 0)
    col = lax.broadcasted_iota(jnp.int32, (c, c), 1)
    dirs = ((zf_ref, czf_ref, lbf_ref[...], False, col <= row, of_s),
            (zb_ref, czb_ref, lbb_ref[...], True, col >= row, ob_s))

    def lanes(h):
        return slice(h * HG_DK, (h + 1) * HG_DK)

    def rows(chunk):
        return pl.ds(pl.multiple_of(chunk * c, c), c)

    def dec_rows(chunk):
        return pl.ds(pl.multiple_of(chunk * SUBLANES, SUBLANES), SUBLANES)

    def gate_decay(z, lb, reverse, slot):
        bt = (0.5 * (1.0 - lb)) * jnp.tanh(0.5 * z)
        k = 0.5 * (1.0 - lb) - bt
        lf = jnp.log(0.5 * (1.0 + lb) + bt)
        lo, hi = lf[:half], lf[half:]
        first, second = (hi, lo) if reverse else (lo, hi)
        away, t_first = _chunk_cumsum(first, not reverse, cs_s.at[slot, 0], cs_s.at[slot, 1])
        toward, t_second = _chunk_cumsum(second, reverse, cs_s.at[slot, 2], cs_s.at[slot, 3])
        a_first = first - away
        a = jnp.concatenate([toward, a_first] if reverse else [a_first, toward], axis=0)
        k_inv = k * jnp.exp(-a)
        d_first = jnp.exp(t_first)
        d_second = jnp.exp(t_second)
        k_dec = (k_inv * d_second[0:1, :]).astype(BF16)
        return a, k_inv, k_dec, d_first, d_first * d_second

    def update_state(d, h, v, k_dec, decay):
        ds_t = lax.dot_general(v, k_dec, _TN, preferred_element_type=F32)
        st_s[d, h] = st_s[d, h] * decay + ds_t

    def cumsum_slot(u, d, h):
        return (u * 2 + d) * n_heads + h

    def prep(d, chunk, u):
        z_ref, _, lb, reverse, _, _ = dirs[d]
        sl = rows(chunk)
        for h in range(n_heads):
            a, k_inv, k_dec, d_first, decay = gate_decay(
                z_ref[sl, lanes(h)], lb[:, lanes(h)], reverse, cumsum_slot(u, d, h))
            qd_s[d, sl, lanes(h)] = (q_ref[sl, lanes(h)].astype(F32) * jnp.exp(a)).astype(BF16)
            ki_s[d, sl, lanes(h)] = k_inv.astype(BF16)
            kd_s[d, sl, lanes(h)] = k_dec
            dec_s[d, dec_rows(chunk), lanes(h)] = decay
            mid_s[d, dec_rows(chunk), lanes(h)] = d_first

    def state_rows(chunk):
        return pl.ds(pl.multiple_of(chunk * HG_DV, HG_DV), HG_DV)

    def local(d, chunk, _):
        mask = dirs[d][4]
        sl = rows(chunk)
        for h in range(n_heads):
            sc = lax.dot_general(qd_s[d, sl, lanes(h)], ki_s[d, sl, lanes(h)], _NT,
                                 preferred_element_type=F32)
            sc_s[d, h, sl, :] = jnp.where(mask, sc, 0.0).astype(BF16)
            ds_s[d, h, state_rows(chunk), :] = lax.dot_general(
                v_ref[sl, lanes(h)], kd_s[d, sl, lanes(h)], _TN, preferred_element_type=F32)

    def step(d, chunk, _):
        out_s = dirs[d][5]
        sl = rows(chunk)
        decay = dec_s[d, dec_rows(chunk), :]
        d_first = mid_s[d, dec_rows(chunk), :]
        for h in range(n_heads):
            st_mid = (st_s[d, h] * d_first[0:1, lanes(h)]).astype(BF16)
            inter = lax.dot_general(qd_s[d, sl, lanes(h)], st_mid, _NT,
                                    preferred_element_type=F32)
            out_s[sl, lanes(h)] = inter + jnp.dot(sc_s[d, h, sl, :], v_ref[sl, lanes(h)],
                                                  preferred_element_type=F32)
            st_s[d, h] = (st_s[d, h] * decay[0:1, lanes(h)]
                          + ds_s[d, h, state_rows(chunk), :])

    st_s[...] = jnp.zeros_like(st_s)
    for j in range(n_ctx):
        for d in range(2):
            _, cz_ref, lb, reverse, _, _ = dirs[d]
            sl = pl.ds((n_ctx - 1 - j if reverse else j) * c, c)
            for h in range(n_heads):
                _, _, k_dec, _, decay = gate_decay(cz_ref[sl, lanes(h)], lb[:, lanes(h)],
                                                   reverse, cumsum_slot(j % SCAN_UNROLL, d, h))
                update_state(d, h, cv_ref[sl, lanes(h)], k_dec, decay[0:1, :])

    def chunk_of(d, idx):
        return n_lat - 1 - idx if dirs[d][3] else idx

    def for_chunks(stage, first):
        for u in range(SCAN_UNROLL):
            idx = first + u
            idx = min(idx, n_lat - 1) if isinstance(idx, int) else jnp.minimum(idx, n_lat - 1)
            for d in range(2):
                stage(d, chunk_of(d, idx), u)

    for_chunks(prep, 0)
    for_chunks(prep, SCAN_UNROLL)
    for_chunks(local, 0)

    def body(it, carry):
        for_chunks(step, it * SCAN_UNROLL)
        for_chunks(local, (it + 1) * SCAN_UNROLL)
        for_chunks(prep, (it + 2) * SCAN_UNROLL)
        return carry

    lax.fori_loop(0, n_lat // SCAN_UNROLL, body, 0)

    og = og_ref[...]
    for h in range(n_heads):
        o = of_s[:, lanes(h)] + ob_s[:, lanes(h)]
        o = o * lax.rsqrt(jnp.mean(o * o, axis=-1, keepdims=True) + EPS)
        o_ref[:, lanes(h)] = (o * og[:, lanes(h)]
                              * _silu(g_ref[:, lanes(h)].astype(F32))).astype(o_ref.dtype)


def _hgrn2_scan(zf, zb, v, q, g, czf, czb, cv, lb_f, lb_b, onorm_g):
    nb, n, _ = zf.shape
    nc = czf.shape[1]
    w = SCAN_HEADS * HG_DK
    lat = pl.BlockSpec((None, n, w), lambda b, h: (b, 0, h))
    ctx = pl.BlockSpec((None, nc, w), lambda b, h: (b, 0, h))
    vec = pl.BlockSpec((1, w), lambda b, h: (0, h))
    return pl.pallas_call(
        _scan_kernel,
        grid=(nb, HG_HEADS // SCAN_HEADS),
        in_specs=[lat] * 5 + [ctx] * 3 + [vec] * 3,
        out_specs=lat,
        out_shape=jax.ShapeDtypeStruct((nb, n, HG_V), BF16),
        scratch_shapes=[pltpu.VMEM((2, n, w), BF16),
                        pltpu.VMEM((2, n, w), BF16),
                        pltpu.VMEM((2, n, w), BF16),
                        pltpu.VMEM((2, n // SCAN_CHUNK * SUBLANES, w), F32),
                        pltpu.VMEM((2, n // SCAN_CHUNK * SUBLANES, w), F32),
                        pltpu.VMEM((SCAN_UNROLL * 2 * SCAN_HEADS, 4, CHUNK, HG_DK), F32),
                        pltpu.VMEM((2, SCAN_HEADS, n, SCAN_CHUNK), BF16),
                        pltpu.VMEM((2, SCAN_HEADS, n // SCAN_CHUNK * HG_DV, HG_DK), F32),
                        pltpu.VMEM((2, SCAN_HEADS, HG_DV, HG_DK), F32),
                        pltpu.VMEM((n, w), F32),
                        pltpu.VMEM((n, w), F32)],
        compiler_params=pltpu.CompilerParams(
            dimension_semantics=("arbitrary", "arbitrary"), vmem_limit_bytes=VMEM_LIMIT),
        name="hgrn2_scan",
    )(zf, zb, v, q, g, czf, czb, cv, lb_f, lb_b, onorm_g)


def _window(n, w):
    t = np.arange(n)
    lo = np.maximum(t - w // 2, 0)
    hi = np.minimum(t + w // 2 - 1, n - 1)
    s = np.arange(n)
    inside = (s[None, :] >= lo[:, None]) & (s[None, :] <= hi[:, None])
    return inside, (hi - lo + 1)


def _pool_operators(n, on_grid):
    mats, icnts, taps = [], [], []
    for w in POOL_WINDOWS:
        if on_grid:
            in_c, cnt_c = _window(GRID_W, w)
            _, cnt_r = _window(n // GRID_W, w)
            m = np.kron(np.eye(POOL_SLAB // GRID_W, dtype=bool), in_c)
            cnt = (cnt_r[:, None] * cnt_c[None, :]).reshape(n)
            taps.append((-(w // 2), w // 2 - 1))
        else:
            assert n == POOL_SLAB
            m, cnt = _window(n, w)
            taps.append((0, 0))
        mats.append(m)
        icnts.append(1.0 / cnt.astype(np.float64))
    colop = jnp.asarray(np.stack(mats).astype(np.float32), dtype=BF16)
    icnt = jnp.asarray(np.stack(icnts).astype(np.float32)[..., None])
    return colop, icnt, tuple(taps)


def kernel(x, c, ctx, c_ctx, ada_w, ada_b, pre_g, post_g, ev_w_in, ev_pool_w, ev_pool_scale,
           ev_conv_w, ev_conv_b, ev_w_out, od_w_in, od_onorm_g, od_w_out, lb_logits):
    nb, n, d = x.shape
    nc = ctx.shape[1]
    lat_row = lambda b: b
    ctx_row = lambda b: CTX_ROW

    lb_table = jnp.cumsum(jax.nn.softmax(lb_logits.astype(F32), axis=1), axis=1)

    cc = jnp.zeros((MOD_ROWS, d), F32).at[:nb].set(c).at[CTX_ROW].set(c_ctx)
    mod = _ada_table(cc, ada_w, ada_b)
    mod = mod.reshape(2, MOD_ROWS, 3, 1, d)
    shift = [mod[l, :, 0] for l in range(2)]
    scale = [mod[l, :, 1] for l in range(2)]
    gate = [mod[l, :, 2] for l in range(2)]

    ctx_flat = ctx.reshape(1, nb * nc, d)

    pool_w = ev_pool_w[0].astype(BF16)
    pool_scale = ev_pool_scale[0].reshape(1, POOL_W)
    conv_w = ev_conv_w[0]
    conv_b = ev_conv_b[0].reshape(1, CONV_W)
    pre0 = pre_g[0].reshape(1, d)
    post0 = post_g[0].reshape(1, d)

    pre1 = pre_g[1].reshape(1, d)
    post1 = post_g[1].reshape(1, d)
    norm1 = (shift[1], scale[1], pre1)

    def even_mixer(tokens3, mod_row, on_grid):
        b3, n3, _ = tokens3.shape
        h = _norm_tokens(tokens3, mod_row, shift[0], scale[0], pre0, TM_NORM)
        proj, = _inproj(h.reshape(b3 * n3, d), ev_w_in[0], EVEN_IN, [BF16], TM_IN, 1024)
        n_tok = n if on_grid else nc
        proj = proj.reshape(nb, n_tok, EVEN_IN)
        colop, icnt, taps = _pool_operators(n_tok, on_grid)
        a_out = _pool_mixer(proj, colop, icnt, taps, pool_w, pool_scale)
        b_out = _conv_mixer(proj, conv_w, conv_b)
        acts = [a_out.reshape(b3, n3, POOL_W), b_out.reshape(b3, n3, CONV_W)]
        return _outproj(acts, ev_w_out[0], tokens3, mod_row, gate[0], post0, TM_OUT, norm1)

    x1, h1 = even_mixer(x, lat_row, True)
    _, hc1 = even_mixer(ctx_flat, ctx_row, False)

    tn1 = SCAN_HEADS * HG_DK
    zf, zb, v, q, g = [t.reshape(nb, n, HG_K) for t in
                       _inproj(h1.reshape(nb * n, d), od_w_in[0], HG_K,
                               [F32, F32, BF16, BF16, BF16], TM_IN, tn1)]
    czf, czb, cv = [t.reshape(nb, nc, HG_K) for t in
                    _inproj(hc1.reshape(nb * nc, d), od_w_in[0], HG_K,
                            [F32, F32, BF16], TM_IN, tn1)]
    o = _hgrn2_scan(zf, zb, v, q, g, czf, czb, cv, lb_table[0, 1].reshape(1, HG_K),
                    lb_table[1, 1].reshape(1, HG_K), od_onorm_g[0].reshape(1, HG_V))
    return _outproj([o], od_w_out[0], x1, lat_row, gate[1], post1, TM_OUT)
```

```python
import functools

import numpy as np
import jax
import jax.numpy as jnp
from jax import lax
from jax.experimental import pallas as pl
from jax.experimental.pallas import tpu as pltpu

F32 = jnp.float32
BF16 = jnp.bfloat16

D_MODEL = 2048
BATCH = 4
SEQ = 2048
CTX_LEN = 256
GRID_W = 64
EPS = 1e-6

POOL_WINDOWS = (2, 4, 8, 16)
POOL_W = D_MODEL // 2
POOL_GROUP = POOL_W // len(POOL_WINDOWS)
CONV_W = D_MODEL // 2
EVEN_IN = 2 * POOL_W + 4 * CONV_W

HG_DK = 128
HG_HEADS = D_MODEL // HG_DK
HG_DV = D_MODEL // HG_HEADS
HG_K = HG_HEADS * HG_DK
HG_V = HG_HEADS * HG_DV
ODD_IN = 3 * HG_K + 2 * HG_V
CHUNK = 64

MOD_ROWS = 8
CTX_ROW = BATCH
VMEM_LIMIT = 56 * 1024 * 1024
TM_NORM = 512
TM_IN = 1024
TM_OUT = 512


def _silu(v):
    hv = 0.5 * v
    return hv + hv * jnp.tanh(hv)


def _ada_kernel(cc_ref, w_ref, b_ref, o_ref):
    s = _silu(cc_ref[...])
    o_ref[...] = jnp.dot(s, w_ref[...], preferred_element_type=F32) + b_ref[...]


def _ada_table(cc, ada_w, ada_b):
    depth = ada_w.shape[0]
    tn = 1024
    return pl.pallas_call(
        _ada_kernel,
        grid=(depth, 3 * D_MODEL // tn),
        in_specs=[
            pl.BlockSpec((MOD_ROWS, D_MODEL), lambda l, j: (0, 0)),
            pl.BlockSpec((None, D_MODEL, tn), lambda l, j: (l, 0, j)),
            pl.BlockSpec((None, 1, tn), lambda l, j: (l, 0, j)),
        ],
        out_specs=pl.BlockSpec((None, MOD_ROWS, tn), lambda l, j: (l, 0, j)),
        out_shape=jax.ShapeDtypeStruct((depth, MOD_ROWS, 3 * D_MODEL), F32),
        compiler_params=pltpu.CompilerParams(
            dimension_semantics=("arbitrary", "arbitrary"), vmem_limit_bytes=VMEM_LIMIT),
        name="ada_table",
    )(cc, ada_w, ada_b.reshape(depth, 1, 3 * D_MODEL))


def _norm_mod(x, g, scale, shift):
    ms = jnp.mean(x * x, axis=-1, keepdims=True)
    y = x * lax.rsqrt(ms + EPS) * g
    return y * (1.0 + scale) + shift


def _norm_kernel(x_ref, shift_ref, scale_ref, g_ref, o_ref):
    o_ref[...] = _norm_mod(x_ref[...], g_ref[...], scale_ref[...],
                           shift_ref[...]).astype(o_ref.dtype)


def _norm_tokens(x3, mod_row_of_batch, shift, scale, g, tm):
    nb, rows, _ = x3.shape
    mod_map = lambda b, i: (mod_row_of_batch(b), 0, 0)
    return pl.pallas_call(
        _norm_kernel,
        grid=(nb, rows // tm),
        in_specs=[
            pl.BlockSpec((None, tm, D_MODEL), lambda b, i: (b, i, 0)),
            pl.BlockSpec((None, 1, D_MODEL), mod_map),
            pl.BlockSpec((None, 1, D_MODEL), mod_map),
            pl.BlockSpec((1, D_MODEL), lambda b, i: (0, 0)),
        ],
        out_specs=pl.BlockSpec((None, tm, D_MODEL), lambda b, i: (b, i, 0)),
        out_shape=jax.ShapeDtypeStruct((nb, rows, D_MODEL), BF16),
        compiler_params=pltpu.CompilerParams(
            dimension_semantics=("arbitrary", "arbitrary"), vmem_limit_bytes=VMEM_LIMIT),
        name="norm_tokens",
    )(x3, shift, scale, g)


def _inproj_kernel(seg_scales, h_ref, *refs):
    n_seg = len(seg_scales)
    w_refs = refs[:n_seg]
    o_refs = refs[n_seg:2 * n_seg]
    wb_refs = refs[2 * n_seg:]

    @pl.when(pl.program_id(1) == 0)
    def _():
        for w_ref, wb_ref in zip(w_refs, wb_refs):
            wb_ref[...] = w_ref[...].astype(BF16)

    for wb_ref, o_ref, scale in zip(wb_refs, o_refs, seg_scales):
        acc = jnp.dot(h_ref[...], wb_ref[...], preferred_element_type=F32)
        o_ref[...] = (acc if scale == 1.0 else acc * scale).astype(o_ref.dtype)


def _inproj(h, w, seg_width, seg_dtypes, tm, tn, seg_scales=None):
    rows = h.shape[0]
    n_seg = len(seg_dtypes)
    seg_scales = tuple(seg_scales or (1.0,) * n_seg)
    n_tiles = seg_width // tn
    w_specs = [pl.BlockSpec((D_MODEL, tn), functools.partial(
        lambda k, j, i: (0, k * n_tiles + j), k)) for k in range(n_seg)]
    return pl.pallas_call(
        functools.partial(_inproj_kernel, seg_scales),
        grid=(n_tiles, rows // tm),
        in_specs=[pl.BlockSpec((tm, D_MODEL), lambda j, i: (i, 0))] + w_specs,
        out_specs=[pl.BlockSpec((tm, tn), lambda j, i: (i, j))] * n_seg,
        out_shape=[jax.ShapeDtypeStruct((rows, seg_width), dt) for dt in seg_dtypes],
        scratch_shapes=[pltpu.VMEM((D_MODEL, tn), BF16)] * n_seg,
        compiler_params=pltpu.CompilerParams(
            dimension_semantics=("arbitrary", "arbitrary"), vmem_limit_bytes=VMEM_LIMIT),
        name="inproj",
    )(h, *([w] * n_seg))


POOL_SLAB = 256
POOL_ROWS = 128


def _pool_kernel(taps, pad, colop_ref, icnt_ref, v_ref, gate_ref, w_ref, sc_ref, o_ref,
                 box_s, pooled_s):
    n = v_ref.shape[0]
    if pad:
        box_s[0:pad, :] = jnp.zeros((pad, POOL_GROUP), F32)
        box_s[pad + n:pad + n + pad, :] = jnp.zeros((pad, POOL_GROUP), F32)
    for gi, (lo, hi) in enumerate(taps):
        lanes = slice(gi * POOL_GROUP, (gi + 1) * POOL_GROUP)
        for s in range(n // POOL_SLAB):
            tok = slice(s * POOL_SLAB, (s + 1) * POOL_SLAB)
            box_s[pad + s * POOL_SLAB:pad + (s + 1) * POOL_SLAB, :] = jnp.dot(
                colop_ref[gi], v_ref[tok, lanes], preferred_element_type=F32)

        def block(i, carry, gi=gi, lo=lo, hi=hi, lanes=lanes):
            r0 = pl.multiple_of(i * POOL_ROWS, POOL_ROWS)
            tok = pl.ds(r0, POOL_ROWS)
            acc = box_s[pl.ds(pad + lo * GRID_W + r0, POOL_ROWS), :]
            for dlt in range(lo + 1, hi + 1):
                acc = acc + box_s[pl.ds(pad + dlt * GRID_W + r0, POOL_ROWS), :]
            pooled = acc * icnt_ref[gi, tok, :] - v_ref[tok, lanes].astype(F32)
            pooled_s[tok, :] = pooled.astype(BF16)
            return carry

        lax.fori_loop(0, n // POOL_ROWS, block, 0, unroll=2)
        mixed = jnp.dot(pooled_s[...], w_ref[gi], preferred_element_type=F32)
        o_ref[:, lanes] = (mixed * sc_ref[:, lanes]
                           * _silu(gate_ref[:, lanes].astype(F32))).astype(o_ref.dtype)


def _pool_mixer(proj, colop, icnt, taps, pool_w, pool_scale):
    nb, n, _ = proj.shape
    ng = len(POOL_WINDOWS)
    pad = max(max(-lo, hi) for lo, hi in taps) * GRID_W
    pad = -(-pad // POOL_ROWS) * POOL_ROWS
    return pl.pallas_call(
        functools.partial(_pool_kernel, taps, pad),
        grid=(nb,),
        in_specs=[
            pl.BlockSpec((ng, POOL_SLAB, POOL_SLAB), lambda b: (0, 0, 0)),
            pl.BlockSpec((ng, n, 1), lambda b: (0, 0, 0)),
            pl.BlockSpec((None, n, POOL_W), lambda b: (b, 0, 0)),
            pl.BlockSpec((None, n, POOL_W), lambda b: (b, 0, 1)),
            pl.BlockSpec((ng, POOL_GROUP, POOL_GROUP), lambda b: (0, 0, 0)),
            pl.BlockSpec((1, POOL_W), lambda b: (0, 0)),
        ],
        out_specs=pl.BlockSpec((None, n, POOL_W), lambda b: (b, 0, 0)),
        out_shape=jax.ShapeDtypeStruct((nb, n, POOL_W), BF16),
        scratch_shapes=[pltpu.VMEM((n + 2 * pad, POOL_GROUP), F32),
                        pltpu.VMEM((n, POOL_GROUP), BF16)],
        compiler_params=pltpu.CompilerParams(
            dimension_semantics=("arbitrary",), vmem_limit_bytes=VMEM_LIMIT),
        name="pool_mixer",
    )(colop, icnt, proj, proj, pool_w, pool_scale)


def _conv_kernel(x_ref, b_ref, c_ref, gate_ref, cw_ref, cb_ref, o_ref):
    n = x_ref.shape[0]
    u = c_ref[...].astype(F32) * x_ref[...].astype(F32)
    row = lax.broadcasted_iota(jnp.int32, u.shape, 0)
    u_prev = jnp.where(row == 0, 0.0, pltpu.roll(u, 1, axis=0))
    u_next = jnp.where(row == n - 1, 0.0, pltpu.roll(u, n - 1, axis=0))
    cw = cw_ref[...]
    conv = u_prev * cw[0:1, :] + u * cw[1:2, :] + u_next * cw[2:3, :] + cb_ref[...]
    o_ref[...] = (b_ref[...].astype(F32) * conv
                  * _silu(gate_ref[...].astype(F32))).astype(o_ref.dtype)


def _conv_mixer(proj, conv_w, conv_b):
    nb, n, _ = proj.shape
    tc = 256
    nblk = CONV_W // tc
    base = 2 * POOL_W // tc

    def seg(k):
        return pl.BlockSpec((None, n, tc), lambda b, j: (b, 0, base + k * nblk + j))

    return pl.pallas_call(
        _conv_kernel,
        grid=(nb, nblk),
        in_specs=[
            seg(0), seg(1), seg(2), seg(3),
            pl.BlockSpec((3, tc), lambda b, j: (0, j)),
            pl.BlockSpec((1, tc), lambda b, j: (0, j)),
        ],
        out_specs=pl.BlockSpec((None, n, tc), lambda b, j: (b, 0, j)),
        out_shape=jax.ShapeDtypeStruct((nb, n, CONV_W), BF16),
        compiler_params=pltpu.CompilerParams(
            dimension_semantics=("arbitrary", "arbitrary"), vmem_limit_bytes=VMEM_LIMIT),
        name="conv_mixer",
    )(proj, proj, proj, proj, conv_w, conv_b)


def _outproj_kernel(n_in, emit_next, *refs):
    a_refs = refs[:n_in]
    w_refs = refs[n_in:2 * n_in]
    x_ref, gate_ref, pg_ref = refs[2 * n_in:2 * n_in + 3]
    rest = refs[2 * n_in + 3:]
    if emit_next:
        shift_ref, scale_ref, g_ref, o_ref, h_ref = rest[:5]
        wb_refs = rest[5:]
    else:
        o_ref = rest[0]
        wb_refs = rest[1:]

    @pl.when((pl.program_id(0) == 0) & (pl.program_id(1) == 0))
    def _():
        for w_ref, wb_ref in zip(w_refs, wb_refs):
            wb_ref[...] = w_ref[...].astype(BF16)

    y = jnp.dot(a_refs[0][...], wb_refs[0][...], preferred_element_type=F32)
    for a_ref, wb_ref in zip(a_refs[1:], wb_refs[1:]):
        y = y + jnp.dot(a_ref[...], wb_ref[...], preferred_element_type=F32)
    ms = jnp.mean(y * y, axis=-1, keepdims=True)
    r = y * lax.rsqrt(ms + EPS) * pg_ref[...]
    x_new = x_ref[...] + gate_ref[...] * r
    o_ref[...] = x_new
    if emit_next:
        h_ref[...] = _norm_mod(x_new, g_ref[...], scale_ref[...],
                               shift_ref[...]).astype(h_ref.dtype)


def _outproj(acts, w, x3, mod_row_of_batch, gate, pg, tm, next_norm=None):
    nb, rows, _ = x3.shape
    n_in = len(acts)
    k_in = acts[0].shape[-1]
    assert all(a.shape[-1] == k_in for a in acts) and n_in * k_in == w.shape[0]
    emit_next = next_norm is not None
    mod_spec = pl.BlockSpec((None, 1, D_MODEL), lambda b, i: (mod_row_of_batch(b), 0, 0))
    vec_spec = pl.BlockSpec((1, D_MODEL), lambda b, i: (0, 0))
    tok_spec = pl.BlockSpec((None, tm, D_MODEL), lambda b, i: (b, i, 0))
    in_specs = [pl.BlockSpec((None, tm, k_in), lambda b, i: (b, i, 0))] * n_in
    in_specs += [pl.BlockSpec((k_in, D_MODEL), functools.partial(lambda k, b, i: (k, 0), k),
                              pipeline_mode=pl.Buffered(1)) for k in range(n_in)]
    in_specs += [tok_spec, mod_spec, vec_spec]
    operands = [*acts, *([w] * n_in), x3, gate, pg]
    out_specs, out_shape = tok_spec, jax.ShapeDtypeStruct((nb, rows, D_MODEL), F32)
    if emit_next:
        in_specs += [mod_spec, mod_spec, vec_spec]
        operands += list(next_norm)
        out_specs = [tok_spec, tok_spec]
        out_shape = [out_shape, jax.ShapeDtypeStruct((nb, rows, D_MODEL), BF16)]
    return pl.pallas_call(
        functools.partial(_outproj_kernel, n_in, emit_next),
        grid=(nb, rows // tm),
        in_specs=in_specs,
        out_specs=out_specs,
        out_shape=out_shape,
        scratch_shapes=[pltpu.VMEM((k_in, D_MODEL), BF16)] * n_in,
        compiler_params=pltpu.CompilerParams(
            dimension_semantics=("arbitrary", "arbitrary"), vmem_limit_bytes=VMEM_LIMIT),
        name="outproj",
    )(*operands)


_NT = (((1,), (1,)), ((), ()))
_TN = (((0,), (0,)), ((), ()))
SUBLANES = 8
SCAN_HEADS = 2
SCAN_UNROLL = 1
SCAN_CHUNK = 2 * CHUNK


def _chunk_cumsum(x, reverse, in_ref, out_ref):
    c, w = x.shape
    nblk = c // SUBLANES
    order = range(nblk - 1, -1, -1) if reverse else range(nblk)
    in_ref[...] = x
    pref = [None] * nblk
    acc = None
    for j in order:
        blk = in_ref[pl.ds(j, SUBLANES, stride=nblk), :]
        acc = blk if acc is None else acc + blk
        pref[j] = acc
    total = acc
    sub = lax.broadcasted_iota(jnp.int32, (SUBLANES, w), 0)
    incl = total
    for s in (1, 2, 4):
        if reverse:
            incl = incl + jnp.where(sub < SUBLANES - s,
                                    pltpu.roll(incl, SUBLANES - s, axis=0), 0.0)
        else:
            incl = incl + jnp.where(sub >= s, pltpu.roll(incl, s, axis=0), 0.0)
    before = incl - total
    for j in range(nblk):
        out_ref[pl.ds(j, SUBLANES, stride=nblk), :] = pref[j] + before
    edge = incl[0:1, :] if reverse else incl[SUBLANES - 1:SUBLANES, :]
    return out_ref[...], jnp.broadcast_to(edge, (SUBLANES, w))


def _scan_kernel(zf_ref, zb_ref, v_ref, q_ref, g_ref, czf_ref, czb_ref, cv_ref,
                 lbf_ref, lbb_ref, og_ref, o_ref,
                 qd_s, ki_s, kd_s, dec_s, mid_s, cs_s, sc_s, ds_s, st_s, of_s, ob_s):
    c = SCAN_CHUNK
    half = CHUNK
    n_lat = zf_ref.shape[0] // c
    n_ctx = czf_ref.shape[0] // c
    n_heads = zf_ref.shape[1] // HG_DK
    row = lax.broadcasted_iota(jnp.int32, (c, c), 0)
    col = lax.broadcasted_iota(jnp.int32, (c, c), 1)
    dirs = ((zf_ref, czf_ref, lbf_ref[...], False, col <= row, of_s),
            (zb_ref, czb_ref, lbb_ref[...], True, col >= row, ob_s))

    def lanes(h):
        return slice(h * HG_DK, (h + 1) * HG_DK)

    def rows(chunk):
        return pl.ds(pl.multiple_of(chunk * c, c), c)

    def dec_rows(chunk):
        return pl.ds(pl.multiple_of(chunk * SUBLANES, SUBLANES), SUBLANES)

    def gate_decay(zh, lb, reverse, slot):
        bt = (0.5 * (1.0 - lb)) * jnp.tanh(zh)
        k = 0.5 * (1.0 - lb) - bt
        lf = jnp.log(0.5 * (1.0 + lb) + bt)
        lo, hi = lf[:half], lf[half:]
        first, second = (hi, lo) if reverse else (lo, hi)
        away, t_first = _chunk_cumsum(first, not reverse, cs_s.at[slot, 0], cs_s.at[slot, 1])
        toward, t_second = _chunk_cumsum(second, reverse, cs_s.at[slot, 2], cs_s.at[slot, 3])
        a_first = first - away
        a = jnp.concatenate([toward, a_first] if reverse else [a_first, toward], axis=0)
        k_inv = (k * jnp.exp(-a)).astype(BF16)
        d_first = jnp.exp(t_first)
        d_second = jnp.exp(t_second)
        k_dec = k_inv * d_second[0:1, :].astype(BF16)
        return a, k_inv, k_dec, d_first, d_first * d_second

    def update_state(d, h, v, k_dec, decay):
        ds_t = lax.dot_general(v, k_dec, _TN, preferred_element_type=F32)
        st_s[d, h] = st_s[d, h] * decay + ds_t

    def cumsum_slot(u, d, h):
        return (u * 2 + d) * n_heads + h

    def prep(d, chunk, u):
        z_ref, _, lb, reverse, _, _ = dirs[d]
        sl = rows(chunk)
        for h in range(n_heads):
            a, k_inv, k_dec, d_first, decay = gate_decay(
                z_ref[sl, lanes(h)], lb[:, lanes(h)], reverse, cumsum_slot(u, d, h))
            qd_s[d, sl, lanes(h)] = q_ref[sl, lanes(h)] * jnp.exp(a).astype(BF16)
            ki_s[d, sl, lanes(h)] = k_inv
            kd_s[d, sl, lanes(h)] = k_dec
            dec_s[d, dec_rows(chunk), lanes(h)] = decay
            mid_s[d, dec_rows(chunk), lanes(h)] = d_first

    def state_rows(chunk):
        return pl.ds(pl.multiple_of(chunk * HG_DV, HG_DV), HG_DV)

    def local(d, chunk, _):
        mask = dirs[d][4]
        sl = rows(chunk)
        for h in range(n_heads):
            sc = lax.dot_general(qd_s[d, sl, lanes(h)], ki_s[d, sl, lanes(h)], _NT,
                                 preferred_element_type=F32)
            sc_s[d, h, sl, :] = jnp.where(mask, sc, 0.0).astype(BF16)
            ds_s[d, h, state_rows(chunk), :] = lax.dot_general(
                v_ref[sl, lanes(h)], kd_s[d, sl, lanes(h)], _TN, preferred_element_type=F32)

    def step(d, chunk, _):
        out_s = dirs[d][5]
        sl = rows(chunk)
        decay = dec_s[d, dec_rows(chunk), :]
        d_first = mid_s[d, dec_rows(chunk), :]
        for h in range(n_heads):
            st_mid = (st_s[d, h] * d_first[0:1, lanes(h)]).astype(BF16)
            inter = lax.dot_general(qd_s[d, sl, lanes(h)], st_mid, _NT,
                                    preferred_element_type=F32)
            out_s[sl, lanes(h)] = inter + jnp.dot(sc_s[d, h, sl, :], v_ref[sl, lanes(h)],
                                                  preferred_element_type=F32)
            st_s[d, h] = (st_s[d, h] * decay[0:1, lanes(h)]
                          + ds_s[d, h, state_rows(chunk), :])

    st_s[...] = jnp.zeros_like(st_s)
    for j in range(n_ctx):
        for d in range(2):
            _, cz_ref, lb, reverse, _, _ = dirs[d]
            sl = pl.ds((n_ctx - 1 - j if reverse else j) * c, c)
            for h in range(n_heads):
                _, _, k_dec, _, decay = gate_decay(cz_ref[sl, lanes(h)], lb[:, lanes(h)],
                                                   reverse, cumsum_slot(j % SCAN_UNROLL, d, h))
                update_state(d, h, cv_ref[sl, lanes(h)], k_dec, decay[0:1, :])

    def chunk_of(d, idx):
        return n_lat - 1 - idx if dirs[d][3] else idx

    def for_chunks(stage, first):
        for u in range(SCAN_UNROLL):
            idx = first + u
            idx = min(idx, n_lat - 1) if isinstance(idx, int) else jnp.minimum(idx, n_lat - 1)
            for d in range(2):
                stage(d, chunk_of(d, idx), u)

    for_chunks(prep, 0)
    for_chunks(prep, SCAN_UNROLL)
    for_chunks(local, 0)

    def body(it, carry):
        for_chunks(step, it * SCAN_UNROLL)
        for_chunks(local, (it + 1) * SCAN_UNROLL)
        for_chunks(prep, (it + 2) * SCAN_UNROLL)
        return carry

    lax.fori_loop(0, n_lat // SCAN_UNROLL, body, 0)

    og = og_ref[...]
    ones = jnp.ones((HG_DV, HG_DV), BF16)
    for h in range(n_heads):
        o = of_s[:, lanes(h)] + ob_s[:, lanes(h)]
        ss = jnp.dot((o * o).astype(BF16), ones, preferred_element_type=F32)
        o = o * lax.rsqrt(ss * (1.0 / HG_DV) + EPS)
        gh = g_ref[:, lanes(h)].astype(F32)
        o_ref[:, lanes(h)] = (o * og[:, lanes(h)] * (gh + gh * jnp.tanh(gh))).astype(o_ref.dtype)


def _hgrn2_scan(zf, zb, v, q, g, czf, czb, cv, lb_f, lb_b, onorm_g):
    nb, n, _ = zf.shape
    nc = czf.shape[1]
    w = SCAN_HEADS * HG_DK
    lat = pl.BlockSpec((None, n, w), lambda b, h: (b, 0, h))
    ctx = pl.BlockSpec((None, nc, w), lambda b, h: (b, 0, h))
    vec = pl.BlockSpec((1, w), lambda b, h: (0, h))
    return pl.pallas_call(
        _scan_kernel,
        grid=(nb, HG_HEADS // SCAN_HEADS),
        in_specs=[lat] * 5 + [ctx] * 3 + [vec] * 3,
        out_specs=lat,
        out_shape=jax.ShapeDtypeStruct((nb, n, HG_V), BF16),
        scratch_shapes=[pltpu.VMEM((2, n, w), BF16),
                        pltpu.VMEM((2, n, w), BF16),
                        pltpu.VMEM((2, n, w), BF16),
                        pltpu.VMEM((2, n // SCAN_CHUNK * SUBLANES, w), F32),
                        pltpu.VMEM((2, n // SCAN_CHUNK * SUBLANES, w), F32),
                        pltpu.VMEM((SCAN_UNROLL * 2 * SCAN_HEADS, 4, CHUNK, HG_DK), F32),
                        pltpu.VMEM((2, SCAN_HEADS, n, SCAN_CHUNK), BF16),
                        pltpu.VMEM((2, SCAN_HEADS, n // SCAN_CHUNK * HG_DV, HG_DK), F32),
                        pltpu.VMEM((2, SCAN_HEADS, HG_DV, HG_DK), F32),
                        pltpu.VMEM((n, w), F32),
                        pltpu.VMEM((n, w), F32)],
        compiler_params=pltpu.CompilerParams(
            dimension_semantics=("arbitrary", "arbitrary"), vmem_limit_bytes=VMEM_LIMIT),
        name="hgrn2_scan",
    )(zf, zb, v, q, g, czf, czb, cv, lb_f, lb_b, onorm_g)


def _window(n, w):
    t = np.arange(n)
    lo = np.maximum(t - w // 2, 0)
    hi = np.minimum(t + w // 2 - 1, n - 1)
    s = np.arange(n)
    inside = (s[None, :] >= lo[:, None]) & (s[None, :] <= hi[:, None])
    return inside, (hi - lo + 1)


def _pool_operators(n, on_grid):
    mats, icnts, taps = [], [], []
    for w in POOL_WINDOWS:
        if on_grid:
            in_c, cnt_c = _window(GRID_W, w)
            _, cnt_r = _window(n // GRID_W, w)
            m = np.kron(np.eye(POOL_SLAB // GRID_W, dtype=bool), in_c)
            cnt = (cnt_r[:, None] * cnt_c[None, :]).reshape(n)
            taps.append((-(w // 2), w // 2 - 1))
        else:
            assert n == POOL_SLAB
            m, cnt = _window(n, w)
            taps.append((0, 0))
        mats.append(m)
        icnts.append(1.0 / cnt.astype(np.float64))
    colop = jnp.asarray(np.stack(mats).astype(np.float32), dtype=BF16)
    icnt = jnp.asarray(np.stack(icnts).astype(np.float32)[..., None])
    return colop, icnt, tuple(taps)


def kernel(x, c, ctx, c_ctx, ada_w, ada_b, pre_g, post_g, ev_w_in, ev_pool_w, ev_pool_scale,
           ev_conv_w, ev_conv_b, ev_w_out, od_w_in, od_onorm_g, od_w_out, lb_logits):
    nb, n, d = x.shape
    nc = ctx.shape[1]
    lat_row = lambda b: b
    ctx_row = lambda b: CTX_ROW

    lb_table = jnp.cumsum(jax.nn.softmax(lb_logits.astype(F32), axis=1), axis=1)

    cc = jnp.zeros((MOD_ROWS, d), F32).at[:nb].set(c).at[CTX_ROW].set(c_ctx)
    mod = _ada_table(cc, ada_w, ada_b)
    mod = mod.reshape(2, MOD_ROWS, 3, 1, d)
    shift = [mod[l, :, 0] for l in range(2)]
    scale = [mod[l, :, 1] for l in range(2)]
    gate = [mod[l, :, 2] for l in range(2)]

    ctx_flat = ctx.reshape(1, nb * nc, d)

    pool_w = ev_pool_w[0].astype(BF16)
    pool_scale = ev_pool_scale[0].reshape(1, POOL_W)
    conv_w = ev_conv_w[0]
    conv_b = ev_conv_b[0].reshape(1, CONV_W)
    pre0 = pre_g[0].reshape(1, d)
    post0 = post_g[0].reshape(1, d)

    pre1 = pre_g[1].reshape(1, d)
    post1 = post_g[1].reshape(1, d)
    norm1 = (shift[1], scale[1], pre1)

    def even_mixer(tokens3, mod_row, on_grid):
        b3, n3, _ = tokens3.shape
        h = _norm_tokens(tokens3, mod_row, shift[0], scale[0], pre0, TM_NORM)
        proj, = _inproj(h.reshape(b3 * n3, d), ev_w_in[0], EVEN_IN, [BF16], TM_IN, 1024)
        n_tok = n if on_grid else nc
        proj = proj.reshape(nb, n_tok, EVEN_IN)
        colop, icnt, taps = _pool_operators(n_tok, on_grid)
        a_out = _pool_mixer(proj, colop, icnt, taps, pool_w, pool_scale)
        b_out = _conv_mixer(proj, conv_w, conv_b)
        acts = [a_out.reshape(b3, n3, POOL_W), b_out.reshape(b3, n3, CONV_W)]
        return _outproj(acts, ev_w_out[0], tokens3, mod_row, gate[0], post0, TM_OUT, norm1)

    x1, h1 = even_mixer(x, lat_row, True)
    _, hc1 = even_mixer(ctx_flat, ctx_row, False)

    tn1 = SCAN_HEADS * HG_DK
    zf, zb, v, q, g = [t.reshape(nb, n, HG_K) for t in
                       _inproj(h1.reshape(nb * n, d), od_w_in[0], HG_K,
                               [F32, F32, BF16, BF16, BF16], TM_IN, tn1,
                               (0.5, 0.5, 1.0, 1.0, 0.5))]
    czf, czb, cv = [t.reshape(nb, nc, HG_K) for t in
                    _inproj(hc1.reshape(nb * nc, d), od_w_in[0], HG_K,
                            [F32, F32, BF16], TM_IN, tn1, (0.5, 0.5, 1.0))]
    o = _hgrn2_scan(zf, zb, v, q, g, czf, czb, cv, lb_table[0, 1].reshape(1, HG_K),
                    lb_table[1, 1].reshape(1, HG_K), od_onorm_g[0].reshape(1, HG_V))
    return _outproj([o], od_w_out[0], x1, lat_row, gate[1], post1, TM_OUT)
```

```python
import functools

import numpy as np
import jax
import jax.numpy as jnp
from jax import lax
from jax.experimental import pallas as pl
from jax.experimental.pallas import tpu as pltpu

F32 = jnp.float32
BF16 = jnp.bfloat16

D_MODEL = 2048
BATCH = 4
SEQ = 2048
CTX_LEN = 256
GRID_W = 64
EPS = 1e-6

POOL_WINDOWS = (2, 4, 8, 16)
POOL_W = D_MODEL // 2
POOL_GROUP = POOL_W // len(POOL_WINDOWS)
CONV_W = D_MODEL // 2
EVEN_IN = 2 * POOL_W + 4 * CONV_W

HG_DK = 128
HG_HEADS = D_MODEL // HG_DK
HG_DV = D_MODEL // HG_HEADS
HG_K = HG_HEADS * HG_DK
HG_V = HG_HEADS * HG_DV
ODD_IN = 3 * HG_K + 2 * HG_V
CHUNK = 64

MOD_ROWS = 8
CTX_ROW = BATCH
VMEM_LIMIT = 56 * 1024 * 1024
TM_NORM = 512
TM_IN = 1024
TM_OUT = 512


def _silu(v):
    hv = 0.5 * v
    return hv + hv * jnp.tanh(hv)


def _ada_kernel(cc_ref, w_ref, b_ref, o_ref):
    s = _silu(cc_ref[...])
    o_ref[...] = jnp.dot(s, w_ref[...], preferred_element_type=F32) + b_ref[...]


def _ada_table(cc, ada_w, ada_b):
    depth = ada_w.shape[0]
    tn = 1024
    return pl.pallas_call(
        _ada_kernel,
        grid=(depth, 3 * D_MODEL // tn),
        in_specs=[
            pl.BlockSpec((MOD_ROWS, D_MODEL), lambda l, j: (0, 0)),
            pl.BlockSpec((None, D_MODEL, tn), lambda l, j: (l, 0, j)),
            pl.BlockSpec((None, 1, tn), lambda l, j: (l, 0, j)),
        ],
        out_specs=pl.BlockSpec((None, MOD_ROWS, tn), lambda l, j: (l, 0, j)),
        out_shape=jax.ShapeDtypeStruct((depth, MOD_ROWS, 3 * D_MODEL), F32),
        compiler_params=pltpu.CompilerParams(
            dimension_semantics=("arbitrary", "arbitrary"), vmem_limit_bytes=VMEM_LIMIT),
        name="ada_table",
    )(cc, ada_w, ada_b.reshape(depth, 1, 3 * D_MODEL))


def _norm_mod(x, g, scale, shift):
    ms = jnp.mean(x * x, axis=-1, keepdims=True)
    y = x * lax.rsqrt(ms + EPS) * g
    return y * (1.0 + scale) + shift


def _norm_kernel(x_ref, shift_ref, scale_ref, g_ref, o_ref):
    o_ref[...] = _norm_mod(x_ref[...], g_ref[...], scale_ref[...],
                           shift_ref[...]).astype(o_ref.dtype)


def _norm_tokens(x3, mod_row_of_batch, shift, scale, g, tm):
    nb, rows, _ = x3.shape
    mod_map = lambda b, i: (mod_row_of_batch(b), 0, 0)
    return pl.pallas_call(
        _norm_kernel,
        grid=(nb, rows // tm),
        in_specs=[
            pl.BlockSpec((None, tm, D_MODEL), lambda b, i: (b, i, 0)),
            pl.BlockSpec((None, 1, D_MODEL), mod_map),
            pl.BlockSpec((None, 1, D_MODEL), mod_map),
            pl.BlockSpec((1, D_MODEL), lambda b, i: (0, 0)),
        ],
        out_specs=pl.BlockSpec((None, tm, D_MODEL), lambda b, i: (b, i, 0)),
        out_shape=jax.ShapeDtypeStruct((nb, rows, D_MODEL), BF16),
        compiler_params=pltpu.CompilerParams(
            dimension_semantics=("arbitrary", "arbitrary"), vmem_limit_bytes=VMEM_LIMIT),
        name="norm_tokens",
    )(x3, shift, scale, g)


def _inproj_kernel(seg_scales, h_ref, *refs):
    n_seg = len(seg_scales)
    w_refs = refs[:n_seg]
    o_refs = refs[n_seg:2 * n_seg]
    wb_refs = refs[2 * n_seg:]

    @pl.when(pl.program_id(1) == 0)
    def _():
        for w_ref, wb_ref in zip(w_refs, wb_refs):
            wb_ref[...] = w_ref[...].astype(BF16)

    for wb_ref, o_ref, scale in zip(wb_refs, o_refs, seg_scales):
        acc = jnp.dot(h_ref[...], wb_ref[...], preferred_element_type=F32)
        o_ref[...] = (acc if scale == 1.0 else acc * scale).astype(o_ref.dtype)


def _inproj(h, w, seg_width, seg_dtypes, tm, tn, seg_scales=None, tile_major=False):
    rows = h.shape[0]
    n_seg = len(seg_dtypes)
    seg_scales = tuple(seg_scales or (1.0,) * n_seg)
    n_tiles = seg_width // tn
    w_specs = [pl.BlockSpec((D_MODEL, tn), functools.partial(
        lambda k, j, i: (0, k * n_tiles + j), k)) for k in range(n_seg)]
    if tile_major:
        out_spec = pl.BlockSpec((None, tm, tn), lambda j, i: (j, i, 0))
        out_dims = (n_tiles, rows, tn)
    else:
        out_spec = pl.BlockSpec((tm, tn), lambda j, i: (i, j))
        out_dims = (rows, seg_width)
    return pl.pallas_call(
        functools.partial(_inproj_kernel, seg_scales),
        grid=(n_tiles, rows // tm),
        in_specs=[pl.BlockSpec((tm, D_MODEL), lambda j, i: (i, 0))] + w_specs,
        out_specs=[out_spec] * n_seg,
        out_shape=[jax.ShapeDtypeStruct(out_dims, dt) for dt in seg_dtypes],
        scratch_shapes=[pltpu.VMEM((D_MODEL, tn), BF16)] * n_seg,
        compiler_params=pltpu.CompilerParams(
            dimension_semantics=("arbitrary", "arbitrary"), vmem_limit_bytes=VMEM_LIMIT),
        name="inproj",
    )(h, *([w] * n_seg))


POOL_SLAB = 256
POOL_ROWS = 128


def _pool_kernel(taps, pad, colop_ref, icnt_ref, v_ref, gate_ref, w_ref, sc_ref, o_ref,
                 box_s, pooled_s):
    n = v_ref.shape[0]
    if pad:
        box_s[0:pad, :] = jnp.zeros((pad, POOL_GROUP), F32)
        box_s[pad + n:pad + n + pad, :] = jnp.zeros((pad, POOL_GROUP), F32)
    for gi, (lo, hi) in enumerate(taps):
        lanes = slice(gi * POOL_GROUP, (gi + 1) * POOL_GROUP)
        for s in range(n // POOL_SLAB):
            tok = slice(s * POOL_SLAB, (s + 1) * POOL_SLAB)
            box_s[pad + s * POOL_SLAB:pad + (s + 1) * POOL_SLAB, :] = jnp.dot(
                colop_ref[gi], v_ref[tok, lanes], preferred_element_type=F32)

        def block(i, carry, gi=gi, lo=lo, hi=hi, lanes=lanes):
            r0 = pl.multiple_of(i * POOL_ROWS, POOL_ROWS)
            tok = pl.ds(r0, POOL_ROWS)
            acc = box_s[pl.ds(pad + lo * GRID_W + r0, POOL_ROWS), :]
            for dlt in range(lo + 1, hi + 1):
                acc = acc + box_s[pl.ds(pad + dlt * GRID_W + r0, POOL_ROWS), :]
            pooled = acc * icnt_ref[gi, tok, :] - v_ref[tok, lanes].astype(F32)
            pooled_s[tok, :] = pooled.astype(BF16)
            return carry

        lax.fori_loop(0, n // POOL_ROWS, block, 0, unroll=2)
        mixed = jnp.dot(pooled_s[...], w_ref[gi], preferred_element_type=F32)
        o_ref[:, lanes] = (mixed * sc_ref[:, lanes]
                           * _silu(gate_ref[:, lanes].astype(F32))).astype(o_ref.dtype)


def _pool_mixer(proj, colop, icnt, taps, pool_w, pool_scale):
    nb, n, _ = proj.shape
    ng = len(POOL_WINDOWS)
    pad = max(max(-lo, hi) for lo, hi in taps) * GRID_W
    pad = -(-pad // POOL_ROWS) * POOL_ROWS
    return pl.pallas_call(
        functools.partial(_pool_kernel, taps, pad),
        grid=(nb,),
        in_specs=[
            pl.BlockSpec((ng, POOL_SLAB, POOL_SLAB), lambda b: (0, 0, 0)),
            pl.BlockSpec((ng, n, 1), lambda b: (0, 0, 0)),
            pl.BlockSpec((None, n, POOL_W), lambda b: (b, 0, 0)),
            pl.BlockSpec((None, n, POOL_W), lambda b: (b, 0, 1)),
            pl.BlockSpec((ng, POOL_GROUP, POOL_GROUP), lambda b: (0, 0, 0)),
            pl.BlockSpec((1, POOL_W), lambda b: (0, 0)),
        ],
        out_specs=pl.BlockSpec((None, n, POOL_W), lambda b: (b, 0, 0)),
        out_shape=jax.ShapeDtypeStruct((nb, n, POOL_W), BF16),
        scratch_shapes=[pltpu.VMEM((n + 2 * pad, POOL_GROUP), F32),
                        pltpu.VMEM((n, POOL_GROUP), BF16)],
        compiler_params=pltpu.CompilerParams(
            dimension_semantics=("arbitrary",), vmem_limit_bytes=VMEM_LIMIT),
        name="pool_mixer",
    )(colop, icnt, proj, proj, pool_w, pool_scale)


def _conv_kernel(x_ref, b_ref, c_ref, gate_ref, cw_ref, cb_ref, o_ref):
    n = x_ref.shape[0]
    u = c_ref[...].astype(F32) * x_ref[...].astype(F32)
    row = lax.broadcasted_iota(jnp.int32, u.shape, 0)
    u_prev = jnp.where(row == 0, 0.0, pltpu.roll(u, 1, axis=0))
    u_next = jnp.where(row == n - 1, 0.0, pltpu.roll(u, n - 1, axis=0))
    cw = cw_ref[...]
    conv = u_prev * cw[0:1, :] + u * cw[1:2, :] + u_next * cw[2:3, :] + cb_ref[...]
    o_ref[...] = (b_ref[...].astype(F32) * conv
                  * _silu(gate_ref[...].astype(F32))).astype(o_ref.dtype)


def _conv_mixer(proj, conv_w, conv_b):
    nb, n, _ = proj.shape
    tc = 256
    nblk = CONV_W // tc
    base = 2 * POOL_W // tc

    def seg(k):
        return pl.BlockSpec((None, n, tc), lambda b, j: (b, 0, base + k * nblk + j))

    return pl.pallas_call(
        _conv_kernel,
        grid=(nb, nblk),
        in_specs=[
            seg(0), seg(1), seg(2), seg(3),
            pl.BlockSpec((3, tc), lambda b, j: (0, j)),
            pl.BlockSpec((1, tc), lambda b, j: (0, j)),
        ],
        out_specs=pl.BlockSpec((None, n, tc), lambda b, j: (b, 0, j)),
        out_shape=jax.ShapeDtypeStruct((nb, n, CONV_W), BF16),
        compiler_params=pltpu.CompilerParams(
            dimension_semantics=("arbitrary", "arbitrary"), vmem_limit_bytes=VMEM_LIMIT),
        name="conv_mixer",
    )(proj, proj, proj, proj, conv_w, conv_b)


def _outproj_kernel(n_in, emit_next, *refs):
    a_refs = refs[:n_in]
    w_refs = refs[n_in:2 * n_in]
    x_ref, gate_ref, pg_ref = refs[2 * n_in:2 * n_in + 3]
    rest = refs[2 * n_in + 3:]
    if emit_next:
        shift_ref, scale_ref, g_ref, o_ref, h_ref = rest[:5]
        wb_refs = rest[5:]
    else:
        o_ref = rest[0]
        wb_refs = rest[1:]

    @pl.when((pl.program_id(0) == 0) & (pl.program_id(1) == 0))
    def _():
        for w_ref, wb_ref in zip(w_refs, wb_refs):
            wb_ref[...] = w_ref[...].astype(BF16)

    y = jnp.dot(a_refs[0][...], wb_refs[0][...], preferred_element_type=F32)
    for a_ref, wb_ref in zip(a_refs[1:], wb_refs[1:]):
        y = y + jnp.dot(a_ref[...], wb_ref[...], preferred_element_type=F32)
    ms = jnp.mean(y * y, axis=-1, keepdims=True)
    r = y * lax.rsqrt(ms + EPS) * pg_ref[...]
    x_new = x_ref[...] + gate_ref[...] * r
    o_ref[...] = x_new
    if emit_next:
        h_ref[...] = _norm_mod(x_new, g_ref[...], scale_ref[...],
                               shift_ref[...]).astype(h_ref.dtype)


def _outproj(acts, w, x3, mod_row_of_batch, gate, pg, tm, next_norm=None):
    nb, rows, _ = x3.shape
    n_in = len(acts)
    k_in = acts[0].shape[-1]
    assert all(a.shape[-1] == k_in for a in acts) and n_in * k_in == w.shape[0]
    emit_next = next_norm is not None
    mod_spec = pl.BlockSpec((None, 1, D_MODEL), lambda b, i: (mod_row_of_batch(b), 0, 0))
    vec_spec = pl.BlockSpec((1, D_MODEL), lambda b, i: (0, 0))
    tok_spec = pl.BlockSpec((None, tm, D_MODEL), lambda b, i: (b, i, 0))
    in_specs = [pl.BlockSpec((None, tm, k_in), lambda b, i: (b, i, 0))] * n_in
    in_specs += [pl.BlockSpec((k_in, D_MODEL), functools.partial(lambda k, b, i: (k, 0), k),
                              pipeline_mode=pl.Buffered(1)) for k in range(n_in)]
    in_specs += [tok_spec, mod_spec, vec_spec]
    operands = [*acts, *([w] * n_in), x3, gate, pg]
    out_specs, out_shape = tok_spec, jax.ShapeDtypeStruct((nb, rows, D_MODEL), F32)
    if emit_next:
        in_specs += [mod_spec, mod_spec, vec_spec]
        operands += list(next_norm)
        out_specs = [tok_spec, tok_spec]
        out_shape = [out_shape, jax.ShapeDtypeStruct((nb, rows, D_MODEL), BF16)]
    return pl.pallas_call(
        functools.partial(_outproj_kernel, n_in, emit_next),
        grid=(nb, rows // tm),
        in_specs=in_specs,
        out_specs=out_specs,
        out_shape=out_shape,
        scratch_shapes=[pltpu.VMEM((k_in, D_MODEL), BF16)] * n_in,
        compiler_params=pltpu.CompilerParams(
            dimension_semantics=("arbitrary", "arbitrary"), vmem_limit_bytes=VMEM_LIMIT),
        name="outproj",
    )(*operands)


_NT = (((1,), (1,)), ((), ()))
_TN = (((0,), (0,)), ((), ()))
SUBLANES = 8
SCAN_HEADS = 2
SCAN_UNROLL = 1
SCAN_CHUNK = 2 * CHUNK


def _chunk_cumsum(x, reverse, in_ref, out_ref):
    c, w = x.shape
    nblk = c // SUBLANES
    order = range(nblk - 1, -1, -1) if reverse else range(nblk)
    in_ref[...] = x
    pref = [None] * nblk
    acc = None
    for j in order:
        blk = in_ref[pl.ds(j, SUBLANES, stride=nblk), :]
        acc = blk if acc is None else acc + blk
        pref[j] = acc
    total = acc
    sub = lax.broadcasted_iota(jnp.int32, (SUBLANES, w), 0)
    incl = total
    for s in (1, 2, 4):
        if reverse:
            incl = incl + jnp.where(sub < SUBLANES - s,
                                    pltpu.roll(incl, SUBLANES - s, axis=0), 0.0)
        else:
            incl = incl + jnp.where(sub >= s, pltpu.roll(incl, s, axis=0), 0.0)
    before = incl - total
    for j in range(nblk):
        out_ref[pl.ds(j, SUBLANES, stride=nblk), :] = pref[j] + before
    edge = incl[0:1, :] if reverse else incl[SUBLANES - 1:SUBLANES, :]
    return out_ref[...], jnp.broadcast_to(edge, (SUBLANES, w))


def _scan_kernel(zf_ref, zb_ref, v_ref, q_ref, g_ref, czf_ref, czb_ref, cv_ref,
                 lbf_ref, lbb_ref, og_ref, o_ref,
                 qd_s, ki_s, kd_s, dec_s, mid_s, cs_s, sc_s, ds_s, st_s, of_s, ob_s):
    c = SCAN_CHUNK
    half = CHUNK
    n_lat = zf_ref.shape[0] // c
    n_ctx = czf_ref.shape[0] // c
    n_heads = zf_ref.shape[1] // HG_DK
    row = lax.broadcasted_iota(jnp.int32, (c, c), 0)
    col = lax.broadcasted_iota(jnp.int32, (c, c), 1)
    dirs = ((zf_ref, czf_ref, lbf_ref[...], False, col <= row, of_s),
            (zb_ref, czb_ref, lbb_ref[...], True, col >= row, ob_s))

    def lanes(h):
        return slice(h * HG_DK, (h + 1) * HG_DK)

    def rows(chunk):
        return pl.ds(pl.multiple_of(chunk * c, c), c)

    def dec_rows(chunk):
        return pl.ds(pl.multiple_of(chunk * SUBLANES, SUBLANES), SUBLANES)

    def gate_decay(zh, lb, reverse, slot):
        bt = (0.5 * (1.0 - lb)) * jnp.tanh(zh)
        k = 0.5 * (1.0 - lb) - bt
        lf = jnp.log(0.5 * (1.0 + lb) + bt)
        lo, hi = lf[:half], lf[half:]
        first, second = (hi, lo) if reverse else (lo, hi)
        away, t_first = _chunk_cumsum(first, not reverse, cs_s.at[slot, 0], cs_s.at[slot, 1])
        toward, t_second = _chunk_cumsum(second, reverse, cs_s.at[slot, 2], cs_s.at[slot, 3])
        a_first = first - away
        a = jnp.concatenate([toward, a_first] if reverse else [a_first, toward], axis=0)
        k_inv = (k * jnp.exp(-a)).astype(BF16)
        d_first = jnp.exp(t_first)
        d_second = jnp.exp(t_second)
        k_dec = k_inv * d_second[0:1, :].astype(BF16)
        return a, k_inv, k_dec, d_first, d_first * d_second

    def update_state(d, h, v, k_dec, decay):
        ds_t = lax.dot_general(v, k_dec, _TN, preferred_element_type=F32)
        st_s[d, h] = st_s[d, h] * decay + ds_t

    def cumsum_slot(u, d, h):
        return (u * 2 + d) * n_heads + h

    def prep(d, chunk, u):
        z_ref, _, lb, reverse, _, _ = dirs[d]
        sl = rows(chunk)
        for h in range(n_heads):
            a, k_inv, k_dec, d_first, decay = gate_decay(
                z_ref[sl, lanes(h)], lb[:, lanes(h)], reverse, cumsum_slot(u, d, h))
            qd_s[d, sl, lanes(h)] = q_ref[sl, lanes(h)] * jnp.exp(a).astype(BF16)
            ki_s[d, sl, lanes(h)] = k_inv
            kd_s[d, sl, lanes(h)] = k_dec
            dec_s[d, dec_rows(chunk), lanes(h)] = decay
            mid_s[d, dec_rows(chunk), lanes(h)] = d_first

    def state_rows(chunk):
        return pl.ds(pl.multiple_of(chunk * HG_DV, HG_DV), HG_DV)

    def local(d, chunk, _):
        mask = dirs[d][4]
        sl = rows(chunk)
        for h in range(n_heads):
            sc = lax.dot_general(qd_s[d, sl, lanes(h)], ki_s[d, sl, lanes(h)], _NT,
                                 preferred_element_type=F32)
            sc_s[d, h, sl, :] = jnp.where(mask, sc, 0.0).astype(BF16)
            ds_s[d, h, state_rows(chunk), :] = lax.dot_general(
                v_ref[sl, lanes(h)], kd_s[d, sl, lanes(h)], _TN, preferred_element_type=F32)

    def step(d, chunk, _):
        out_s = dirs[d][5]
        sl = rows(chunk)
        decay = dec_s[d, dec_rows(chunk), :]
        d_first = mid_s[d, dec_rows(chunk), :]
        for h in range(n_heads):
            st_mid = (st_s[d, h] * d_first[0:1, lanes(h)]).astype(BF16)
            inter = lax.dot_general(qd_s[d, sl, lanes(h)], st_mid, _NT,
                                    preferred_element_type=F32)
            out_s[sl, lanes(h)] = inter + jnp.dot(sc_s[d, h, sl, :], v_ref[sl, lanes(h)],
                                                  preferred_element_type=F32)
            st_s[d, h] = (st_s[d, h] * decay[0:1, lanes(h)]
                          + ds_s[d, h, state_rows(chunk), :])

    st_s[...] = jnp.zeros_like(st_s)
    for j in range(n_ctx):
        for d in range(2):
            _, cz_ref, lb, reverse, _, _ = dirs[d]
            sl = pl.ds((n_ctx - 1 - j if reverse else j) * c, c)
            for h in range(n_heads):
                _, _, k_dec, _, decay = gate_decay(cz_ref[sl, lanes(h)], lb[:, lanes(h)],
                                                   reverse, cumsum_slot(j % SCAN_UNROLL, d, h))
                update_state(d, h, cv_ref[sl, lanes(h)], k_dec, decay[0:1, :])

    def chunk_of(d, idx):
        return n_lat - 1 - idx if dirs[d][3] else idx

    def for_chunks(stage, first):
        for u in range(SCAN_UNROLL):
            idx = first + u
            idx = min(idx, n_lat - 1) if isinstance(idx, int) else jnp.minimum(idx, n_lat - 1)
            for d in range(2):
                stage(d, chunk_of(d, idx), u)

    for_chunks(prep, 0)
    for_chunks(prep, SCAN_UNROLL)
    for_chunks(local, 0)

    def body(it, carry):
        for_chunks(step, it * SCAN_UNROLL)
        for_chunks(local, (it + 1) * SCAN_UNROLL)
        for_chunks(prep, (it + 2) * SCAN_UNROLL)
        return carry

    lax.fori_loop(0, n_lat // SCAN_UNROLL, body, 0)

    og = og_ref[...]
    ones = jnp.ones((HG_DV, HG_DV), BF16)
    for h in range(n_heads):
        o = of_s[:, lanes(h)] + ob_s[:, lanes(h)]
        ss = jnp.dot((o * o).astype(BF16), ones, preferred_element_type=F32)
        o = o * lax.rsqrt(ss * (1.0 / HG_DV) + EPS)
        gh = g_ref[:, lanes(h)].astype(F32)
        o_ref[:, lanes(h)] = (o * og[:, lanes(h)] * (gh + gh * jnp.tanh(gh))).astype(o_ref.dtype)


def _hgrn2_scan(zf, zb, v, q, g, czf, czb, cv, lb_f, lb_b, onorm_g):
    _, nb, n, w = zf.shape
    nc = czf.shape[2]
    lat = pl.BlockSpec((None, None, n, w), lambda b, h: (h, b, 0, 0))
    ctx = pl.BlockSpec((None, None, nc, w), lambda b, h: (h, b, 0, 0))
    vec = pl.BlockSpec((1, w), lambda b, h: (0, h))
    return pl.pallas_call(
        _scan_kernel,
        grid=(nb, HG_HEADS // SCAN_HEADS),
        in_specs=[lat] * 5 + [ctx] * 3 + [vec] * 3,
        out_specs=pl.BlockSpec((None, n, w), lambda b, h: (b, 0, h)),
        out_shape=jax.ShapeDtypeStruct((nb, n, HG_V), BF16),
        scratch_shapes=[pltpu.VMEM((2, n, w), BF16),
                        pltpu.VMEM((2, n, w), BF16),
                        pltpu.VMEM((2, n, w), BF16),
                        pltpu.VMEM((2, n // SCAN_CHUNK * SUBLANES, w), F32),
                        pltpu.VMEM((2, n // SCAN_CHUNK * SUBLANES, w), F32),
                        pltpu.VMEM((SCAN_UNROLL * 2 * SCAN_HEADS, 4, CHUNK, HG_DK), F32),
                        pltpu.VMEM((2, SCAN_HEADS, n, SCAN_CHUNK), BF16),
                        pltpu.VMEM((2, SCAN_HEADS, n // SCAN_CHUNK * HG_DV, HG_DK), F32),
                        pltpu.VMEM((2, SCAN_HEADS, HG_DV, HG_DK), F32),
                        pltpu.VMEM((n, w), F32),
                        pltpu.VMEM((n, w), F32)],
        compiler_params=pltpu.CompilerParams(
            dimension_semantics=("arbitrary", "arbitrary"), vmem_limit_bytes=VMEM_LIMIT),
        name="hgrn2_scan",
    )(zf, zb, v, q, g, czf, czb, cv, lb_f, lb_b, onorm_g)


def _window(n, w):
    t = np.arange(n)
    lo = np.maximum(t - w // 2, 0)
    hi = np.minimum(t + w // 2 - 1, n - 1)
    s = np.arange(n)
    inside = (s[None, :] >= lo[:, None]) & (s[None, :] <= hi[:, None])
    return inside, (hi - lo + 1)


def _pool_operators(n, on_grid):
    mats, icnts, taps = [], [], []
    for w in POOL_WINDOWS:
        if on_grid:
            in_c, cnt_c = _window(GRID_W, w)
            _, cnt_r = _window(n // GRID_W, w)
            m = np.kron(np.eye(POOL_SLAB // GRID_W, dtype=bool), in_c)
            cnt = (cnt_r[:, None] * cnt_c[None, :]).reshape(n)
            taps.append((-(w // 2), w // 2 - 1))
        else:
            assert n == POOL_SLAB
            m, cnt = _window(n, w)
            taps.append((0, 0))
        mats.append(m)
        icnts.append(1.0 / cnt.astype(np.float64))
    colop = jnp.asarray(np.stack(mats).astype(np.float32), dtype=BF16)
    icnt = jnp.asarray(np.stack(icnts).astype(np.float32)[..., None])
    return colop, icnt, tuple(taps)


def kernel(x, c, ctx, c_ctx, ada_w, ada_b, pre_g, post_g, ev_w_in, ev_pool_w, ev_pool_scale,
           ev_conv_w, ev_conv_b, ev_w_out, od_w_in, od_onorm_g, od_w_out, lb_logits):
    nb, n, d = x.shape
    nc = ctx.shape[1]
    lat_row = lambda b: b
    ctx_row = lambda b: CTX_ROW

    lb_table = jnp.cumsum(jax.nn.softmax(lb_logits.astype(F32), axis=1), axis=1)

    cc = jnp.zeros((MOD_ROWS, d), F32).at[:nb].set(c).at[CTX_ROW].set(c_ctx)
    mod = _ada_table(cc, ada_w, ada_b)
    mod = mod.reshape(2, MOD_ROWS, 3, 1, d)
    shift = [mod[l, :, 0] for l in range(2)]
    scale = [mod[l, :, 1] for l in range(2)]
    gate = [mod[l, :, 2] for l in range(2)]

    ctx_flat = ctx.reshape(1, nb * nc, d)

    pool_w = ev_pool_w[0].astype(BF16)
    pool_scale = ev_pool_scale[0].reshape(1, POOL_W)
    conv_w = ev_conv_w[0]
    conv_b = ev_conv_b[0].reshape(1, CONV_W)
    pre0 = pre_g[0].reshape(1, d)
    post0 = post_g[0].reshape(1, d)

    pre1 = pre_g[1].reshape(1, d)
    post1 = post_g[1].reshape(1, d)
    norm1 = (shift[1], scale[1], pre1)

    def even_mixer(tokens3, mod_row, on_grid):
        b3, n3, _ = tokens3.shape
        h = _norm_tokens(tokens3, mod_row, shift[0], scale[0], pre0, TM_NORM)
        proj, = _inproj(h.reshape(b3 * n3, d), ev_w_in[0], EVEN_IN, [BF16], TM_IN, 1024)
        n_tok = n if on_grid else nc
        proj = proj.reshape(nb, n_tok, EVEN_IN)
        colop, icnt, taps = _pool_operators(n_tok, on_grid)
        a_out = _pool_mixer(proj, colop, icnt, taps, pool_w, pool_scale)
        b_out = _conv_mixer(proj, conv_w, conv_b)
        acts = [a_out.reshape(b3, n3, POOL_W), b_out.reshape(b3, n3, CONV_W)]
        return _outproj(acts, ev_w_out[0], tokens3, mod_row, gate[0], post0, TM_OUT, norm1)

    x1, h1 = even_mixer(x, lat_row, True)
    _, hc1 = even_mixer(ctx_flat, ctx_row, False)

    tn1 = SCAN_HEADS * HG_DK
    groups = HG_HEADS // SCAN_HEADS
    zf, zb, v, q, g = [t.reshape(groups, nb, n, tn1) for t in
                       _inproj(h1.reshape(nb * n, d), od_w_in[0], HG_K,
                               [F32, F32, BF16, BF16, BF16], TM_IN, tn1,
                               (0.5, 0.5, 1.0, 1.0, 0.5), tile_major=True)]
    czf, czb, cv = [t.reshape(groups, nb, nc, tn1) for t in
                    _inproj(hc1.reshape(nb * nc, d), od_w_in[0], HG_K,
                            [F32, F32, BF16], TM_IN, tn1, (0.5, 0.5, 1.0), tile_major=True)]
    o = _hgrn2_scan(zf, zb, v, q, g, czf, czb, cv, lb_table[0, 1].reshape(1, HG_K),
                    lb_table[1, 1].reshape(1, HG_K), od_onorm_g[0].reshape(1, HG_V))
    return _outproj([o], od_w_out[0], x1, lat_row, gate[1], post1, TM_OUT)
```

```python
import functools

import numpy as np
import jax
import jax.numpy as jnp
from jax import lax
from jax.experimental import pallas as pl
from jax.experimental.pallas import tpu as pltpu

F32 = jnp.float32
BF16 = jnp.bfloat16

D_MODEL = 2048
BATCH = 4
SEQ = 2048
CTX_LEN = 256
GRID_W = 64
EPS = 1e-6

POOL_WINDOWS = (2, 4, 8, 16)
POOL_W = D_MODEL // 2
POOL_GROUP = POOL_W // len(POOL_WINDOWS)
CONV_W = D_MODEL // 2
EVEN_IN = 2 * POOL_W + 4 * CONV_W

HG_DK = 128
HG_HEADS = D_MODEL // HG_DK
HG_DV = D_MODEL // HG_HEADS
HG_K = HG_HEADS * HG_DK
HG_V = HG_HEADS * HG_DV
ODD_IN = 3 * HG_K + 2 * HG_V
CHUNK = 64

MOD_ROWS = 8
CTX_ROW = BATCH
VMEM_LIMIT = 56 * 1024 * 1024
TM_NORM = 512
TM_IN = 1024
TM_OUT = 512


def _silu(v):
    hv = 0.5 * v
    return hv + hv * jnp.tanh(hv)


def _ada_kernel(cc_ref, w_ref, b_ref, o_ref):
    s = _silu(cc_ref[...])
    o_ref[...] = jnp.dot(s, w_ref[...], preferred_element_type=F32) + b_ref[...]


def _ada_table(cc, ada_w, ada_b):
    depth = ada_w.shape[0]
    tn = 1024
    return pl.pallas_call(
        _ada_kernel,
        grid=(depth, 3 * D_MODEL // tn),
        in_specs=[
            pl.BlockSpec((MOD_ROWS, D_MODEL), lambda l, j: (0, 0)),
            pl.BlockSpec((None, D_MODEL, tn), lambda l, j: (l, 0, j)),
            pl.BlockSpec((None, 1, tn), lambda l, j: (l, 0, j)),
        ],
        out_specs=pl.BlockSpec((None, MOD_ROWS, tn), lambda l, j: (l, 0, j)),
        out_shape=jax.ShapeDtypeStruct((depth, MOD_ROWS, 3 * D_MODEL), F32),
        compiler_params=pltpu.CompilerParams(
            dimension_semantics=("arbitrary", "arbitrary"), vmem_limit_bytes=VMEM_LIMIT),
        name="ada_table",
    )(cc, ada_w, ada_b.reshape(depth, 1, 3 * D_MODEL))


def _norm_mod(x, g, scale, shift):
    ms = jnp.mean(x * x, axis=-1, keepdims=True)
    y = x * lax.rsqrt(ms + EPS) * g
    return y * (1.0 + scale) + shift


def _norm_kernel(x_ref, shift_ref, scale_ref, g_ref, o_ref):
    o_ref[...] = _norm_mod(x_ref[...], g_ref[...], scale_ref[...],
                           shift_ref[...]).astype(o_ref.dtype)


def _norm_tokens(x3, mod_row_of_batch, shift, scale, g, tm):
    nb, rows, _ = x3.shape
    mod_map = lambda b, i: (mod_row_of_batch(b), 0, 0)
    return pl.pallas_call(
        _norm_kernel,
        grid=(nb, rows // tm),
        in_specs=[
            pl.BlockSpec((None, tm, D_MODEL), lambda b, i: (b, i, 0)),
            pl.BlockSpec((None, 1, D_MODEL), mod_map),
            pl.BlockSpec((None, 1, D_MODEL), mod_map),
            pl.BlockSpec((1, D_MODEL), lambda b, i: (0, 0)),
        ],
        out_specs=pl.BlockSpec((None, tm, D_MODEL), lambda b, i: (b, i, 0)),
        out_shape=jax.ShapeDtypeStruct((nb, rows, D_MODEL), BF16),
        compiler_params=pltpu.CompilerParams(
            dimension_semantics=("arbitrary", "arbitrary"), vmem_limit_bytes=VMEM_LIMIT),
        name="norm_tokens",
    )(x3, shift, scale, g)


def _inproj_kernel(seg_scales, h_ref, *refs):
    n_seg = len(seg_scales)
    w_refs = refs[:n_seg]
    o_refs = refs[n_seg:2 * n_seg]
    wb_refs = refs[2 * n_seg:]

    @pl.when(pl.program_id(1) == 0)
    def _():
        for w_ref, wb_ref in zip(w_refs, wb_refs):
            wb_ref[...] = w_ref[...].astype(BF16)

    for wb_ref, o_ref, scale in zip(wb_refs, o_refs, seg_scales):
        acc = jnp.dot(h_ref[...], wb_ref[...], preferred_element_type=F32)
        o_ref[...] = (acc if scale == 1.0 else acc * scale).astype(o_ref.dtype)


def _inproj(h, w, seg_width, seg_dtypes, tm, tn, seg_scales=None, tile_major=False):
    rows = h.shape[0]
    n_seg = len(seg_dtypes)
    seg_scales = tuple(seg_scales or (1.0,) * n_seg)
    n_tiles = seg_width // tn
    w_specs = [pl.BlockSpec((D_MODEL, tn), functools.partial(
        lambda k, j, i: (0, k * n_tiles + j), k)) for k in range(n_seg)]
    if tile_major:
        out_spec = pl.BlockSpec((None, tm, tn), lambda j, i: (j, i, 0))
        out_dims = (n_tiles, rows, tn)
    else:
        out_spec = pl.BlockSpec((tm, tn), lambda j, i: (i, j))
        out_dims = (rows, seg_width)
    return pl.pallas_call(
        functools.partial(_inproj_kernel, seg_scales),
        grid=(n_tiles, rows // tm),
        in_specs=[pl.BlockSpec((tm, D_MODEL), lambda j, i: (i, 0))] + w_specs,
        out_specs=[out_spec] * n_seg,
        out_shape=[jax.ShapeDtypeStruct(out_dims, dt) for dt in seg_dtypes],
        scratch_shapes=[pltpu.VMEM((D_MODEL, tn), BF16)] * n_seg,
        compiler_params=pltpu.CompilerParams(
            dimension_semantics=("arbitrary", "arbitrary"), vmem_limit_bytes=VMEM_LIMIT),
        name="inproj",
    )(h, *([w] * n_seg))


POOL_SLAB = 256
POOL_ROWS = 128


def _pool_kernel(taps, pad, colop_ref, icnt_ref, v_ref, gate_ref, w_ref, sc_ref, o_ref,
                 box_s, pooled_s):
    n = v_ref.shape[0]
    if pad:
        box_s[0:pad, :] = jnp.zeros((pad, POOL_GROUP), F32)
        box_s[pad + n:pad + n + pad, :] = jnp.zeros((pad, POOL_GROUP), F32)
    for gi, (lo, hi) in enumerate(taps):
        lanes = slice(gi * POOL_GROUP, (gi + 1) * POOL_GROUP)
        for s in range(n // POOL_SLAB):
            tok = slice(s * POOL_SLAB, (s + 1) * POOL_SLAB)
            box_s[pad + s * POOL_SLAB:pad + (s + 1) * POOL_SLAB, :] = jnp.dot(
                colop_ref[gi], v_ref[tok, lanes], preferred_element_type=F32)

        def block(i, carry, gi=gi, lo=lo, hi=hi, lanes=lanes):
            r0 = pl.multiple_of(i * POOL_ROWS, POOL_ROWS)
            tok = pl.ds(r0, POOL_ROWS)
            acc = box_s[pl.ds(pad + lo * GRID_W + r0, POOL_ROWS), :]
            for dlt in range(lo + 1, hi + 1):
                acc = acc + box_s[pl.ds(pad + dlt * GRID_W + r0, POOL_ROWS), :]
            pooled = acc * icnt_ref[gi, tok, :] - v_ref[tok, lanes].astype(F32)
            pooled_s[tok, :] = pooled.astype(BF16)
            return carry

        lax.fori_loop(0, n // POOL_ROWS, block, 0, unroll=2)
        mixed = jnp.dot(pooled_s[...], w_ref[gi], preferred_element_type=F32)
        o_ref[:, lanes] = (mixed * sc_ref[:, lanes]
                           * _silu(gate_ref[:, lanes].astype(F32))).astype(o_ref.dtype)


def _pool_mixer(proj, colop, icnt, taps, pool_w, pool_scale):
    nb, n, _ = proj.shape
    ng = len(POOL_WINDOWS)
    pad = max(max(-lo, hi) for lo, hi in taps) * GRID_W
    pad = -(-pad // POOL_ROWS) * POOL_ROWS
    return pl.pallas_call(
        functools.partial(_pool_kernel, taps, pad),
        grid=(nb,),
        in_specs=[
            pl.BlockSpec((ng, POOL_SLAB, POOL_SLAB), lambda b: (0, 0, 0)),
            pl.BlockSpec((ng, n, 1), lambda b: (0, 0, 0)),
            pl.BlockSpec((None, n, POOL_W), lambda b: (b, 0, 0)),
            pl.BlockSpec((None, n, POOL_W), lambda b: (b, 0, 1)),
            pl.BlockSpec((ng, POOL_GROUP, POOL_GROUP), lambda b: (0, 0, 0)),
            pl.BlockSpec((1, POOL_W), lambda b: (0, 0)),
        ],
        out_specs=pl.BlockSpec((None, n, POOL_W), lambda b: (b, 0, 0)),
        out_shape=jax.ShapeDtypeStruct((nb, n, POOL_W), BF16),
        scratch_shapes=[pltpu.VMEM((n + 2 * pad, POOL_GROUP), F32),
                        pltpu.VMEM((n, POOL_GROUP), BF16)],
        compiler_params=pltpu.CompilerParams(
            dimension_semantics=("arbitrary",), vmem_limit_bytes=VMEM_LIMIT),
        name="pool_mixer",
    )(colop, icnt, proj, proj, pool_w, pool_scale)


def _conv_kernel(x_ref, b_ref, c_ref, gate_ref, cw_ref, cb_ref, o_ref):
    n = x_ref.shape[0]
    u = c_ref[...].astype(F32) * x_ref[...].astype(F32)
    row = lax.broadcasted_iota(jnp.int32, u.shape, 0)
    u_prev = jnp.where(row == 0, 0.0, pltpu.roll(u, 1, axis=0))
    u_next = jnp.where(row == n - 1, 0.0, pltpu.roll(u, n - 1, axis=0))
    cw = cw_ref[...]
    conv = u_prev * cw[0:1, :] + u * cw[1:2, :] + u_next * cw[2:3, :] + cb_ref[...]
    o_ref[...] = (b_ref[...].astype(F32) * conv
                  * _silu(gate_ref[...].astype(F32))).astype(o_ref.dtype)


def _conv_mixer(proj, conv_w, conv_b):
    nb, n, _ = proj.shape
    tc = 256
    nblk = CONV_W // tc
    base = 2 * POOL_W // tc

    def seg(k):
        return pl.BlockSpec((None, n, tc), lambda b, j: (b, 0, base + k * nblk + j))

    return pl.pallas_call(
        _conv_kernel,
        grid=(nb, nblk),
        in_specs=[
            seg(0), seg(1), seg(2), seg(3),
            pl.BlockSpec((3, tc), lambda b, j: (0, j)),
            pl.BlockSpec((1, tc), lambda b, j: (0, j)),
        ],
        out_specs=pl.BlockSpec((None, n, tc), lambda b, j: (b, 0, j)),
        out_shape=jax.ShapeDtypeStruct((nb, n, CONV_W), BF16),
        compiler_params=pltpu.CompilerParams(
            dimension_semantics=("arbitrary", "arbitrary"), vmem_limit_bytes=VMEM_LIMIT),
        name="conv_mixer",
    )(proj, proj, proj, proj, conv_w, conv_b)


def _outproj_kernel(n_in, emit_next, *refs):
    a_refs = refs[:n_in]
    w_refs = refs[n_in:2 * n_in]
    x_ref, gate_ref, pg_ref = refs[2 * n_in:2 * n_in + 3]
    rest = refs[2 * n_in + 3:]
    if emit_next:
        shift_ref, scale_ref, g_ref, o_ref, h_ref = rest[:5]
        wb_refs = rest[5:]
    else:
        o_ref = rest[0]
        wb_refs = rest[1:]

    @pl.when((pl.program_id(0) == 0) & (pl.program_id(1) == 0))
    def _():
        for w_ref, wb_ref in zip(w_refs, wb_refs):
            wb_ref[...] = w_ref[...].astype(BF16)

    y = jnp.dot(a_refs[0][...], wb_refs[0][...], preferred_element_type=F32)
    for a_ref, wb_ref in zip(a_refs[1:], wb_refs[1:]):
        y = y + jnp.dot(a_ref[...], wb_ref[...], preferred_element_type=F32)
    ms = jnp.mean(y * y, axis=-1, keepdims=True)
    r = y * lax.rsqrt(ms + EPS) * pg_ref[...]
    x_new = x_ref[...] + gate_ref[...] * r
    o_ref[...] = x_new
    if emit_next:
        h_ref[...] = _norm_mod(x_new, g_ref[...], scale_ref[...],
                               shift_ref[...]).astype(h_ref.dtype)


def _outproj(acts, w, x3, mod_row_of_batch, gate, pg, tm, next_norm=None):
    nb, rows, _ = x3.shape
    n_in = len(acts)
    k_in = acts[0].shape[-1]
    assert all(a.shape[-1] == k_in for a in acts) and n_in * k_in == w.shape[0]
    emit_next = next_norm is not None
    mod_spec = pl.BlockSpec((None, 1, D_MODEL), lambda b, i: (mod_row_of_batch(b), 0, 0))
    vec_spec = pl.BlockSpec((1, D_MODEL), lambda b, i: (0, 0))
    tok_spec = pl.BlockSpec((None, tm, D_MODEL), lambda b, i: (b, i, 0))
    in_specs = [pl.BlockSpec((None, tm, k_in), lambda b, i: (b, i, 0))] * n_in
    in_specs += [pl.BlockSpec((k_in, D_MODEL), functools.partial(lambda k, b, i: (k, 0), k),
                              pipeline_mode=pl.Buffered(1)) for k in range(n_in)]
    in_specs += [tok_spec, mod_spec, vec_spec]
    operands = [*acts, *([w] * n_in), x3, gate, pg]
    out_specs, out_shape = tok_spec, jax.ShapeDtypeStruct((nb, rows, D_MODEL), F32)
    if emit_next:
        in_specs += [mod_spec, mod_spec, vec_spec]
        operands += list(next_norm)
        out_specs = [tok_spec, tok_spec]
        out_shape = [out_shape, jax.ShapeDtypeStruct((nb, rows, D_MODEL), BF16)]
    return pl.pallas_call(
        functools.partial(_outproj_kernel, n_in, emit_next),
        grid=(nb, rows // tm),
        in_specs=in_specs,
        out_specs=out_specs,
        out_shape=out_shape,
        scratch_shapes=[pltpu.VMEM((k_in, D_MODEL), BF16)] * n_in,
        compiler_params=pltpu.CompilerParams(
            dimension_semantics=("arbitrary", "arbitrary"), vmem_limit_bytes=VMEM_LIMIT),
        name="outproj",
    )(*operands)


_NT = (((1,), (1,)), ((), ()))
_TN = (((0,), (0,)), ((), ()))
SUBLANES = 8
SCAN_HEADS = 2
SCAN_UNROLL = 1
SCAN_CHUNK = 2 * CHUNK


def _chunk_cumsum(x, reverse, in_ref, out_ref):
    c, w = x.shape
    nblk = c // SUBLANES
    order = range(nblk - 1, -1, -1) if reverse else range(nblk)
    in_ref[...] = x
    pref = [None] * nblk
    acc = None
    for j in order:
        blk = in_ref[pl.ds(j, SUBLANES, stride=nblk), :]
        acc = blk if acc is None else acc + blk
        pref[j] = acc
    total = acc
    sub = lax.broadcasted_iota(jnp.int32, (SUBLANES, w), 0)
    incl = total
    for s in (1, 2, 4):
        if reverse:
            incl = incl + jnp.where(sub < SUBLANES - s,
                                    pltpu.roll(incl, SUBLANES - s, axis=0), 0.0)
        else:
            incl = incl + jnp.where(sub >= s, pltpu.roll(incl, s, axis=0), 0.0)
    before = incl - total
    for j in range(nblk):
        out_ref[pl.ds(j, SUBLANES, stride=nblk), :] = pref[j] + before
    edge = incl[0:1, :] if reverse else incl[SUBLANES - 1:SUBLANES, :]
    return out_ref[...], jnp.broadcast_to(edge, (SUBLANES, w))


def _scan_kernel(zf_ref, zb_ref, v_ref, q_ref, g_ref, czf_ref, czb_ref, cv_ref,
                 lbf_ref, lbb_ref, og_ref, o_ref,
                 qd_s, ki_s, kd_s, dec_s, mid_s, cs_s, sc_s, ds_s, st_s, stt_s, of_s, ob_s):
    c = SCAN_CHUNK
    half = CHUNK
    n_lat = zf_ref.shape[0] // c
    n_ctx = czf_ref.shape[0] // c
    n_heads = zf_ref.shape[1] // HG_DK
    row = lax.broadcasted_iota(jnp.int32, (c, c), 0)
    col = lax.broadcasted_iota(jnp.int32, (c, c), 1)
    dirs = ((zf_ref, czf_ref, lbf_ref[...], False, col <= row, of_s),
            (zb_ref, czb_ref, lbb_ref[...], True, col >= row, ob_s))

    def lanes(h):
        return slice(h * HG_DK, (h + 1) * HG_DK)

    def rows(chunk):
        return pl.ds(pl.multiple_of(chunk * c, c), c)

    def dec_rows(chunk):
        return pl.ds(pl.multiple_of(chunk * SUBLANES, SUBLANES), SUBLANES)

    def gate_decay(zh, lb, reverse, slot):
        bt = (0.5 * (1.0 - lb)) * jnp.tanh(zh)
        k = 0.5 * (1.0 - lb) - bt
        lf = jnp.log(0.5 * (1.0 + lb) + bt)
        lo, hi = lf[:half], lf[half:]
        first, second = (hi, lo) if reverse else (lo, hi)
        away, t_first = _chunk_cumsum(first, not reverse, cs_s.at[slot, 0], cs_s.at[slot, 1])
        toward, t_second = _chunk_cumsum(second, reverse, cs_s.at[slot, 2], cs_s.at[slot, 3])
        a_first = first - away
        a = jnp.concatenate([toward, a_first] if reverse else [a_first, toward], axis=0)
        k_inv = (k * jnp.exp(-a)).astype(BF16)
        d_first = jnp.exp(t_first)
        d_second = jnp.exp(t_second)
        k_dec = k_inv * d_second[0:1, :].astype(BF16)
        return a, k_inv, k_dec, d_first, d_first * d_second

    def update_state(d, h, v, k_dec, decay):
        ds_t = lax.dot_general(v, k_dec, _TN, preferred_element_type=F32)
        st_s[d, h] = st_s[d, h] * decay + ds_t

    def cumsum_slot(u, d, h):
        return (u * 2 + d) * n_heads + h

    def prep(d, chunk, u):
        z_ref, _, lb, reverse, _, _ = dirs[d]
        sl = rows(chunk)
        for h in range(n_heads):
            a, k_inv, k_dec, d_first, decay = gate_decay(
                z_ref[sl, lanes(h)], lb[:, lanes(h)], reverse, cumsum_slot(u, d, h))
            qd_s[d, sl, lanes(h)] = q_ref[sl, lanes(h)] * jnp.exp(a).astype(BF16)
            ki_s[d, sl, lanes(h)] = k_inv.T
            kd_s[d, sl, lanes(h)] = k_dec
            dec_s[d, dec_rows(chunk), lanes(h)] = decay
            mid_s[d, dec_rows(chunk), lanes(h)] = d_first

    def state_rows(chunk):
        return pl.ds(pl.multiple_of(chunk * HG_DV, HG_DV), HG_DV)

    def local(d, chunk, _):
        mask = dirs[d][4]
        sl = rows(chunk)
        for h in range(n_heads):
            sc = jnp.dot(qd_s[d, sl, lanes(h)], ki_s[d, sl, lanes(h)],
                         preferred_element_type=F32)
            sc_s[d, h, sl, :] = jnp.where(mask, sc, 0.0).astype(BF16)
            ds_s[d, h, state_rows(chunk), :] = lax.dot_general(
                v_ref[sl, lanes(h)], kd_s[d, sl, lanes(h)], _TN, preferred_element_type=F32)

    def step(d, chunk, _):
        out_s = dirs[d][5]
        sl = rows(chunk)
        decay = dec_s[d, dec_rows(chunk), :]
        d_first = mid_s[d, dec_rows(chunk), :]
        for h in range(n_heads):
            st_mid = (st_s[d, h] * d_first[0:1, lanes(h)]).astype(BF16)
            stt_s[d, h] = st_mid.T
            inter = jnp.dot(qd_s[d, sl, lanes(h)], stt_s[d, h], preferred_element_type=F32)
            out_s[sl, lanes(h)] = inter + jnp.dot(sc_s[d, h, sl, :], v_ref[sl, lanes(h)],
                                                  preferred_element_type=F32)
            st_s[d, h] = (st_s[d, h] * decay[0:1, lanes(h)]
                          + ds_s[d, h, state_rows(chunk), :])

    st_s[...] = jnp.zeros_like(st_s)
    for j in range(n_ctx):
        for d in range(2):
            _, cz_ref, lb, reverse, _, _ = dirs[d]
            sl = pl.ds((n_ctx - 1 - j if reverse else j) * c, c)
            for h in range(n_heads):
                _, _, k_dec, _, decay = gate_decay(cz_ref[sl, lanes(h)], lb[:, lanes(h)],
                                                   reverse, cumsum_slot(j % SCAN_UNROLL, d, h))
                update_state(d, h, cv_ref[sl, lanes(h)], k_dec, decay[0:1, :])

    def chunk_of(d, idx):
        return n_lat - 1 - idx if dirs[d][3] else idx

    def for_chunks(stage, first):
        for u in range(SCAN_UNROLL):
            idx = first + u
            idx = min(idx, n_lat - 1) if isinstance(idx, int) else jnp.minimum(idx, n_lat - 1)
            for d in range(2):
                stage(d, chunk_of(d, idx), u)

    for_chunks(prep, 0)
    for_chunks(prep, SCAN_UNROLL)
    for_chunks(local, 0)

    def body(it, carry):
        for_chunks(step, it * SCAN_UNROLL)
        for_chunks(local, (it + 1) * SCAN_UNROLL)
        for_chunks(prep, (it + 2) * SCAN_UNROLL)
        return carry

    lax.fori_loop(0, n_lat // SCAN_UNROLL, body, 0)

    og = og_ref[...]
    ones = jnp.ones((HG_DV, HG_DV), BF16)
    for h in range(n_heads):
        o = of_s[:, lanes(h)] + ob_s[:, lanes(h)]
        ss = jnp.dot((o * o).astype(BF16), ones, preferred_element_type=F32)
        o = o * lax.rsqrt(ss * (1.0 / HG_DV) + EPS)
        gh = g_ref[:, lanes(h)].astype(F32)
        o_ref[:, lanes(h)] = (o * og[:, lanes(h)] * (gh + gh * jnp.tanh(gh))).astype(o_ref.dtype)


def _hgrn2_scan(zf, zb, v, q, g, czf, czb, cv, lb_f, lb_b, onorm_g):
    _, nb, n, w = zf.shape
    nc = czf.shape[2]
    lat = pl.BlockSpec((None, None, n, w), lambda b, h: (h, b, 0, 0))
    ctx = pl.BlockSpec((None, None, nc, w), lambda b, h: (h, b, 0, 0))
    vec = pl.BlockSpec((1, w), lambda b, h: (0, h))
    return pl.pallas_call(
        _scan_kernel,
        grid=(nb, HG_HEADS // SCAN_HEADS),
        in_specs=[lat] * 5 + [ctx] * 3 + [vec] * 3,
        out_specs=pl.BlockSpec((None, n, w), lambda b, h: (b, 0, h)),
        out_shape=jax.ShapeDtypeStruct((nb, n, HG_V), BF16),
        scratch_shapes=[pltpu.VMEM((2, n, w), BF16),
                        pltpu.VMEM((2, n, w), BF16),
                        pltpu.VMEM((2, n, w), BF16),
                        pltpu.VMEM((2, n // SCAN_CHUNK * SUBLANES, w), F32),
                        pltpu.VMEM((2, n // SCAN_CHUNK * SUBLANES, w), F32),
                        pltpu.VMEM((SCAN_UNROLL * 2 * SCAN_HEADS, 4, CHUNK, HG_DK), F32),
                        pltpu.VMEM((2, SCAN_HEADS, n, SCAN_CHUNK), BF16),
                        pltpu.VMEM((2, SCAN_HEADS, n // SCAN_CHUNK * HG_DV, HG_DK), F32),
                        pltpu.VMEM((2, SCAN_HEADS, HG_DV, HG_DK), F32),
                        pltpu.VMEM((2, SCAN_HEADS, HG_DK, HG_DV), BF16),
                        pltpu.VMEM((n, w), F32),
                        pltpu.VMEM((n, w), F32)],
        compiler_params=pltpu.CompilerParams(
            dimension_semantics=("arbitrary", "arbitrary"), vmem_limit_bytes=VMEM_LIMIT),
        name="hgrn2_scan",
    )(zf, zb, v, q, g, czf, czb, cv, lb_f, lb_b, onorm_g)


def _window(n, w):
    t = np.arange(n)
    lo = np.maximum(t - w // 2, 0)
    hi = np.minimum(t + w // 2 - 1, n - 1)
    s = np.arange(n)
    inside = (s[None, :] >= lo[:, None]) & (s[None, :] <= hi[:, None])
    return inside, (hi - lo + 1)


def _pool_operators(n, on_grid):
    mats, icnts, taps = [], [], []
    for w in POOL_WINDOWS:
        if on_grid:
            in_c, cnt_c = _window(GRID_W, w)
            _, cnt_r = _window(n // GRID_W, w)
            m = np.kron(np.eye(POOL_SLAB // GRID_W, dtype=bool), in_c)
            cnt = (cnt_r[:, None] * cnt_c[None, :]).reshape(n)
            taps.append((-(w // 2), w // 2 - 1))
        else:
            assert n == POOL_SLAB
            m, cnt = _window(n, w)
            taps.append((0, 0))
        mats.append(m)
        icnts.append(1.0 / cnt.astype(np.float64))
    colop = jnp.asarray(np.stack(mats).astype(np.float32), dtype=BF16)
    icnt = jnp.asarray(np.stack(icnts).astype(np.float32)[..., None])
    return colop, icnt, tuple(taps)


def kernel(x, c, ctx, c_ctx, ada_w, ada_b, pre_g, post_g, ev_w_in, ev_pool_w, ev_pool_scale,
           ev_conv_w, ev_conv_b, ev_w_out, od_w_in, od_onorm_g, od_w_out, lb_logits):
    nb, n, d = x.shape
    nc = ctx.shape[1]
    lat_row = lambda b: b
    ctx_row = lambda b: CTX_ROW

    lb_table = jnp.cumsum(jax.nn.softmax(lb_logits.astype(F32), axis=1), axis=1)

    cc = jnp.zeros((MOD_ROWS, d), F32).at[:nb].set(c).at[CTX_ROW].set(c_ctx)
    mod = _ada_table(cc, ada_w, ada_b)
    mod = mod.reshape(2, MOD_ROWS, 3, 1, d)
    shift = [mod[l, :, 0] for l in range(2)]
    scale = [mod[l, :, 1] for l in range(2)]
    gate = [mod[l, :, 2] for l in range(2)]

    ctx_flat = ctx.reshape(1, nb * nc, d)

    pool_w = ev_pool_w[0].astype(BF16)
    pool_scale = ev_pool_scale[0].reshape(1, POOL_W)
    conv_w = ev_conv_w[0]
    conv_b = ev_conv_b[0].reshape(1, CONV_W)
    pre0 = pre_g[0].reshape(1, d)
    post0 = post_g[0].reshape(1, d)

    pre1 = pre_g[1].reshape(1, d)
    post1 = post_g[1].reshape(1, d)
    norm1 = (shift[1], scale[1], pre1)

    def even_mixer(tokens3, mod_row, on_grid):
        b3, n3, _ = tokens3.shape
        h = _norm_tokens(tokens3, mod_row, shift[0], scale[0], pre0, TM_NORM)
        proj, = _inproj(h.reshape(b3 * n3, d), ev_w_in[0], EVEN_IN, [BF16], TM_IN, 1024)
        n_tok = n if on_grid else nc
        proj = proj.reshape(nb, n_tok, EVEN_IN)
        colop, icnt, taps = _pool_operators(n_tok, on_grid)
        a_out = _pool_mixer(proj, colop, icnt, taps, pool_w, pool_scale)
        b_out = _conv_mixer(proj, conv_w, conv_b)
        acts = [a_out.reshape(b3, n3, POOL_W), b_out.reshape(b3, n3, CONV_W)]
        return _outproj(acts, ev_w_out[0], tokens3, mod_row, gate[0], post0, TM_OUT, norm1)

    x1, h1 = even_mixer(x, lat_row, True)
    _, hc1 = even_mixer(ctx_flat, ctx_row, False)

    tn1 = SCAN_HEADS * HG_DK
    groups = HG_HEADS // SCAN_HEADS
    zf, zb, v, q, g = [t.reshape(groups, nb, n, tn1) for t in
                       _inproj(h1.reshape(nb * n, d), od_w_in[0], HG_K,
                               [F32, F32, BF16, BF16, BF16], TM_IN, tn1,
                               (0.5, 0.5, 1.0, 1.0, 0.5), tile_major=True)]
    czf, czb, cv = [t.reshape(groups, nb, nc, tn1) for t in
                    _inproj(hc1.reshape(nb * nc, d), od_w_in[0], HG_K,
                            [F32, F32, BF16], TM_IN, tn1, (0.5, 0.5, 1.0), tile_major=True)]
    o = _hgrn2_scan(zf, zb, v, q, g, czf, czb, cv, lb_table[0, 1].reshape(1, HG_K),
                    lb_table[1, 1].reshape(1, HG_K), od_onorm_g[0].reshape(1, HG_V))
    return _outproj([o], od_w_out[0], x1, lat_row, gate[1], post1, TM_OUT)
```

```python
import functools

import numpy as np
import jax
import jax.numpy as jnp
from jax import lax
from jax.experimental import pallas as pl
from jax.experimental.pallas import tpu as pltpu

F32 = jnp.float32
BF16 = jnp.bfloat16

D_MODEL = 2048
BATCH = 4
SEQ = 2048
CTX_LEN = 256
GRID_W = 64
EPS = 1e-6

POOL_WINDOWS = (2, 4, 8, 16)
POOL_W = D_MODEL // 2
POOL_GROUP = POOL_W // len(POOL_WINDOWS)
CONV_W = D_MODEL // 2
EVEN_IN = 2 * POOL_W + 4 * CONV_W

HG_DK = 128
HG_HEADS = D_MODEL // HG_DK
HG_DV = D_MODEL // HG_HEADS
HG_K = HG_HEADS * HG_DK
HG_V = HG_HEADS * HG_DV
ODD_IN = 3 * HG_K + 2 * HG_V
CHUNK = 64

MOD_ROWS = 8
CTX_ROW = BATCH
VMEM_LIMIT = 56 * 1024 * 1024
TM_NORM = 512
TM_IN = 1024
TM_OUT = 512


def _silu(v):
    hv = 0.5 * v
    return hv + hv * jnp.tanh(hv)


def _ada_kernel(cc_ref, w_ref, b_ref, o_ref):
    s = _silu(cc_ref[...])
    o_ref[...] = jnp.dot(s, w_ref[...], preferred_element_type=F32) + b_ref[...]


def _ada_table(cc, ada_w, ada_b):
    depth = ada_w.shape[0]
    tn = 1024
    return pl.pallas_call(
        _ada_kernel,
        grid=(depth, 3 * D_MODEL // tn),
        in_specs=[
            pl.BlockSpec((MOD_ROWS, D_MODEL), lambda l, j: (0, 0)),
            pl.BlockSpec((None, D_MODEL, tn), lambda l, j: (l, 0, j)),
            pl.BlockSpec((None, 1, tn), lambda l, j: (l, 0, j)),
        ],
        out_specs=pl.BlockSpec((None, MOD_ROWS, tn), lambda l, j: (l, 0, j)),
        out_shape=jax.ShapeDtypeStruct((depth, MOD_ROWS, 3 * D_MODEL), F32),
        compiler_params=pltpu.CompilerParams(
            dimension_semantics=("arbitrary", "arbitrary"), vmem_limit_bytes=VMEM_LIMIT),
        name="ada_table",
    )(cc, ada_w, ada_b.reshape(depth, 1, 3 * D_MODEL))


def _norm_mod(x, g, scale, shift):
    ms = jnp.mean(x * x, axis=-1, keepdims=True)
    y = x * lax.rsqrt(ms + EPS) * g
    return y * (1.0 + scale) + shift


def _norm_kernel(x_ref, shift_ref, scale_ref, g_ref, o_ref):
    o_ref[...] = _norm_mod(x_ref[...], g_ref[...], scale_ref[...],
                           shift_ref[...]).astype(o_ref.dtype)


def _norm_tokens(x3, mod_row_of_batch, shift, scale, g, tm):
    nb, rows, _ = x3.shape
    mod_map = lambda b, i: (mod_row_of_batch(b), 0, 0)
    return pl.pallas_call(
        _norm_kernel,
        grid=(nb, rows // tm),
        in_specs=[
            pl.BlockSpec((None, tm, D_MODEL), lambda b, i: (b, i, 0)),
            pl.BlockSpec((None, 1, D_MODEL), mod_map),
            pl.BlockSpec((None, 1, D_MODEL), mod_map),
            pl.BlockSpec((1, D_MODEL), lambda b, i: (0, 0)),
        ],
        out_specs=pl.BlockSpec((None, tm, D_MODEL), lambda b, i: (b, i, 0)),
        out_shape=jax.ShapeDtypeStruct((nb, rows, D_MODEL), BF16),
        compiler_params=pltpu.CompilerParams(
            dimension_semantics=("arbitrary", "arbitrary"), vmem_limit_bytes=VMEM_LIMIT),
        name="norm_tokens",
    )(x3, shift, scale, g)


def _inproj_kernel(n_seg, h_ref, *refs):
    w_refs = refs[:n_seg]
    o_refs = refs[n_seg:2 * n_seg]
    wb_refs = refs[2 * n_seg:]

    @pl.when(pl.program_id(1) == 0)
    def _():
        for w_ref, wb_ref in zip(w_refs, wb_refs):
            wb_ref[...] = w_ref[...].astype(BF16)

    for wb_ref, o_ref in zip(wb_refs, o_refs):
        o_ref[...] = jnp.dot(h_ref[...], wb_ref[...],
                             preferred_element_type=F32).astype(o_ref.dtype)


def _inproj(h, w, seg_width, seg_dtypes, tm, tn):
    rows = h.shape[0]
    n_seg = len(seg_dtypes)
    n_tiles = seg_width // tn
    w_specs = [pl.BlockSpec((D_MODEL, tn), functools.partial(
        lambda k, j, i: (0, k * n_tiles + j), k)) for k in range(n_seg)]
    return pl.pallas_call(
        functools.partial(_inproj_kernel, n_seg),
        grid=(n_tiles, rows // tm),
        in_specs=[pl.BlockSpec((tm, D_MODEL), lambda j, i: (i, 0))] + w_specs,
        out_specs=[pl.BlockSpec((tm, tn), lambda j, i: (i, j))] * n_seg,
        out_shape=[jax.ShapeDtypeStruct((rows, seg_width), dt) for dt in seg_dtypes],
        scratch_shapes=[pltpu.VMEM((D_MODEL, tn), BF16)] * n_seg,
        compiler_params=pltpu.CompilerParams(
            dimension_semantics=("arbitrary", "arbitrary"), vmem_limit_bytes=VMEM_LIMIT),
        name="inproj",
    )(h, *([w] * n_seg))


POOL_SLAB = 256
POOL_ROWS = 128


def _pool_kernel(taps, pad, colop_ref, icnt_ref, v_ref, gate_ref, w_ref, sc_ref, o_ref,
                 box_s, pooled_s):
    n = v_ref.shape[0]
    if pad:
        box_s[0:pad, :] = jnp.zeros((pad, POOL_GROUP), F32)
        box_s[pad + n:pad + n + pad, :] = jnp.zeros((pad, POOL_GROUP), F32)
    for gi, (lo, hi) in enumerate(taps):
        lanes = slice(gi * POOL_GROUP, (gi + 1) * POOL_GROUP)
        for s in range(n // POOL_SLAB):
            tok = slice(s * POOL_SLAB, (s + 1) * POOL_SLAB)
            box_s[pad + s * POOL_SLAB:pad + (s + 1) * POOL_SLAB, :] = jnp.dot(
                colop_ref[gi], v_ref[tok, lanes], preferred_element_type=F32)

        def block(i, carry, gi=gi, lo=lo, hi=hi, lanes=lanes):
            r0 = pl.multiple_of(i * POOL_ROWS, POOL_ROWS)
            tok = pl.ds(r0, POOL_ROWS)
            acc = box_s[pl.ds(pad + lo * GRID_W + r0, POOL_ROWS), :]
            for dlt in range(lo + 1, hi + 1):
                acc = acc + box_s[pl.ds(pad + dlt * GRID_W + r0, POOL_ROWS), :]
            pooled = acc * icnt_ref[gi, tok, :] - v_ref[tok, lanes].astype(F32)
            pooled_s[tok, :] = pooled.astype(BF16)
            return carry

        lax.fori_loop(0, n // POOL_ROWS, block, 0, unroll=2)
        mixed = jnp.dot(pooled_s[...], w_ref[gi], preferred_element_type=F32)
        o_ref[:, lanes] = (mixed * sc_ref[:, lanes]
                           * _silu(gate_ref[:, lanes].astype(F32))).astype(o_ref.dtype)


def _pool_mixer(proj, colop, icnt, taps, pool_w, pool_scale):
    nb, n, _ = proj.shape
    ng = len(POOL_WINDOWS)
    pad = max(max(-lo, hi) for lo, hi in taps) * GRID_W
    pad = -(-pad // POOL_ROWS) * POOL_ROWS
    return pl.pallas_call(
        functools.partial(_pool_kernel, taps, pad),
        grid=(nb,),
        in_specs=[
            pl.BlockSpec((ng, POOL_SLAB, POOL_SLAB), lambda b: (0, 0, 0)),
            pl.BlockSpec((ng, n, 1), lambda b: (0, 0, 0)),
            pl.BlockSpec((None, n, POOL_W), lambda b: (b, 0, 0)),
            pl.BlockSpec((None, n, POOL_W), lambda b: (b, 0, 1)),
            pl.BlockSpec((ng, POOL_GROUP, POOL_GROUP), lambda b: (0, 0, 0)),
            pl.BlockSpec((1, POOL_W), lambda b: (0, 0)),
        ],
        out_specs=pl.BlockSpec((None, n, POOL_W), lambda b: (b, 0, 0)),
        out_shape=jax.ShapeDtypeStruct((nb, n, POOL_W), BF16),
        scratch_shapes=[pltpu.VMEM((n + 2 * pad, POOL_GROUP), F32),
                        pltpu.VMEM((n, POOL_GROUP), BF16)],
        compiler_params=pltpu.CompilerParams(
            dimension_semantics=("arbitrary",), vmem_limit_bytes=VMEM_LIMIT),
        name="pool_mixer",
    )(colop, icnt, proj, proj, pool_w, pool_scale)


def _conv_kernel(x_ref, b_ref, c_ref, gate_ref, cw_ref, cb_ref, o_ref):
    n = x_ref.shape[0]
    u = c_ref[...].astype(F32) * x_ref[...].astype(F32)
    row = lax.broadcasted_iota(jnp.int32, u.shape, 0)
    u_prev = jnp.where(row == 0, 0.0, pltpu.roll(u, 1, axis=0))
    u_next = jnp.where(row == n - 1, 0.0, pltpu.roll(u, n - 1, axis=0))
    cw = cw_ref[...]
    conv = u_prev * cw[0:1, :] + u * cw[1:2, :] + u_next * cw[2:3, :] + cb_ref[...]
    o_ref[...] = (b_ref[...].astype(F32) * conv
                  * _silu(gate_ref[...].astype(F32))).astype(o_ref.dtype)


def _conv_mixer(proj, conv_w, conv_b):
    nb, n, _ = proj.shape
    tc = 256
    nblk = CONV_W // tc
    base = 2 * POOL_W // tc

    def seg(k):
        return pl.BlockSpec((None, n, tc), lambda b, j: (b, 0, base + k * nblk + j))

    return pl.pallas_call(
        _conv_kernel,
        grid=(nb, nblk),
        in_specs=[
            seg(0), seg(1), seg(2), seg(3),
            pl.BlockSpec((3, tc), lambda b, j: (0, j)),
            pl.BlockSpec((1, tc), lambda b, j: (0, j)),
        ],
        out_specs=pl.BlockSpec((None, n, tc), lambda b, j: (b, 0, j)),
        out_shape=jax.ShapeDtypeStruct((nb, n, CONV_W), BF16),
        compiler_params=pltpu.CompilerParams(
            dimension_semantics=("arbitrary", "arbitrary"), vmem_limit_bytes=VMEM_LIMIT),
        name="conv_mixer",
    )(proj, proj, proj, proj, conv_w, conv_b)


def _outproj_kernel(n_in, emit_next, *refs):
    a_refs = refs[:n_in]
    w_refs = refs[n_in:2 * n_in]
    x_ref, gate_ref, pg_ref = refs[2 * n_in:2 * n_in + 3]
    rest = refs[2 * n_in + 3:]
    if emit_next:
        shift_ref, scale_ref, g_ref, o_ref, h_ref = rest[:5]
        wb_refs = rest[5:]
    else:
        o_ref = rest[0]
        wb_refs = rest[1:]

    @pl.when((pl.program_id(0) == 0) & (pl.program_id(1) == 0))
    def _():
        for w_ref, wb_ref in zip(w_refs, wb_refs):
            wb_ref[...] = w_ref[...].astype(BF16)

    y = jnp.dot(a_refs[0][...], wb_refs[0][...], preferred_element_type=F32)
    for a_ref, wb_ref in zip(a_refs[1:], wb_refs[1:]):
        y = y + jnp.dot(a_ref[...], wb_ref[...], preferred_element_type=F32)
    ms = jnp.mean(y * y, axis=-1, keepdims=True)
    r = y * lax.rsqrt(ms + EPS) * pg_ref[...]
    x_new = x_ref[...] + gate_ref[...] * r
    o_ref[...] = x_new
    if emit_next:
        h_ref[...] = _norm_mod(x_new, g_ref[...], scale_ref[...],
                               shift_ref[...]).astype(h_ref.dtype)


def _outproj(acts, w, x3, mod_row_of_batch, gate, pg, tm, next_norm=None):
    nb, rows, _ = x3.shape
    n_in = len(acts)
    k_in = acts[0].shape[-1]
    assert all(a.shape[-1] == k_in for a in acts) and n_in * k_in == w.shape[0]
    emit_next = next_norm is not None
    mod_spec = pl.BlockSpec((None, 1, D_MODEL), lambda b, i: (mod_row_of_batch(b), 0, 0))
    vec_spec = pl.BlockSpec((1, D_MODEL), lambda b, i: (0, 0))
    tok_spec = pl.BlockSpec((None, tm, D_MODEL), lambda b, i: (b, i, 0))
    in_specs = [pl.BlockSpec((None, tm, k_in), lambda b, i: (b, i, 0))] * n_in
    in_specs += [pl.BlockSpec((k_in, D_MODEL), functools.partial(lambda k, b, i: (k, 0), k),
                              pipeline_mode=pl.Buffered(1)) for k in range(n_in)]
    in_specs += [tok_spec, mod_spec, vec_spec]
    operands = [*acts, *([w] * n_in), x3, gate, pg]
    out_specs, out_shape = tok_spec, jax.ShapeDtypeStruct((nb, rows, D_MODEL), F32)
    if emit_next:
        in_specs += [mod_spec, mod_spec, vec_spec]
        operands += list(next_norm)
        out_specs = [tok_spec, tok_spec]
        out_shape = [out_shape, jax.ShapeDtypeStruct((nb, rows, D_MODEL), BF16)]
    return pl.pallas_call(
        functools.partial(_outproj_kernel, n_in, emit_next),
        grid=(nb, rows // tm),
        in_specs=in_specs,
        out_specs=out_specs,
        out_shape=out_shape,
        scratch_shapes=[pltpu.VMEM((k_in, D_MODEL), BF16)] * n_in,
        compiler_params=pltpu.CompilerParams(
            dimension_semantics=("arbitrary", "arbitrary"), vmem_limit_bytes=VMEM_LIMIT),
        name="outproj",
    )(*operands)


_NT = (((1,), (1,)), ((), ()))
_TN = (((0,), (0,)), ((), ()))
SUBLANES = 8
SCAN_HEADS = 2
SCAN_CHUNK = 2 * CHUNK
SCAN_UNROLL = 2


def _chunk_cumsum(x, reverse, in_ref, out_ref):
    c, w = x.shape
    nblk = c // SUBLANES
    order = range(nblk - 1, -1, -1) if reverse else range(nblk)
    in_ref[...] = x
    pref = [None] * nblk
    acc = None
    for j in order:
        blk = in_ref[pl.ds(j, SUBLANES, stride=nblk), :]
        acc = blk if acc is None else acc + blk
        pref[j] = acc
    total = acc
    sub = lax.broadcasted_iota(jnp.int32, (SUBLANES, w), 0)
    incl = total
    for s in (1, 2, 4):
        if reverse:
            incl = incl + jnp.where(sub < SUBLANES - s,
                                    pltpu.roll(incl, SUBLANES - s, axis=0), 0.0)
        else:
            incl = incl + jnp.where(sub >= s, pltpu.roll(incl, s, axis=0), 0.0)
    before = incl - total
    for j in range(nblk):
        out_ref[pl.ds(j, SUBLANES, stride=nblk), :] = pref[j] + before
    edge = incl[0:1, :] if reverse else incl[SUBLANES - 1:SUBLANES, :]
    return out_ref[...], jnp.broadcast_to(edge, (SUBLANES, w))


def _gate_decay(zh, lb, reverse, cs_ref):
    bt = (0.5 * (1.0 - lb)) * jnp.tanh(zh)
    k = 0.5 * (1.0 - lb) - bt
    lf = jnp.log(0.5 * (1.0 + lb) + bt)
    lo, hi = lf[:CHUNK], lf[CHUNK:]
    first, second = (hi, lo) if reverse else (lo, hi)
    away, t_first = _chunk_cumsum(first, not reverse, cs_ref.at[0], cs_ref.at[1])
    toward, t_second = _chunk_cumsum(second, reverse, cs_ref.at[2], cs_ref.at[3])
    a_first = first - away
    a = jnp.concatenate([toward, a_first] if reverse else [a_first, toward], axis=0)
    k_inv = (k * jnp.exp(-a)).astype(BF16)
    d_first = jnp.exp(t_first)
    d_second = jnp.exp(t_second)
    k_dec = k_inv * d_second[0:1, :].astype(BF16)
    return a, k_inv, k_dec, d_first, d_first * d_second


def _hgrn_inproj_kernel(with_queries, h_ref, lbf_ref, lbb_ref, *refs):
    n_w = 5 if with_queries else 3
    w_refs, rest = refs[:n_w], refs[n_w:]
    if with_queries:
        per_dir = (rest[0:5], rest[5:10])
        v_ref, g_ref = rest[10:12]
        rest = rest[12:]
    else:
        per_dir = (rest[0:2], rest[2:4])
        v_ref = rest[4]
        rest = rest[5:]
    wb_refs, cs_s = rest[:n_w], rest[n_w]

    @pl.when(pl.program_id(1) == 0)
    def _():
        for w_ref, wb_ref in zip(w_refs, wb_refs):
            wb_ref[...] = w_ref[...].astype(BF16)

    h = h_ref[...]

    def proj(k):
        return jnp.dot(h, wb_refs[k][...], preferred_element_type=F32)

    tm, w = v_ref.shape
    n_chunks = tm // SCAN_CHUNK
    n_heads = w // HG_DK
    for d, (lb_ref, reverse) in enumerate(((lbf_ref, False), (lbb_ref, True))):
        zh = proj(d) * 0.5
        if with_queries and d == 0:
            q = proj(3).astype(BF16)
        lb = lb_ref[...]
        for c in range(n_chunks):
            rows = slice(c * SCAN_CHUNK, (c + 1) * SCAN_CHUNK)
            drows = slice(c * SUBLANES, (c + 1) * SUBLANES)
            for hd in range(n_heads):
                lanes = slice(hd * HG_DK, (hd + 1) * HG_DK)
                slot = (d * n_chunks + c) * n_heads + hd
                a, k_inv, k_dec, d_first, decay = _gate_decay(
                    zh[rows, lanes], lb[:, lanes], reverse, cs_s.at[slot])
                if with_queries:
                    qd_ref, ki_ref, kd_ref, dec_ref, mid_ref = per_dir[d]
                    qd_ref[rows, lanes] = q[rows, lanes] * jnp.exp(a).astype(BF16)
                    ki_ref[rows, lanes] = k_inv
                    mid_ref[drows, lanes] = d_first
                else:
                    kd_ref, dec_ref = per_dir[d]
                kd_ref[rows, lanes] = k_dec
                dec_ref[drows, lanes] = decay
    v_ref[...] = proj(2).astype(BF16)
    if with_queries:
        g_ref[...] = (proj(4) * 0.5).astype(BF16)


def _hgrn_inproj(h, w, lb_f, lb_b, with_queries, tm):
    rows = h.shape[0]
    tn = SCAN_HEADS * HG_DK
    groups = HG_K // tn
    n_w = 5 if with_queries else 3
    w_specs = [pl.BlockSpec((D_MODEL, tn), functools.partial(
        lambda k, j, i: (0, k * groups + j), k)) for k in range(n_w)]
    lb_spec = pl.BlockSpec((1, tn), lambda j, i: (0, j))
    dec_rows = tm // SCAN_CHUNK * SUBLANES
    tok = (pl.BlockSpec((None, tm, tn), lambda j, i: (j, i, 0)),
           jax.ShapeDtypeStruct((groups, rows, tn), BF16))
    dec = (pl.BlockSpec((None, dec_rows, tn), lambda j, i: (j, i, 0)),
           jax.ShapeDtypeStruct((groups, rows // SCAN_CHUNK * SUBLANES, tn), F32))
    one_dir = [tok, tok, tok, dec, dec] if with_queries else [tok, dec]
    outs = one_dir + one_dir + ([tok, tok] if with_queries else [tok])
    n_slots = 2 * (tm // SCAN_CHUNK) * SCAN_HEADS
    return pl.pallas_call(
        functools.partial(_hgrn_inproj_kernel, with_queries),
        grid=(groups, rows // tm),
        in_specs=[pl.BlockSpec((tm, D_MODEL), lambda j, i: (i, 0)), lb_spec, lb_spec] + w_specs,
        out_specs=[o[0] for o in outs],
        out_shape=[o[1] for o in outs],
        scratch_shapes=[pltpu.VMEM((D_MODEL, tn), BF16)] * n_w
        + [pltpu.VMEM((n_slots, 4, CHUNK, HG_DK), F32)],
        compiler_params=pltpu.CompilerParams(
            dimension_semantics=("arbitrary", "arbitrary"), vmem_limit_bytes=VMEM_LIMIT),
        name="hgrn_inproj",
    )(h, lb_f, lb_b, *([w] * n_w))


def _scan_kernel(qdf_ref, kif_ref, kdf_ref, decf_ref, midf_ref,
                 qdb_ref, kib_ref, kdb_ref, decb_ref, midb_ref, v_ref, g_ref,
                 ckdf_ref, cdecf_ref, ckdb_ref, cdecb_ref, cv_ref, og_ref, o_ref,
                 sc_s, ds_s, st_s, of_s, ob_s):
    c = SCAN_CHUNK
    n_lat = v_ref.shape[0] // c
    n_ctx = cv_ref.shape[0] // c
    n_heads = v_ref.shape[1] // HG_DK
    row = lax.broadcasted_iota(jnp.int32, (c, c), 0)
    col = lax.broadcasted_iota(jnp.int32, (c, c), 1)
    dirs = ((qdf_ref, kif_ref, kdf_ref, decf_ref, midf_ref, ckdf_ref, cdecf_ref,
             False, col <= row, of_s),
            (qdb_ref, kib_ref, kdb_ref, decb_ref, midb_ref, ckdb_ref, cdecb_ref,
             True, col >= row, ob_s))

    def lanes(h):
        return slice(h * HG_DK, (h + 1) * HG_DK)

    def rows(chunk):
        return pl.ds(pl.multiple_of(chunk * c, c), c)

    def dec_rows(chunk):
        return pl.ds(pl.multiple_of(chunk * SUBLANES, SUBLANES), SUBLANES)

    def state_rows(chunk):
        return pl.ds(pl.multiple_of(chunk * HG_DV, HG_DV), HG_DV)

    def local(d, chunk):
        qd_ref, ki_ref, kd_ref = dirs[d][:3]
        mask = dirs[d][8]
        sl = rows(chunk)
        for h in range(n_heads):
            sc = lax.dot_general(qd_ref[sl, lanes(h)], ki_ref[sl, lanes(h)], _NT,
                                 preferred_element_type=F32)
            sc_s[d, h, sl, :] = jnp.where(mask, sc, 0.0).astype(BF16)
            ds_s[d, h, state_rows(chunk), :] = lax.dot_general(
                v_ref[sl, lanes(h)], kd_ref[sl, lanes(h)], _TN, preferred_element_type=F32)

    def step(d, chunk):
        qd_ref, dec_ref, mid_ref, out_s = dirs[d][0], dirs[d][3], dirs[d][4], dirs[d][9]
        sl = rows(chunk)
        decay = dec_ref[dec_rows(chunk), :]
        d_first = mid_ref[dec_rows(chunk), :]
        for h in range(n_heads):
            st_mid = (st_s[d, h] * d_first[0:1, lanes(h)]).astype(BF16)
            inter = lax.dot_general(qd_ref[sl, lanes(h)], st_mid, _NT,
                                    preferred_element_type=F32)
            out_s[sl, lanes(h)] = inter + jnp.dot(sc_s[d, h, sl, :], v_ref[sl, lanes(h)],
                                                  preferred_element_type=F32)
            st_s[d, h] = (st_s[d, h] * decay[0:1, lanes(h)]
                          + ds_s[d, h, state_rows(chunk), :])

    st_s[...] = jnp.zeros_like(st_s)
    for j in range(n_ctx):
        for d in range(2):
            ckd_ref, cdec_ref, reverse = dirs[d][5:8]
            cj = n_ctx - 1 - j if reverse else j
            decay = cdec_ref[cj * SUBLANES:(cj + 1) * SUBLANES, :]
            for h in range(n_heads):
                ds_t = lax.dot_general(cv_ref[cj * c:(cj + 1) * c, lanes(h)],
                                       ckd_ref[cj * c:(cj + 1) * c, lanes(h)], _TN,
                                       preferred_element_type=F32)
                st_s[d, h] = st_s[d, h] * decay[0:1, lanes(h)] + ds_t

    def chunk_of(d, idx):
        return n_lat - 1 - idx if dirs[d][7] else idx

    def for_chunks(stage, first):
        for u in range(SCAN_UNROLL):
            idx = jnp.minimum(first + u, n_lat - 1)
            for d in range(2):
                stage(d, chunk_of(d, idx))

    for_chunks(local, 0)

    def body(it, carry):
        for_chunks(step, it * SCAN_UNROLL)
        for_chunks(local, (it + 1) * SCAN_UNROLL)
        return carry

    lax.fori_loop(0, n_lat // SCAN_UNROLL, body, 0)

    og = og_ref[...]
    ones = jnp.ones((HG_DV, HG_DV), BF16)
    for h in range(n_heads):
        o = of_s[:, lanes(h)] + ob_s[:, lanes(h)]
        ss = jnp.dot((o * o).astype(BF16), ones, preferred_element_type=F32)
        o = o * lax.rsqrt(ss * (1.0 / HG_DV) + EPS)
        gh = g_ref[:, lanes(h)].astype(F32)
        o_ref[:, lanes(h)] = (o * og[:, lanes(h)] * (gh + gh * jnp.tanh(gh))).astype(o_ref.dtype)


def _hgrn2_scan(lat_ops, ctx_ops, onorm_g, nb):
    groups, rows, w = lat_ops[-1].shape
    n = rows // nb

    def blocked(a):
        per_batch = a.shape[1] // nb
        return (a.reshape(groups, nb, per_batch, w),
                pl.BlockSpec((None, None, per_batch, w), lambda b, h: (h, b, 0, 0)))

    ops, specs = zip(*[blocked(a) for a in (*lat_ops, *ctx_ops)])
    vec = pl.BlockSpec((1, w), lambda b, h: (0, h))
    return pl.pallas_call(
        _scan_kernel,
        grid=(nb, groups),
        in_specs=list(specs) + [vec],
        out_specs=pl.BlockSpec((None, n, w), lambda b, h: (b, 0, h)),
        out_shape=jax.ShapeDtypeStruct((nb, n, HG_V), BF16),
        scratch_shapes=[pltpu.VMEM((2, SCAN_HEADS, n, SCAN_CHUNK), BF16),
                        pltpu.VMEM((2, SCAN_HEADS, n // SCAN_CHUNK * HG_DV, HG_DK), F32),
                        pltpu.VMEM((2, SCAN_HEADS, HG_DV, HG_DK), F32),
                        pltpu.VMEM((n, w), F32),
                        pltpu.VMEM((n, w), F32)],
        compiler_params=pltpu.CompilerParams(
            dimension_semantics=("arbitrary", "arbitrary"), vmem_limit_bytes=VMEM_LIMIT),
        name="hgrn2_scan",
    )(*ops, onorm_g)


def _window(n, w):
    t = np.arange(n)
    lo = np.maximum(t - w // 2, 0)
    hi = np.minimum(t + w // 2 - 1, n - 1)
    s = np.arange(n)
    inside = (s[None, :] >= lo[:, None]) & (s[None, :] <= hi[:, None])
    return inside, (hi - lo + 1)


def _pool_operators(n, on_grid):
    mats, icnts, taps = [], [], []
    for w in POOL_WINDOWS:
        if on_grid:
            in_c, cnt_c = _window(GRID_W, w)
            _, cnt_r = _window(n // GRID_W, w)
            m = np.kron(np.eye(POOL_SLAB // GRID_W, dtype=bool), in_c)
            cnt = (cnt_r[:, None] * cnt_c[None, :]).reshape(n)
            taps.append((-(w // 2), w // 2 - 1))
        else:
            assert n == POOL_SLAB
            m, cnt = _window(n, w)
            taps.append((0, 0))
        mats.append(m)
        icnts.append(1.0 / cnt.astype(np.float64))
    colop = jnp.asarray(np.stack(mats).astype(np.float32), dtype=BF16)
    icnt = jnp.asarray(np.stack(icnts).astype(np.float32)[..., None])
    return colop, icnt, tuple(taps)


def kernel(x, c, ctx, c_ctx, ada_w, ada_b, pre_g, post_g, ev_w_in, ev_pool_w, ev_pool_scale,
           ev_conv_w, ev_conv_b, ev_w_out, od_w_in, od_onorm_g, od_w_out, lb_logits):
    nb, n, d = x.shape
    nc = ctx.shape[1]
    lat_row = lambda b: b
    ctx_row = lambda b: CTX_ROW

    lb_table = jnp.cumsum(jax.nn.softmax(lb_logits.astype(F32), axis=1), axis=1)

    cc = jnp.zeros((MOD_ROWS, d), F32).at[:nb].set(c).at[CTX_ROW].set(c_ctx)
    mod = _ada_table(cc, ada_w, ada_b)
    mod = mod.reshape(2, MOD_ROWS, 3, 1, d)
    shift = [mod[l, :, 0] for l in range(2)]
    scale = [mod[l, :, 1] for l in range(2)]
    gate = [mod[l, :, 2] for l in range(2)]

    ctx_flat = ctx.reshape(1, nb * nc, d)

    pool_w = ev_pool_w[0].astype(BF16)
    pool_scale = ev_pool_scale[0].reshape(1, POOL_W)
    conv_w = ev_conv_w[0]
    conv_b = ev_conv_b[0].reshape(1, CONV_W)
    pre0 = pre_g[0].reshape(1, d)
    post0 = post_g[0].reshape(1, d)

    pre1 = pre_g[1].reshape(1, d)
    post1 = post_g[1].reshape(1, d)
    norm1 = (shift[1], scale[1], pre1)

    def even_mixer(tokens3, mod_row, on_grid):
        b3, n3, _ = tokens3.shape
        h = _norm_tokens(tokens3, mod_row, shift[0], scale[0], pre0, TM_NORM)
        proj, = _inproj(h.reshape(b3 * n3, d), ev_w_in[0], EVEN_IN, [BF16], TM_IN, 1024)
        n_tok = n if on_grid else nc
        proj = proj.reshape(nb, n_tok, EVEN_IN)
        colop, icnt, taps = _pool_operators(n_tok, on_grid)
        a_out = _pool_mixer(proj, colop, icnt, taps, pool_w, pool_scale)
        b_out = _conv_mixer(proj, conv_w, conv_b)
        acts = [a_out.reshape(b3, n3, POOL_W), b_out.reshape(b3, n3, CONV_W)]
        return _outproj(acts, ev_w_out[0], tokens3, mod_row, gate[0], post0, TM_OUT, norm1)

    x1, h1 = even_mixer(x, lat_row, True)
    _, hc1 = even_mixer(ctx_flat, ctx_row, False)

    lb_f = lb_table[0, 1].reshape(1, HG_K)
    lb_b = lb_table[1, 1].reshape(1, HG_K)
    lat_ops = _hgrn_inproj(h1.reshape(nb * n, d), od_w_in[0], lb_f, lb_b, True, TM_IN)
    ctx_ops = _hgrn_inproj(hc1.reshape(nb * nc, d), od_w_in[0], lb_f, lb_b, False, TM_IN)
    o = _hgrn2_scan(lat_ops, ctx_ops, od_onorm_g[0].reshape(1, HG_V), nb)
    return _outproj([o], od_w_out[0], x1, lat_row, gate[1], post1, TM_OUT)
```

```python
import functools

import numpy as np
import jax
import jax.numpy as jnp
from jax import lax
from jax.experimental import pallas as pl
from jax.experimental.pallas import tpu as pltpu

F32 = jnp.float32
BF16 = jnp.bfloat16

D_MODEL = 2048
BATCH = 4
SEQ = 2048
CTX_LEN = 256
GRID_W = 64
EPS = 1e-6

POOL_WINDOWS = (2, 4, 8, 16)
POOL_W = D_MODEL // 2
POOL_GROUP = POOL_W // len(POOL_WINDOWS)
CONV_W = D_MODEL // 2
EVEN_IN = 2 * POOL_W + 4 * CONV_W

HG_DK = 128
HG_HEADS = D_MODEL // HG_DK
HG_DV = D_MODEL // HG_HEADS
HG_K = HG_HEADS * HG_DK
HG_V = HG_HEADS * HG_DV
ODD_IN = 3 * HG_K + 2 * HG_V
CHUNK = 64

MOD_ROWS = 8
CTX_ROW = BATCH
VMEM_LIMIT = 56 * 1024 * 1024
TM_NORM = 512
TM_IN = 1024
TM_OUT = 512


def _silu(v):
    hv = 0.5 * v
    return hv + hv * jnp.tanh(hv)


def _ada_kernel(cc_ref, w_ref, b_ref, o_ref):
    s = _silu(cc_ref[...])
    o_ref[...] = jnp.dot(s, w_ref[...], preferred_element_type=F32) + b_ref[...]


def _ada_table(cc, ada_w, ada_b):
    depth = ada_w.shape[0]
    tn = 1024
    return pl.pallas_call(
        _ada_kernel,
        grid=(depth, 3 * D_MODEL // tn),
        in_specs=[
            pl.BlockSpec((MOD_ROWS, D_MODEL), lambda l, j: (0, 0)),
            pl.BlockSpec((None, D_MODEL, tn), lambda l, j: (l, 0, j)),
            pl.BlockSpec((None, 1, tn), lambda l, j: (l, 0, j)),
        ],
        out_specs=pl.BlockSpec((None, MOD_ROWS, tn), lambda l, j: (l, 0, j)),
        out_shape=jax.ShapeDtypeStruct((depth, MOD_ROWS, 3 * D_MODEL), F32),
        compiler_params=pltpu.CompilerParams(
            dimension_semantics=("arbitrary", "arbitrary"), vmem_limit_bytes=VMEM_LIMIT),
        name="ada_table",
    )(cc, ada_w, ada_b.reshape(depth, 1, 3 * D_MODEL))


def _norm_mod(x, g, scale, shift):
    ms = jnp.mean(x * x, axis=-1, keepdims=True)
    y = x * lax.rsqrt(ms + EPS) * g
    return y * (1.0 + scale) + shift


def _norm_kernel(x_ref, shift_ref, scale_ref, g_ref, o_ref):
    o_ref[...] = _norm_mod(x_ref[...], g_ref[...], scale_ref[...],
                           shift_ref[...]).astype(o_ref.dtype)


def _norm_tokens(x3, mod_row_of_batch, shift, scale, g, tm):
    nb, rows, _ = x3.shape
    mod_map = lambda b, i: (mod_row_of_batch(b), 0, 0)
    return pl.pallas_call(
        _norm_kernel,
        grid=(nb, rows // tm),
        in_specs=[
            pl.BlockSpec((None, tm, D_MODEL), lambda b, i: (b, i, 0)),
            pl.BlockSpec((None, 1, D_MODEL), mod_map),
            pl.BlockSpec((None, 1, D_MODEL), mod_map),
            pl.BlockSpec((1, D_MODEL), lambda b, i: (0, 0)),
        ],
        out_specs=pl.BlockSpec((None, tm, D_MODEL), lambda b, i: (b, i, 0)),
        out_shape=jax.ShapeDtypeStruct((nb, rows, D_MODEL), BF16),
        compiler_params=pltpu.CompilerParams(
            dimension_semantics=("arbitrary", "arbitrary"), vmem_limit_bytes=VMEM_LIMIT),
        name="norm_tokens",
    )(x3, shift, scale, g)


def _inproj_kernel(n_seg, h_ref, *refs):
    w_refs = refs[:n_seg]
    o_refs = refs[n_seg:2 * n_seg]
    wb_refs = refs[2 * n_seg:]

    @pl.when(pl.program_id(1) == 0)
    def _():
        for w_ref, wb_ref in zip(w_refs, wb_refs):
            wb_ref[...] = w_ref[...].astype(BF16)

    for wb_ref, o_ref in zip(wb_refs, o_refs):
        o_ref[...] = jnp.dot(h_ref[...], wb_ref[...],
                             preferred_element_type=F32).astype(o_ref.dtype)


def _inproj(h, w, seg_width, seg_dtypes, tm, tn):
    rows = h.shape[0]
    n_seg = len(seg_dtypes)
    n_tiles = seg_width // tn
    w_specs = [pl.BlockSpec((D_MODEL, tn), functools.partial(
        lambda k, j, i: (0, k * n_tiles + j), k)) for k in range(n_seg)]
    return pl.pallas_call(
        functools.partial(_inproj_kernel, n_seg),
        grid=(n_tiles, rows // tm),
        in_specs=[pl.BlockSpec((tm, D_MODEL), lambda j, i: (i, 0))] + w_specs,
        out_specs=[pl.BlockSpec((tm, tn), lambda j, i: (i, j))] * n_seg,
        out_shape=[jax.ShapeDtypeStruct((rows, seg_width), dt) for dt in seg_dtypes],
        scratch_shapes=[pltpu.VMEM((D_MODEL, tn), BF16)] * n_seg,
        compiler_params=pltpu.CompilerParams(
            dimension_semantics=("arbitrary", "arbitrary"), vmem_limit_bytes=VMEM_LIMIT),
        name="inproj",
    )(h, *([w] * n_seg))


POOL_SLAB = 256
POOL_ROWS = 128


def _pool_kernel(taps, pad, colop_ref, icnt_ref, v_ref, gate_ref, w_ref, sc_ref, o_ref,
                 box_s, pooled_s):
    n = v_ref.shape[0]
    if pad:
        box_s[0:pad, :] = jnp.zeros((pad, POOL_GROUP), F32)
        box_s[pad + n:pad + n + pad, :] = jnp.zeros((pad, POOL_GROUP), F32)
    for gi, (lo, hi) in enumerate(taps):
        lanes = slice(gi * POOL_GROUP, (gi + 1) * POOL_GROUP)
        for s in range(n // POOL_SLAB):
            tok = slice(s * POOL_SLAB, (s + 1) * POOL_SLAB)
            box_s[pad + s * POOL_SLAB:pad + (s + 1) * POOL_SLAB, :] = jnp.dot(
                colop_ref[gi], v_ref[tok, lanes], preferred_element_type=F32)

        def block(i, carry, gi=gi, lo=lo, hi=hi, lanes=lanes):
            r0 = pl.multiple_of(i * POOL_ROWS, POOL_ROWS)
            tok = pl.ds(r0, POOL_ROWS)
            acc = box_s[pl.ds(pad + lo * GRID_W + r0, POOL_ROWS), :]
            for dlt in range(lo + 1, hi + 1):
                acc = acc + box_s[pl.ds(pad + dlt * GRID_W + r0, POOL_ROWS), :]
            pooled = acc * icnt_ref[gi, tok, :] - v_ref[tok, lanes].astype(F32)
            pooled_s[tok, :] = pooled.astype(BF16)
            return carry

        lax.fori_loop(0, n // POOL_ROWS, block, 0, unroll=2)
        mixed = jnp.dot(pooled_s[...], w_ref[gi], preferred_element_type=F32)
        o_ref[:, lanes] = (mixed * sc_ref[:, lanes]
                           * _silu(gate_ref[:, lanes].astype(F32))).astype(o_ref.dtype)


def _pool_mixer(proj, colop, icnt, taps, pool_w, pool_scale):
    nb, n, _ = proj.shape
    ng = len(POOL_WINDOWS)
    pad = max(max(-lo, hi) for lo, hi in taps) * GRID_W
    pad = -(-pad // POOL_ROWS) * POOL_ROWS
    return pl.pallas_call(
        functools.partial(_pool_kernel, taps, pad),
        grid=(nb,),
        in_specs=[
            pl.BlockSpec((ng, POOL_SLAB, POOL_SLAB), lambda b: (0, 0, 0)),
            pl.BlockSpec((ng, n, 1), lambda b: (0, 0, 0)),
            pl.BlockSpec((None, n, POOL_W), lambda b: (b, 0, 0)),
            pl.BlockSpec((None, n, POOL_W), lambda b: (b, 0, 1)),
            pl.BlockSpec((ng, POOL_GROUP, POOL_GROUP), lambda b: (0, 0, 0)),
            pl.BlockSpec((1, POOL_W), lambda b: (0, 0)),
        ],
        out_specs=pl.BlockSpec((None, n, POOL_W), lambda b: (b, 0, 0)),
        out_shape=jax.ShapeDtypeStruct((nb, n, POOL_W), BF16),
        scratch_shapes=[pltpu.VMEM((n + 2 * pad, POOL_GROUP), F32),
                        pltpu.VMEM((n, POOL_GROUP), BF16)],
        compiler_params=pltpu.CompilerParams(
            dimension_semantics=("arbitrary",), vmem_limit_bytes=VMEM_LIMIT),
        name="pool_mixer",
    )(colop, icnt, proj, proj, pool_w, pool_scale)


def _conv_kernel(x_ref, b_ref, c_ref, gate_ref, cw_ref, cb_ref, o_ref):
    n = x_ref.shape[0]
    u = c_ref[...].astype(F32) * x_ref[...].astype(F32)
    row = lax.broadcasted_iota(jnp.int32, u.shape, 0)
    u_prev = jnp.where(row == 0, 0.0, pltpu.roll(u, 1, axis=0))
    u_next = jnp.where(row == n - 1, 0.0, pltpu.roll(u, n - 1, axis=0))
    cw = cw_ref[...]
    conv = u_prev * cw[0:1, :] + u * cw[1:2, :] + u_next * cw[2:3, :] + cb_ref[...]
    o_ref[...] = (b_ref[...].astype(F32) * conv
                  * _silu(gate_ref[...].astype(F32))).astype(o_ref.dtype)


def _conv_mixer(proj, conv_w, conv_b):
    nb, n, _ = proj.shape
    tc = 256
    nblk = CONV_W // tc
    base = 2 * POOL_W // tc

    def seg(k):
        return pl.BlockSpec((None, n, tc), lambda b, j: (b, 0, base + k * nblk + j))

    return pl.pallas_call(
        _conv_kernel,
        grid=(nb, nblk),
        in_specs=[
            seg(0), seg(1), seg(2), seg(3),
            pl.BlockSpec((3, tc), lambda b, j: (0, j)),
            pl.BlockSpec((1, tc), lambda b, j: (0, j)),
        ],
        out_specs=pl.BlockSpec((None, n, tc), lambda b, j: (b, 0, j)),
        out_shape=jax.ShapeDtypeStruct((nb, n, CONV_W), BF16),
        compiler_params=pltpu.CompilerParams(
            dimension_semantics=("arbitrary", "arbitrary"), vmem_limit_bytes=VMEM_LIMIT),
        name="conv_mixer",
    )(proj, proj, proj, proj, conv_w, conv_b)


def _outproj_kernel(n_in, emit_next, *refs):
    a_refs = refs[:n_in]
    w_refs = refs[n_in:2 * n_in]
    x_ref, gate_ref, pg_ref = refs[2 * n_in:2 * n_in + 3]
    rest = refs[2 * n_in + 3:]
    if emit_next:
        shift_ref, scale_ref, g_ref, o_ref, h_ref = rest[:5]
        wb_refs = rest[5:]
    else:
        o_ref = rest[0]
        wb_refs = rest[1:]

    @pl.when((pl.program_id(0) == 0) & (pl.program_id(1) == 0))
    def _():
        for w_ref, wb_ref in zip(w_refs, wb_refs):
            wb_ref[...] = w_ref[...].astype(BF16)

    y = jnp.dot(a_refs[0][...], wb_refs[0][...], preferred_element_type=F32)
    for a_ref, wb_ref in zip(a_refs[1:], wb_refs[1:]):
        y = y + jnp.dot(a_ref[...], wb_ref[...], preferred_element_type=F32)
    ms = jnp.mean(y * y, axis=-1, keepdims=True)
    r = y * lax.rsqrt(ms + EPS) * pg_ref[...]
    x_new = x_ref[...] + gate_ref[...] * r
    o_ref[...] = x_new
    if emit_next:
        h_ref[...] = _norm_mod(x_new, g_ref[...], scale_ref[...],
                               shift_ref[...]).astype(h_ref.dtype)


def _outproj(acts, w, x3, mod_row_of_batch, gate, pg, tm, next_norm=None):
    nb, rows, _ = x3.shape
    n_in = len(acts)
    k_in = acts[0].shape[-1]
    assert all(a.shape[-1] == k_in for a in acts) and n_in * k_in == w.shape[0]
    emit_next = next_norm is not None
    mod_spec = pl.BlockSpec((None, 1, D_MODEL), lambda b, i: (mod_row_of_batch(b), 0, 0))
    vec_spec = pl.BlockSpec((1, D_MODEL), lambda b, i: (0, 0))
    tok_spec = pl.BlockSpec((None, tm, D_MODEL), lambda b, i: (b, i, 0))
    in_specs = [pl.BlockSpec((None, tm, k_in), lambda b, i: (b, i, 0))] * n_in
    in_specs += [pl.BlockSpec((k_in, D_MODEL), functools.partial(lambda k, b, i: (k, 0), k),
                              pipeline_mode=pl.Buffered(1)) for k in range(n_in)]
    in_specs += [tok_spec, mod_spec, vec_spec]
    operands = [*acts, *([w] * n_in), x3, gate, pg]
    out_specs, out_shape = tok_spec, jax.ShapeDtypeStruct((nb, rows, D_MODEL), F32)
    if emit_next:
        in_specs += [mod_spec, mod_spec, vec_spec]
        operands += list(next_norm)
        out_specs = [tok_spec, tok_spec]
        out_shape = [out_shape, jax.ShapeDtypeStruct((nb, rows, D_MODEL), BF16)]
    return pl.pallas_call(
        functools.partial(_outproj_kernel, n_in, emit_next),
        grid=(nb, rows // tm),
        in_specs=in_specs,
        out_specs=out_specs,
        out_shape=out_shape,
        scratch_shapes=[pltpu.VMEM((k_in, D_MODEL), BF16)] * n_in,
        compiler_params=pltpu.CompilerParams(
            dimension_semantics=("arbitrary", "arbitrary"), vmem_limit_bytes=VMEM_LIMIT),
        name="outproj",
    )(*operands)


_NT = (((1,), (1,)), ((), ()))
_TN = (((0,), (0,)), ((), ()))
SUBLANES = 8
SCAN_HEADS = 2
SCAN_CHUNK = 2 * CHUNK


def _chunk_cumsum(x, reverse, in_ref, out_ref):
    c, w = x.shape
    nblk = c // SUBLANES
    order = range(nblk - 1, -1, -1) if reverse else range(nblk)
    in_ref[...] = x
    pref = [None] * nblk
    acc = None
    for j in order:
        blk = in_ref[pl.ds(j, SUBLANES, stride=nblk), :]
        acc = blk if acc is None else acc + blk
        pref[j] = acc
    total = acc
    sub = lax.broadcasted_iota(jnp.int32, (SUBLANES, w), 0)
    incl = total
    for s in (1, 2, 4):
        if reverse:
            incl = incl + jnp.where(sub < SUBLANES - s,
                                    pltpu.roll(incl, SUBLANES - s, axis=0), 0.0)
        else:
            incl = incl + jnp.where(sub >= s, pltpu.roll(incl, s, axis=0), 0.0)
    before = incl - total
    for j in range(nblk):
        out_ref[pl.ds(j, SUBLANES, stride=nblk), :] = pref[j] + before
    edge = incl[0:1, :] if reverse else incl[SUBLANES - 1:SUBLANES, :]
    return out_ref[...], jnp.broadcast_to(edge, (SUBLANES, w))


def _gate_decay(zh, lb, reverse, cs_ref):
    bt = (0.5 * (1.0 - lb)) * jnp.tanh(zh)
    k = 0.5 * (1.0 - lb) - bt
    lf = jnp.log(0.5 * (1.0 + lb) + bt)
    lo, hi = lf[:CHUNK], lf[CHUNK:]
    first, second = (hi, lo) if reverse else (lo, hi)
    away, t_first = _chunk_cumsum(first, not reverse, cs_ref.at[0], cs_ref.at[1])
    toward, t_second = _chunk_cumsum(second, reverse, cs_ref.at[2], cs_ref.at[3])
    a_first = first - away
    a = jnp.concatenate([toward, a_first] if reverse else [a_first, toward], axis=0)
    k_inv = (k * jnp.exp(-a)).astype(BF16)
    d_first = jnp.exp(t_first)
    d_second = jnp.exp(t_second)
    k_dec = k_inv * d_second[0:1, :].astype(BF16)
    return a, k_inv, k_dec, d_first, d_first * d_second


def _hgrn_inproj_kernel(with_queries, h_ref, lbf_ref, lbb_ref, *refs):
    n_w = 5 if with_queries else 3
    w_refs, rest = refs[:n_w], refs[n_w:]
    if with_queries:
        per_dir = (rest[0:5], rest[5:10])
        v_ref, g_ref = rest[10:12]
        rest = rest[12:]
    else:
        per_dir = (rest[0:2], rest[2:4])
        v_ref = rest[4]
        rest = rest[5:]
    wb_refs, cs_s = rest[:n_w], rest[n_w]

    @pl.when(pl.program_id(1) == 0)
    def _():
        for w_ref, wb_ref in zip(w_refs, wb_refs):
            wb_ref[...] = w_ref[...].astype(BF16)

    h = h_ref[...]

    def proj(k):
        return jnp.dot(h, wb_refs[k][...], preferred_element_type=F32)

    tm, w = v_ref.shape
    n_chunks = tm // SCAN_CHUNK
    n_heads = w // HG_DK
    zhs = [proj(0) * 0.5, proj(1) * 0.5]
    if with_queries:
        q = proj(3).astype(BF16)
    for d, (lb_ref, reverse) in enumerate(((lbf_ref, False), (lbb_ref, True))):
        zh = zhs[d]
        lb = lb_ref[...]
        for c in range(n_chunks):
            rows = slice(c * SCAN_CHUNK, (c + 1) * SCAN_CHUNK)
            drows = slice(c * SUBLANES, (c + 1) * SUBLANES)
            for hd in range(n_heads):
                lanes = slice(hd * HG_DK, (hd + 1) * HG_DK)
                slot = (d * n_chunks + c) * n_heads + hd
                a, k_inv, k_dec, d_first, decay = _gate_decay(
                    zh[rows, lanes], lb[:, lanes], reverse, cs_s.at[slot])
                if with_queries:
                    qd_ref, ki_ref, kd_ref, dec_ref, mid_ref = per_dir[d]
                    qd_ref[rows, lanes] = q[rows, lanes] * jnp.exp(a).astype(BF16)
                    ki_ref[rows, lanes] = k_inv
                    mid_ref[drows, lanes] = d_first
                else:
                    kd_ref, dec_ref = per_dir[d]
                kd_ref[rows, lanes] = k_dec
                dec_ref[drows, lanes] = decay
    v_ref[...] = proj(2).astype(BF16)
    if with_queries:
        g_ref[...] = (proj(4) * 0.5).astype(BF16)


def _hgrn_inproj(h, w, lb_f, lb_b, with_queries, tm):
    rows = h.shape[0]
    tn = SCAN_HEADS * HG_DK
    groups = HG_K // tn
    n_w = 5 if with_queries else 3
    w_specs = [pl.BlockSpec((D_MODEL, tn), functools.partial(
        lambda k, j, i: (0, k * groups + j), k)) for k in range(n_w)]
    lb_spec = pl.BlockSpec((1, tn), lambda j, i: (0, j))
    dec_rows = tm // SCAN_CHUNK * SUBLANES
    tok = (pl.BlockSpec((None, tm, tn), lambda j, i: (j, i, 0)),
           jax.ShapeDtypeStruct((groups, rows, tn), BF16))
    dec = (pl.BlockSpec((None, dec_rows, tn), lambda j, i: (j, i, 0)),
           jax.ShapeDtypeStruct((groups, rows // SCAN_CHUNK * SUBLANES, tn), F32))
    one_dir = [tok, tok, tok, dec, dec] if with_queries else [tok, dec]
    outs = one_dir + one_dir + ([tok, tok] if with_queries else [tok])
    n_slots = 2 * (tm // SCAN_CHUNK) * SCAN_HEADS
    return pl.pallas_call(
        functools.partial(_hgrn_inproj_kernel, with_queries),
        grid=(groups, rows // tm),
        in_specs=[pl.BlockSpec((tm, D_MODEL), lambda j, i: (i, 0)), lb_spec, lb_spec] + w_specs,
        out_specs=[o[0] for o in outs],
        out_shape=[o[1] for o in outs],
        scratch_shapes=[pltpu.VMEM((D_MODEL, tn), BF16)] * n_w
        + [pltpu.VMEM((n_slots, 4, CHUNK, HG_DK), F32)],
        compiler_params=pltpu.CompilerParams(
            dimension_semantics=("arbitrary", "arbitrary"), vmem_limit_bytes=VMEM_LIMIT),
        name="hgrn_inproj",
    )(h, lb_f, lb_b, *([w] * n_w))


def _scan_kernel(qdf_ref, kif_ref, kdf_ref, decf_ref, midf_ref,
                 qdb_ref, kib_ref, kdb_ref, decb_ref, midb_ref, v_ref, g_ref,
                 ckdf_ref, cdecf_ref, ckdb_ref, cdecb_ref, cv_ref, og_ref, o_ref,
                 sc_s, ds_s, st_s, of_s, ob_s):
    c = SCAN_CHUNK
    n_lat = v_ref.shape[0] // c
    n_ctx = cv_ref.shape[0] // c
    n_heads = v_ref.shape[1] // HG_DK
    row = lax.broadcasted_iota(jnp.int32, (c, c), 0)
    col = lax.broadcasted_iota(jnp.int32, (c, c), 1)
    dirs = ((qdf_ref, kif_ref, kdf_ref, decf_ref, midf_ref, ckdf_ref, cdecf_ref,
             False, col <= row, of_s),
            (qdb_ref, kib_ref, kdb_ref, decb_ref, midb_ref, ckdb_ref, cdecb_ref,
             True, col >= row, ob_s))

    def lanes(h):
        return slice(h * HG_DK, (h + 1) * HG_DK)

    def rows(chunk):
        return slice(chunk * c, (chunk + 1) * c)

    def dec_rows(chunk):
        return slice(chunk * SUBLANES, (chunk + 1) * SUBLANES)

    def state_rows(chunk):
        return slice(chunk * HG_DV, (chunk + 1) * HG_DV)

    def local(d, chunk):
        qd_ref, ki_ref, kd_ref = dirs[d][:3]
        mask = dirs[d][8]
        sl = rows(chunk)
        for h in range(n_heads):
            sc = lax.dot_general(qd_ref[sl, lanes(h)], ki_ref[sl, lanes(h)], _NT,
                                 preferred_element_type=F32)
            sc_s[d, h, sl, :] = jnp.where(mask, sc, 0.0).astype(BF16)
            ds_s[d, h, state_rows(chunk), :] = lax.dot_general(
                v_ref[sl, lanes(h)], kd_ref[sl, lanes(h)], _TN, preferred_element_type=F32)

    def step(d, chunk):
        qd_ref, dec_ref, mid_ref, out_s = dirs[d][0], dirs[d][3], dirs[d][4], dirs[d][9]
        sl = rows(chunk)
        decay = dec_ref[dec_rows(chunk), :]
        d_first = mid_ref[dec_rows(chunk), :]
        for h in range(n_heads):
            st_mid = (st_s[d, h] * d_first[0:1, lanes(h)]).astype(BF16)
            inter = lax.dot_general(qd_ref[sl, lanes(h)], st_mid, _NT,
                                    preferred_element_type=F32)
            out_s[sl, lanes(h)] = inter + jnp.dot(sc_s[d, h, sl, :], v_ref[sl, lanes(h)],
                                                  preferred_element_type=F32)
            st_s[d, h] = (st_s[d, h] * decay[0:1, lanes(h)]
                          + ds_s[d, h, state_rows(chunk), :])

    st_s[...] = jnp.zeros_like(st_s)
    for j in range(n_ctx):
        for d in range(2):
            ckd_ref, cdec_ref, reverse = dirs[d][5:8]
            cj = n_ctx - 1 - j if reverse else j
            decay = cdec_ref[cj * SUBLANES:(cj + 1) * SUBLANES, :]
            for h in range(n_heads):
                ds_t = lax.dot_general(cv_ref[cj * c:(cj + 1) * c, lanes(h)],
                                       ckd_ref[cj * c:(cj + 1) * c, lanes(h)], _TN,
                                       preferred_element_type=F32)
                st_s[d, h] = st_s[d, h] * decay[0:1, lanes(h)] + ds_t

    def chunk_of(d, idx):
        return n_lat - 1 - idx if dirs[d][7] else idx

    og = og_ref[...]
    ones = jnp.ones((HG_DV, HG_DV), BF16)

    def readout(chunk):
        sl = rows(chunk)
        for h in range(n_heads):
            o = of_s[sl, lanes(h)] + ob_s[sl, lanes(h)]
            ss = jnp.dot((o * o).astype(BF16), ones, preferred_element_type=F32)
            o = o * lax.rsqrt(ss * (1.0 / HG_DV) + EPS)
            gh = g_ref[sl, lanes(h)].astype(F32)
            o_ref[sl, lanes(h)] = (o * og[:, lanes(h)]
                                   * (gh + gh * jnp.tanh(gh))).astype(o_ref.dtype)

    assert n_lat % 2 == 0
    for d in range(2):
        local(d, chunk_of(d, 0))
    for idx in range(n_lat):
        for d in range(2):
            step(d, chunk_of(d, idx))
        if idx + 1 < n_lat:
            for d in range(2):
                local(d, chunk_of(d, idx + 1))
        if 2 * idx >= n_lat:
            readout(idx)
            readout(n_lat - 1 - idx)


def _hgrn2_scan(lat_ops, ctx_ops, onorm_g, nb):
    groups, rows, w = lat_ops[-1].shape
    n = rows // nb

    def blocked(a):
        per_batch = a.shape[1] // nb
        return (a.reshape(groups, nb, per_batch, w),
                pl.BlockSpec((None, None, per_batch, w), lambda b, h: (h, b, 0, 0)))

    ops, specs = zip(*[blocked(a) for a in (*lat_ops, *ctx_ops)])
    vec = pl.BlockSpec((1, w), lambda b, h: (0, h))
    return pl.pallas_call(
        _scan_kernel,
        grid=(nb, groups),
        in_specs=list(specs) + [vec],
        out_specs=pl.BlockSpec((None, n, w), lambda b, h: (b, 0, h)),
        out_shape=jax.ShapeDtypeStruct((nb, n, HG_V), BF16),
        scratch_shapes=[pltpu.VMEM((2, SCAN_HEADS, n, SCAN_CHUNK), BF16),
                        pltpu.VMEM((2, SCAN_HEADS, n // SCAN_CHUNK * HG_DV, HG_DK), F32),
                        pltpu.VMEM((2, SCAN_HEADS, HG_DV, HG_DK), F32),
                        pltpu.VMEM((n, w), F32),
                        pltpu.VMEM((n, w), F32)],
        compiler_params=pltpu.CompilerParams(
            dimension_semantics=("arbitrary", "arbitrary"), vmem_limit_bytes=VMEM_LIMIT),
        name="hgrn2_scan",
    )(*ops, onorm_g)


def _window(n, w):
    t = np.arange(n)
    lo = np.maximum(t - w // 2, 0)
    hi = np.minimum(t + w // 2 - 1, n - 1)
    s = np.arange(n)
    inside = (s[None, :] >= lo[:, None]) & (s[None, :] <= hi[:, None])
    return inside, (hi - lo + 1)


def _pool_operators(n, on_grid):
    mats, icnts, taps = [], [], []
    for w in POOL_WINDOWS:
        if on_grid:
            in_c, cnt_c = _window(GRID_W, w)
            _, cnt_r = _window(n // GRID_W, w)
            m = np.kron(np.eye(POOL_SLAB // GRID_W, dtype=bool), in_c)
            cnt = (cnt_r[:, None] * cnt_c[None, :]).reshape(n)
            taps.append((-(w // 2), w // 2 - 1))
        else:
            assert n == POOL_SLAB
            m, cnt = _window(n, w)
            taps.append((0, 0))
        mats.append(m)
        icnts.append(1.0 / cnt.astype(np.float64))
    colop = jnp.asarray(np.stack(mats).astype(np.float32), dtype=BF16)
    icnt = jnp.asarray(np.stack(icnts).astype(np.float32)[..., None])
    return colop, icnt, tuple(taps)


def kernel(x, c, ctx, c_ctx, ada_w, ada_b, pre_g, post_g, ev_w_in, ev_pool_w, ev_pool_scale,
           ev_conv_w, ev_conv_b, ev_w_out, od_w_in, od_onorm_g, od_w_out, lb_logits):
    nb, n, d = x.shape
    nc = ctx.shape[1]
    lat_row = lambda b: b
    ctx_row = lambda b: CTX_ROW

    lb_table = jnp.cumsum(jax.nn.softmax(lb_logits.astype(F32), axis=1), axis=1)

    cc = jnp.zeros((MOD_ROWS, d), F32).at[:nb].set(c).at[CTX_ROW].set(c_ctx)
    mod = _ada_table(cc, ada_w, ada_b)
    mod = mod.reshape(2, MOD_ROWS, 3, 1, d)
    shift = [mod[l, :, 0] for l in range(2)]
    scale = [mod[l, :, 1] for l in range(2)]
    gate = [mod[l, :, 2] for l in range(2)]

    ctx_flat = ctx.reshape(1, nb * nc, d)

    pool_w = ev_pool_w[0].astype(BF16)
    pool_scale = ev_pool_scale[0].reshape(1, POOL_W)
    conv_w = ev_conv_w[0]
    conv_b = ev_conv_b[0].reshape(1, CONV_W)
    pre0 = pre_g[0].reshape(1, d)
    post0 = post_g[0].reshape(1, d)

    pre1 = pre_g[1].reshape(1, d)
    post1 = post_g[1].reshape(1, d)
    norm1 = (shift[1], scale[1], pre1)

    def even_mixer(tokens3, mod_row, on_grid):
        b3, n3, _ = tokens3.shape
        h = _norm_tokens(tokens3, mod_row, shift[0], scale[0], pre0, TM_NORM)
        proj, = _inproj(h.reshape(b3 * n3, d), ev_w_in[0], EVEN_IN, [BF16], TM_IN, 1024)
        n_tok = n if on_grid else nc
        proj = proj.reshape(nb, n_tok, EVEN_IN)
        colop, icnt, taps = _pool_operators(n_tok, on_grid)
        a_out = _pool_mixer(proj, colop, icnt, taps, pool_w, pool_scale)
        b_out = _conv_mixer(proj, conv_w, conv_b)
        acts = [a_out.reshape(b3, n3, POOL_W), b_out.reshape(b3, n3, CONV_W)]
        return _outproj(acts, ev_w_out[0], tokens3, mod_row, gate[0], post0, TM_OUT, norm1)

    x1, h1 = even_mixer(x, lat_row, True)
    _, hc1 = even_mixer(ctx_flat, ctx_row, False)

    lb_f = lb_table[0, 1].reshape(1, HG_K)
    lb_b = lb_table[1, 1].reshape(1, HG_K)
    lat_ops = _hgrn_inproj(h1.reshape(nb * n, d), od_w_in[0], lb_f, lb_b, True, TM_IN)
    ctx_ops = _hgrn_inproj(hc1.reshape(nb * nc, d), od_w_in[0], lb_f, lb_b, False, TM_IN)
    o = _hgrn2_scan(lat_ops, ctx_ops, od_onorm_g[0].reshape(1, HG_V), nb)
    return _outproj([o], od_w_out[0], x1, lat_row, gate[1], post1, TM_OUT)
```

```python
import functools

import numpy as np
import jax
import jax.numpy as jnp
from jax import lax
from jax.experimental import pallas as pl
from jax.experimental.pallas import tpu as pltpu

F32 = jnp.float32
BF16 = jnp.bfloat16

D_MODEL = 2048
BATCH = 4
SEQ = 2048
CTX_LEN = 256
GRID_W = 64
EPS = 1e-6

POOL_WINDOWS = (2, 4, 8, 16)
POOL_W = D_MODEL // 2
POOL_GROUP = POOL_W // len(POOL_WINDOWS)
CONV_W = D_MODEL // 2
EVEN_IN = 2 * POOL_W + 4 * CONV_W

HG_DK = 128
HG_HEADS = D_MODEL // HG_DK
HG_DV = D_MODEL // HG_HEADS
HG_K = HG_HEADS * HG_DK
HG_V = HG_HEADS * HG_DV
ODD_IN = 3 * HG_K + 2 * HG_V
CHUNK = 64

MOD_ROWS = 8
CTX_ROW = BATCH
VMEM_LIMIT = 56 * 1024 * 1024
TM_NORM = 1024
TM_IN = 1024
TM_OUT = 512


def _silu(v):
    hv = 0.5 * v
    return hv + hv * jnp.tanh(hv)


def _ada_kernel(cc_ref, w_ref, b_ref, o_ref):
    s = _silu(cc_ref[...])
    o_ref[...] = jnp.dot(s, w_ref[...], preferred_element_type=F32) + b_ref[...]


def _ada_table(cc, ada_w, ada_b):
    depth = ada_w.shape[0]
    tn = 1024
    return pl.pallas_call(
        _ada_kernel,
        grid=(depth, 3 * D_MODEL // tn),
        in_specs=[
            pl.BlockSpec((MOD_ROWS, D_MODEL), lambda l, j: (0, 0)),
            pl.BlockSpec((None, D_MODEL, tn), lambda l, j: (l, 0, j)),
            pl.BlockSpec((None, 1, tn), lambda l, j: (l, 0, j)),
        ],
        out_specs=pl.BlockSpec((None, MOD_ROWS, tn), lambda l, j: (l, 0, j)),
        out_shape=jax.ShapeDtypeStruct((depth, MOD_ROWS, 3 * D_MODEL), F32),
        compiler_params=pltpu.CompilerParams(
            dimension_semantics=("arbitrary", "arbitrary"), vmem_limit_bytes=VMEM_LIMIT),
        name="ada_table",
    )(cc, ada_w, ada_b.reshape(depth, 1, 3 * D_MODEL))


def _norm_mod(x, g, scale, shift):
    ms = jnp.mean(x * x, axis=-1, keepdims=True)
    y = x * lax.rsqrt(ms + EPS) * g
    return y * (1.0 + scale) + shift


def _norm_kernel(x_ref, shift_ref, scale_ref, g_ref, o_ref):
    o_ref[...] = _norm_mod(x_ref[...], g_ref[...], scale_ref[...],
                           shift_ref[...]).astype(o_ref.dtype)


def _norm_tokens(x3, mod_row_of_batch, shift, scale, g, tm):
    nb, rows, _ = x3.shape
    mod_map = lambda b, i: (mod_row_of_batch(b), 0, 0)
    return pl.pallas_call(
        _norm_kernel,
        grid=(nb, rows // tm),
        in_specs=[
            pl.BlockSpec((None, tm, D_MODEL), lambda b, i: (b, i, 0)),
            pl.BlockSpec((None, 1, D_MODEL), mod_map),
            pl.BlockSpec((None, 1, D_MODEL), mod_map),
            pl.BlockSpec((1, D_MODEL), lambda b, i: (0, 0)),
        ],
        out_specs=pl.BlockSpec((None, tm, D_MODEL), lambda b, i: (b, i, 0)),
        out_shape=jax.ShapeDtypeStruct((nb, rows, D_MODEL), BF16),
        compiler_params=pltpu.CompilerParams(
            dimension_semantics=("arbitrary", "arbitrary"), vmem_limit_bytes=VMEM_LIMIT),
        name="norm_tokens",
    )(x3, shift, scale, g)


def _inproj_kernel(n_seg, h_ref, *refs):
    w_refs = refs[:n_seg]
    o_refs = refs[n_seg:2 * n_seg]
    wb_refs = refs[2 * n_seg:]

    @pl.when(pl.program_id(1) == 0)
    def _():
        for w_ref, wb_ref in zip(w_refs, wb_refs):
            wb_ref[...] = w_ref[...].astype(BF16)

    for wb_ref, o_ref in zip(wb_refs, o_refs):
        o_ref[...] = jnp.dot(h_ref[...], wb_ref[...],
                             preferred_element_type=F32).astype(o_ref.dtype)


def _inproj(h, w, seg_width, seg_dtypes, tm, tn):
    rows = h.shape[0]
    n_seg = len(seg_dtypes)
    n_tiles = seg_width // tn
    w_specs = [pl.BlockSpec((D_MODEL, tn), functools.partial(
        lambda k, j, i: (0, k * n_tiles + j), k)) for k in range(n_seg)]
    return pl.pallas_call(
        functools.partial(_inproj_kernel, n_seg),
        grid=(n_tiles, rows // tm),
        in_specs=[pl.BlockSpec((tm, D_MODEL), lambda j, i: (i, 0))] + w_specs,
        out_specs=[pl.BlockSpec((tm, tn), lambda j, i: (i, j))] * n_seg,
        out_shape=[jax.ShapeDtypeStruct((rows, seg_width), dt) for dt in seg_dtypes],
        scratch_shapes=[pltpu.VMEM((D_MODEL, tn), BF16)] * n_seg,
        compiler_params=pltpu.CompilerParams(
            dimension_semantics=("arbitrary", "arbitrary"), vmem_limit_bytes=VMEM_LIMIT),
        name="inproj",
    )(h, *([w] * n_seg))


POOL_SLAB = 256
POOL_ROWS = 128


def _pool_kernel(taps, pad, colop_ref, icnt_ref, v_ref, gate_ref, w_ref, sc_ref, o_ref,
                 box_s, pooled_s):
    n = v_ref.shape[0]
    if pad:
        box_s[0:pad, :] = jnp.zeros((pad, POOL_GROUP), F32)
        box_s[pad + n:pad + n + pad, :] = jnp.zeros((pad, POOL_GROUP), F32)
    for gi, (lo, hi) in enumerate(taps):
        lanes = slice(gi * POOL_GROUP, (gi + 1) * POOL_GROUP)
        for s in range(n // POOL_SLAB):
            tok = slice(s * POOL_SLAB, (s + 1) * POOL_SLAB)
            box_s[pad + s * POOL_SLAB:pad + (s + 1) * POOL_SLAB, :] = jnp.dot(
                colop_ref[gi], v_ref[tok, lanes], preferred_element_type=F32)

        def block(i, carry, gi=gi, lo=lo, hi=hi, lanes=lanes):
            r0 = pl.multiple_of(i * POOL_ROWS, POOL_ROWS)
            tok = pl.ds(r0, POOL_ROWS)
            acc = box_s[pl.ds(pad + lo * GRID_W + r0, POOL_ROWS), :]
            for dlt in range(lo + 1, hi + 1):
                acc = acc + box_s[pl.ds(pad + dlt * GRID_W + r0, POOL_ROWS), :]
            pooled = acc * icnt_ref[gi, tok, :] - v_ref[tok, lanes].astype(F32)
            pooled_s[tok, :] = pooled.astype(BF16)
            return carry

        lax.fori_loop(0, n // POOL_ROWS, block, 0, unroll=2)
        mixed = jnp.dot(pooled_s[...], w_ref[gi], preferred_element_type=F32)
        o_ref[:, lanes] = (mixed * sc_ref[:, lanes]
                           * _silu(gate_ref[:, lanes].astype(F32))).astype(o_ref.dtype)


def _pool_mixer(proj, colop, icnt, taps, pool_w, pool_scale):
    nb, n, _ = proj.shape
    ng = len(POOL_WINDOWS)
    pad = max(max(-lo, hi) for lo, hi in taps) * GRID_W
    pad = -(-pad // POOL_ROWS) * POOL_ROWS
    return pl.pallas_call(
        functools.partial(_pool_kernel, taps, pad),
        grid=(nb,),
        in_specs=[
            pl.BlockSpec((ng, POOL_SLAB, POOL_SLAB), lambda b: (0, 0, 0)),
            pl.BlockSpec((ng, n, 1), lambda b: (0, 0, 0)),
            pl.BlockSpec((None, n, POOL_W), lambda b: (b, 0, 0)),
            pl.BlockSpec((None, n, POOL_W), lambda b: (b, 0, 1)),
            pl.BlockSpec((ng, POOL_GROUP, POOL_GROUP), lambda b: (0, 0, 0)),
            pl.BlockSpec((1, POOL_W), lambda b: (0, 0)),
        ],
        out_specs=pl.BlockSpec((None, n, POOL_W), lambda b: (b, 0, 0)),
        out_shape=jax.ShapeDtypeStruct((nb, n, POOL_W), BF16),
        scratch_shapes=[pltpu.VMEM((n + 2 * pad, POOL_GROUP), F32),
                        pltpu.VMEM((n, POOL_GROUP), BF16)],
        compiler_params=pltpu.CompilerParams(
            dimension_semantics=("arbitrary",), vmem_limit_bytes=VMEM_LIMIT),
        name="pool_mixer",
    )(colop, icnt, proj, proj, pool_w, pool_scale)


def _conv_kernel(x_ref, b_ref, c_ref, gate_ref, cw_ref, cb_ref, o_ref):
    n = x_ref.shape[0]
    u = c_ref[...].astype(F32) * x_ref[...].astype(F32)
    row = lax.broadcasted_iota(jnp.int32, u.shape, 0)
    u_prev = jnp.where(row == 0, 0.0, pltpu.roll(u, 1, axis=0))
    u_next = jnp.where(row == n - 1, 0.0, pltpu.roll(u, n - 1, axis=0))
    cw = cw_ref[...]
    conv = u_prev * cw[0:1, :] + u * cw[1:2, :] + u_next * cw[2:3, :] + cb_ref[...]
    o_ref[...] = (b_ref[...].astype(F32) * conv
                  * _silu(gate_ref[...].astype(F32))).astype(o_ref.dtype)


def _conv_mixer(proj, conv_w, conv_b):
    nb, n, _ = proj.shape
    tc = 256
    nblk = CONV_W // tc
    base = 2 * POOL_W // tc

    def seg(k):
        return pl.BlockSpec((None, n, tc), lambda b, j: (b, 0, base + k * nblk + j))

    return pl.pallas_call(
        _conv_kernel,
        grid=(nb, nblk),
        in_specs=[
            seg(0), seg(1), seg(2), seg(3),
            pl.BlockSpec((3, tc), lambda b, j: (0, j)),
            pl.BlockSpec((1, tc), lambda b, j: (0, j)),
        ],
        out_specs=pl.BlockSpec((None, n, tc), lambda b, j: (b, 0, j)),
        out_shape=jax.ShapeDtypeStruct((nb, n, CONV_W), BF16),
        compiler_params=pltpu.CompilerParams(
            dimension_semantics=("arbitrary", "arbitrary"), vmem_limit_bytes=VMEM_LIMIT),
        name="conv_mixer",
    )(proj, proj, proj, proj, conv_w, conv_b)


def _outproj_kernel(n_in, emit_next, *refs):
    a_refs = refs[:n_in]
    w_refs = refs[n_in:2 * n_in]
    x_ref, gate_ref, pg_ref = refs[2 * n_in:2 * n_in + 3]
    rest = refs[2 * n_in + 3:]
    if emit_next:
        shift_ref, scale_ref, g_ref, o_ref, h_ref = rest[:5]
        wb_refs = rest[5:]
    else:
        o_ref = rest[0]
        wb_refs = rest[1:]

    @pl.when((pl.program_id(0) == 0) & (pl.program_id(1) == 0))
    def _():
        for w_ref, wb_ref in zip(w_refs, wb_refs):
            wb_ref[...] = w_ref[...].astype(BF16)

    y = jnp.dot(a_refs[0][...], wb_refs[0][...], preferred_element_type=F32)
    for a_ref, wb_ref in zip(a_refs[1:], wb_refs[1:]):
        y = y + jnp.dot(a_ref[...], wb_ref[...], preferred_element_type=F32)
    ms = jnp.mean(y * y, axis=-1, keepdims=True)
    r = y * lax.rsqrt(ms + EPS) * pg_ref[...]
    x_new = x_ref[...] + gate_ref[...] * r
    o_ref[...] = x_new
    if emit_next:
        h_ref[...] = _norm_mod(x_new, g_ref[...], scale_ref[...],
                               shift_ref[...]).astype(h_ref.dtype)


def _outproj(acts, w, x3, mod_row_of_batch, gate, pg, tm, next_norm=None):
    nb, rows, _ = x3.shape
    n_in = len(acts)
    k_in = acts[0].shape[-1]
    assert all(a.shape[-1] == k_in for a in acts) and n_in * k_in == w.shape[0]
    emit_next = next_norm is not None
    mod_spec = pl.BlockSpec((None, 1, D_MODEL), lambda b, i: (mod_row_of_batch(b), 0, 0))
    vec_spec = pl.BlockSpec((1, D_MODEL), lambda b, i: (0, 0))
    tok_spec = pl.BlockSpec((None, tm, D_MODEL), lambda b, i: (b, i, 0))
    in_specs = [pl.BlockSpec((None, tm, k_in), lambda b, i: (b, i, 0))] * n_in
    in_specs += [pl.BlockSpec((k_in, D_MODEL), functools.partial(lambda k, b, i: (k, 0), k),
                              pipeline_mode=pl.Buffered(1)) for k in range(n_in)]
    in_specs += [tok_spec, mod_spec, vec_spec]
    operands = [*acts, *([w] * n_in), x3, gate, pg]
    out_specs, out_shape = tok_spec, jax.ShapeDtypeStruct((nb, rows, D_MODEL), F32)
    if emit_next:
        in_specs += [mod_spec, mod_spec, vec_spec]
        operands += list(next_norm)
        out_specs = [tok_spec, tok_spec]
        out_shape = [out_shape, jax.ShapeDtypeStruct((nb, rows, D_MODEL), BF16)]
    return pl.pallas_call(
        functools.partial(_outproj_kernel, n_in, emit_next),
        grid=(nb, rows // tm),
        in_specs=in_specs,
        out_specs=out_specs,
        out_shape=out_shape,
        scratch_shapes=[pltpu.VMEM((k_in, D_MODEL), BF16)] * n_in,
        compiler_params=pltpu.CompilerParams(
            dimension_semantics=("arbitrary", "arbitrary"), vmem_limit_bytes=VMEM_LIMIT),
        name="outproj",
    )(*operands)


_NT = (((1,), (1,)), ((), ()))
_TN = (((0,), (0,)), ((), ()))
SUBLANES = 8
SCAN_HEADS = 2
SCAN_CHUNK = 2 * CHUNK


def _chunk_cumsum(x, reverse, in_ref, out_ref):
    c, w = x.shape
    nblk = c // SUBLANES
    order = range(nblk - 1, -1, -1) if reverse else range(nblk)
    in_ref[...] = x
    pref = [None] * nblk
    acc = None
    for j in order:
        blk = in_ref[pl.ds(j, SUBLANES, stride=nblk), :]
        acc = blk if acc is None else acc + blk
        pref[j] = acc
    total = acc
    sub = lax.broadcasted_iota(jnp.int32, (SUBLANES, w), 0)
    incl = total
    for s in (1, 2, 4):
        if reverse:
            incl = incl + jnp.where(sub < SUBLANES - s,
                                    pltpu.roll(incl, SUBLANES - s, axis=0), 0.0)
        else:
            incl = incl + jnp.where(sub >= s, pltpu.roll(incl, s, axis=0), 0.0)
    before = incl - total
    for j in range(nblk):
        out_ref[pl.ds(j, SUBLANES, stride=nblk), :] = pref[j] + before
    edge = incl[0:1, :] if reverse else incl[SUBLANES - 1:SUBLANES, :]
    return out_ref[...], jnp.broadcast_to(edge, (SUBLANES, w))


def _gate_decay(zh, lb, reverse, cs_ref):
    bt = (0.5 * (1.0 - lb)) * jnp.tanh(zh)
    k = 0.5 * (1.0 - lb) - bt
    lf = jnp.log(0.5 * (1.0 + lb) + bt)
    lo, hi = lf[:CHUNK], lf[CHUNK:]
    first, second = (hi, lo) if reverse else (lo, hi)
    away, t_first = _chunk_cumsum(first, not reverse, cs_ref.at[0], cs_ref.at[1])
    toward, t_second = _chunk_cumsum(second, reverse, cs_ref.at[2], cs_ref.at[3])
    a_first = first - away
    a = jnp.concatenate([toward, a_first] if reverse else [a_first, toward], axis=0)
    k_inv = (k * jnp.exp(-a)).astype(BF16)
    return a, k_inv, jnp.exp(t_first), jnp.exp(t_second)


def _hgrn_inproj_kernel(with_queries, h_ref, lbf_ref, lbb_ref, *refs):
    n_w = 5 if with_queries else 3
    n_dir = 4 if with_queries else 3
    w_refs, rest = refs[:n_w], refs[n_w:]
    per_dir = (rest[0:n_dir], rest[n_dir:2 * n_dir])
    rest = rest[2 * n_dir:]
    if with_queries:
        v_ref, g_ref = rest[:2]
        rest = rest[2:]
    else:
        v_ref = rest[0]
        rest = rest[1:]
    wb_refs, cs_s = rest[:n_w], rest[n_w]

    @pl.when(pl.program_id(1) == 0)
    def _():
        for k, (w_ref, wb_ref) in enumerate(zip(w_refs, wb_refs)):
            wk = w_ref[...]
            wb_ref[...] = (wk if k in (2, 3) else wk * 0.5).astype(BF16)

    h = h_ref[...]

    def proj(k):
        return jnp.dot(h, wb_refs[k][...], preferred_element_type=F32)

    tm, w = v_ref.shape
    n_chunks = tm // SCAN_CHUNK
    n_heads = w // HG_DK
    zhs = [proj(0), proj(1)]
    if with_queries:
        q = proj(3).astype(BF16)
    for d, (lb_ref, reverse) in enumerate(((lbf_ref, False), (lbb_ref, True))):
        zh = zhs[d]
        lb = lb_ref[...]
        ki_ref, first_ref, second_ref = per_dir[d][-3:]
        for c in range(n_chunks):
            rows = slice(c * SCAN_CHUNK, (c + 1) * SCAN_CHUNK)
            drows = slice(c * SUBLANES, (c + 1) * SUBLANES)
            for hd in range(n_heads):
                lanes = slice(hd * HG_DK, (hd + 1) * HG_DK)
                slot = (d * n_chunks + c) * n_heads + hd
                a, k_inv, d_first, d_second = _gate_decay(
                    zh[rows, lanes], lb[:, lanes], reverse, cs_s.at[slot])
                if with_queries:
                    per_dir[d][0][rows, lanes] = q[rows, lanes] * jnp.exp(a).astype(BF16)
                ki_ref[rows, lanes] = k_inv
                first_ref[drows, lanes] = d_first
                second_ref[drows, lanes] = d_second
    v_ref[...] = proj(2).astype(BF16)
    if with_queries:
        g_ref[...] = proj(4).astype(BF16)


def _hgrn_inproj(h, w, lb_f, lb_b, with_queries, tm):
    rows = h.shape[0]
    tn = SCAN_HEADS * HG_DK
    groups = HG_K // tn
    n_w = 5 if with_queries else 3
    w_specs = [pl.BlockSpec((D_MODEL, tn), functools.partial(
        lambda k, j, i: (0, k * groups + j), k)) for k in range(n_w)]
    lb_spec = pl.BlockSpec((1, tn), lambda j, i: (0, j))
    dec_rows = tm // SCAN_CHUNK * SUBLANES
    tok = (pl.BlockSpec((None, tm, tn), lambda j, i: (j, i, 0)),
           jax.ShapeDtypeStruct((groups, rows, tn), BF16))
    dec = (pl.BlockSpec((None, dec_rows, tn), lambda j, i: (j, i, 0)),
           jax.ShapeDtypeStruct((groups, rows // SCAN_CHUNK * SUBLANES, tn), F32))
    one_dir = [tok, tok, dec, dec] if with_queries else [tok, dec, dec]
    outs = one_dir + one_dir + ([tok, tok] if with_queries else [tok])
    n_slots = 2 * (tm // SCAN_CHUNK) * SCAN_HEADS
    return pl.pallas_call(
        functools.partial(_hgrn_inproj_kernel, with_queries),
        grid=(groups, rows // tm),
        in_specs=[pl.BlockSpec((tm, D_MODEL), lambda j, i: (i, 0)), lb_spec, lb_spec] + w_specs,
        out_specs=[o[0] for o in outs],
        out_shape=[o[1] for o in outs],
        scratch_shapes=[pltpu.VMEM((D_MODEL, tn), BF16)] * n_w
        + [pltpu.VMEM((n_slots, 4, CHUNK, HG_DK), F32)],
        compiler_params=pltpu.CompilerParams(
            dimension_semantics=("arbitrary", "arbitrary"), vmem_limit_bytes=VMEM_LIMIT),
        name="hgrn_inproj",
    )(h, lb_f, lb_b, *([w] * n_w))


def _scan_kernel(qdf_ref, kif_ref, firstf_ref, secondf_ref,
                 qdb_ref, kib_ref, firstb_ref, secondb_ref, v_ref, g_ref,
                 ckif_ref, cfirstf_ref, csecondf_ref, ckib_ref, cfirstb_ref, csecondb_ref,
                 cv_ref, og_ref, o_ref, st_s, of_s, ob_s):
    c = SCAN_CHUNK
    n_lat = v_ref.shape[0] // c
    n_ctx = cv_ref.shape[0] // c
    n_heads = v_ref.shape[1] // HG_DK
    row = lax.broadcasted_iota(jnp.int32, (c, c), 0)
    col = lax.broadcasted_iota(jnp.int32, (c, c), 1)
    dirs = (((qdf_ref, kif_ref, firstf_ref, secondf_ref),
             (ckif_ref, cfirstf_ref, csecondf_ref), False, col <= row, of_s),
            ((qdb_ref, kib_ref, firstb_ref, secondb_ref),
             (ckib_ref, cfirstb_ref, csecondb_ref), True, col >= row, ob_s))

    def lanes(h):
        return slice(h * HG_DK, (h + 1) * HG_DK)

    def rows(chunk):
        return slice(chunk * c, (chunk + 1) * c)

    def dec_rows(chunk):
        return slice(chunk * SUBLANES, (chunk + 1) * SUBLANES)

    def state_increment(v, k_inv, d_second):
        k_dec = k_inv * d_second.astype(BF16)
        return lax.dot_general(v, k_dec, _TN, preferred_element_type=F32)

    def local(d, chunk):
        qd_ref, ki_ref, _, second_ref = dirs[d][0]
        mask = dirs[d][3]
        sl = rows(chunk)
        d_second = second_ref[dec_rows(chunk), :]
        out = []
        for h in range(n_heads):
            k_inv = ki_ref[sl, lanes(h)]
            sc = lax.dot_general(qd_ref[sl, lanes(h)], k_inv, _NT, preferred_element_type=F32)
            ds_t = state_increment(v_ref[sl, lanes(h)], k_inv, d_second[0:1, lanes(h)])
            out.append((jnp.where(mask, sc, 0.0).astype(BF16), ds_t))
        return out

    def step(d, chunk, products):
        qd_ref, _, first_ref, second_ref = dirs[d][0]
        out_s = dirs[d][4]
        sl = rows(chunk)
        d_first = first_ref[dec_rows(chunk), :]
        decay = d_first * second_ref[dec_rows(chunk), :]
        for h in range(n_heads):
            st_mid = (st_s[d, h] * d_first[0:1, lanes(h)]).astype(BF16)
            inter = lax.dot_general(qd_ref[sl, lanes(h)], st_mid, _NT,
                                    preferred_element_type=F32)
            sc, ds_t = products[h]
            out_s[sl, lanes(h)] = inter + jnp.dot(sc, v_ref[sl, lanes(h)],
                                                  preferred_element_type=F32)
            st_s[d, h] = st_s[d, h] * decay[0:1, lanes(h)] + ds_t

    st_s[...] = jnp.zeros_like(st_s)
    for j in range(n_ctx):
        for d in range(2):
            cki_ref, cfirst_ref, csecond_ref = dirs[d][1]
            cj = n_ctx - 1 - j if dirs[d][2] else j
            d_second = csecond_ref[dec_rows(cj), :]
            decay = cfirst_ref[dec_rows(cj), :] * d_second
            for h in range(n_heads):
                ds_t = state_increment(cv_ref[rows(cj), lanes(h)], cki_ref[rows(cj), lanes(h)],
                                       d_second[0:1, lanes(h)])
                st_s[d, h] = st_s[d, h] * decay[0:1, lanes(h)] + ds_t

    def chunk_of(d, idx):
        return n_lat - 1 - idx if dirs[d][2] else idx

    og = og_ref[...]
    ones = jnp.ones((HG_DV, HG_DV), BF16)

    def readout(chunk):
        sl = rows(chunk)
        for h in range(n_heads):
            o = of_s[sl, lanes(h)] + ob_s[sl, lanes(h)]
            ss = jnp.dot((o * o).astype(BF16), ones, preferred_element_type=F32)
            o = o * lax.rsqrt(ss * (1.0 / HG_DV) + EPS)
            gh = g_ref[sl, lanes(h)].astype(F32)
            o_ref[sl, lanes(h)] = (o * og[:, lanes(h)]
                                   * (gh + gh * jnp.tanh(gh))).astype(o_ref.dtype)

    assert n_lat % 2 == 0
    products = [local(d, chunk_of(d, 0)) for d in range(2)]
    for idx in range(n_lat):
        for d in range(2):
            step(d, chunk_of(d, idx), products[d])
        if idx + 1 < n_lat:
            products = [local(d, chunk_of(d, idx + 1)) for d in range(2)]
        if 2 * idx >= n_lat:
            readout(idx)
            readout(n_lat - 1 - idx)


def _hgrn2_scan(lat_ops, ctx_ops, onorm_g, nb):
    groups, rows, w = lat_ops[-1].shape
    n = rows // nb

    def blocked(a):
        per_batch = a.shape[1] // nb
        return (a.reshape(groups, nb, per_batch, w),
                pl.BlockSpec((None, None, per_batch, w), lambda b, h: (h, b, 0, 0)))

    ops, specs = zip(*[blocked(a) for a in (*lat_ops, *ctx_ops)])
    vec = pl.BlockSpec((1, w), lambda b, h: (0, h))
    return pl.pallas_call(
        _scan_kernel,
        grid=(nb, groups),
        in_specs=list(specs) + [vec],
        out_specs=pl.BlockSpec((None, n, w), lambda b, h: (b, 0, h)),
        out_shape=jax.ShapeDtypeStruct((nb, n, HG_V), BF16),
        scratch_shapes=[pltpu.VMEM((2, SCAN_HEADS, HG_DV, HG_DK), F32),
                        pltpu.VMEM((n, w), F32),
                        pltpu.VMEM((n, w), F32)],
        compiler_params=pltpu.CompilerParams(
            dimension_semantics=("arbitrary", "arbitrary"), vmem_limit_bytes=VMEM_LIMIT),
        name="hgrn2_scan",
    )(*ops, onorm_g)


def _window(n, w):
    t = np.arange(n)
    lo = np.maximum(t - w // 2, 0)
    hi = np.minimum(t + w // 2 - 1, n - 1)
    s = np.arange(n)
    inside = (s[None, :] >= lo[:, None]) & (s[None, :] <= hi[:, None])
    return inside, (hi - lo + 1)


def _pool_operators(n, on_grid):
    mats, icnts, taps = [], [], []
    for w in POOL_WINDOWS:
        if on_grid:
            in_c, cnt_c = _window(GRID_W, w)
            _, cnt_r = _window(n // GRID_W, w)
            m = np.kron(np.eye(POOL_SLAB // GRID_W, dtype=bool), in_c)
            cnt = (cnt_r[:, None] * cnt_c[None, :]).reshape(n)
            taps.append((-(w // 2), w // 2 - 1))
        else:
            assert n == POOL_SLAB
            m, cnt = _window(n, w)
            taps.append((0, 0))
        mats.append(m)
        icnts.append(1.0 / cnt.astype(np.float64))
    colop = jnp.asarray(np.stack(mats).astype(np.float32), dtype=BF16)
    icnt = jnp.asarray(np.stack(icnts).astype(np.float32)[..., None])
    return colop, icnt, tuple(taps)


def kernel(x, c, ctx, c_ctx, ada_w, ada_b, pre_g, post_g, ev_w_in, ev_pool_w, ev_pool_scale,
           ev_conv_w, ev_conv_b, ev_w_out, od_w_in, od_onorm_g, od_w_out, lb_logits):
    nb, n, d = x.shape
    nc = ctx.shape[1]
    lat_row = lambda b: b
    ctx_row = lambda b: CTX_ROW

    lb_table = jnp.cumsum(jax.nn.softmax(lb_logits.astype(F32), axis=1), axis=1)

    cc = jnp.zeros((MOD_ROWS, d), F32).at[:nb].set(c).at[CTX_ROW].set(c_ctx)
    mod = _ada_table(cc, ada_w, ada_b)
    mod = mod.reshape(2, MOD_ROWS, 3, 1, d)
    shift = [mod[l, :, 0] for l in range(2)]
    scale = [mod[l, :, 1] for l in range(2)]
    gate = [mod[l, :, 2] for l in range(2)]

    ctx_flat = ctx.reshape(1, nb * nc, d)

    pool_w = ev_pool_w[0].astype(BF16)
    pool_scale = ev_pool_scale[0].reshape(1, POOL_W)
    conv_w = ev_conv_w[0]
    conv_b = ev_conv_b[0].reshape(1, CONV_W)
    pre0 = pre_g[0].reshape(1, d)
    post0 = post_g[0].reshape(1, d)

    pre1 = pre_g[1].reshape(1, d)
    post1 = post_g[1].reshape(1, d)
    norm1 = (shift[1], scale[1], pre1)

    def even_mixer(tokens3, mod_row, on_grid):
        b3, n3, _ = tokens3.shape
        h = _norm_tokens(tokens3, mod_row, shift[0], scale[0], pre0, TM_NORM)
        proj, = _inproj(h.reshape(b3 * n3, d), ev_w_in[0], EVEN_IN, [BF16], TM_IN, 1024)
        n_tok = n if on_grid else nc
        proj = proj.reshape(nb, n_tok, EVEN_IN)
        colop, icnt, taps = _pool_operators(n_tok, on_grid)
        a_out = _pool_mixer(proj, colop, icnt, taps, pool_w, pool_scale)
        b_out = _conv_mixer(proj, conv_w, conv_b)
        acts = [a_out.reshape(b3, n3, POOL_W), b_out.reshape(b3, n3, CONV_W)]
        return _outproj(acts, ev_w_out[0], tokens3, mod_row, gate[0], post0, TM_OUT, norm1)

    x1, h1 = even_mixer(x, lat_row, True)
    _, hc1 = even_mixer(ctx_flat, ctx_row, False)

    lb_f = lb_table[0, 1].reshape(1, HG_K)
    lb_b = lb_table[1, 1].reshape(1, HG_K)
    lat_ops = _hgrn_inproj(h1.reshape(nb * n, d), od_w_in[0], lb_f, lb_b, True, TM_IN)
    ctx_ops = _hgrn_inproj(hc1.reshape(nb * nc, d), od_w_in[0], lb_f, lb_b, False, TM_IN)
    o = _hgrn2_scan(lat_ops, ctx_ops, od_onorm_g[0].reshape(1, HG_V), nb)
    return _outproj([o], od_w_out[0], x1, lat_row, gate[1], post1, TM_OUT)
```

```python
import functools

import numpy as np
import jax
import jax.numpy as jnp
from jax import lax
from jax.experimental import pallas as pl
from jax.experimental.pallas import tpu as pltpu

F32 = jnp.float32
BF16 = jnp.bfloat16

D_MODEL = 2048
BATCH = 4
SEQ = 2048
CTX_LEN = 256
GRID_W = 64
EPS = 1e-6

POOL_WINDOWS = (2, 4, 8, 16)
POOL_W = D_MODEL // 2
POOL_GROUP = POOL_W // len(POOL_WINDOWS)
CONV_W = D_MODEL // 2
EVEN_IN = 2 * POOL_W + 4 * CONV_W

HG_DK = 128
HG_HEADS = D_MODEL // HG_DK
HG_DV = D_MODEL // HG_HEADS
HG_K = HG_HEADS * HG_DK
HG_V = HG_HEADS * HG_DV
ODD_IN = 3 * HG_K + 2 * HG_V
CHUNK = 64

MOD_ROWS = 8
CTX_ROW = BATCH
VMEM_LIMIT = 56 * 1024 * 1024
TM_NORM = 1024
TM_IN = 1024
TM_OUT = 512


def _silu(v):
    hv = 0.5 * v
    return hv + hv * jnp.tanh(hv)


def _ada_kernel(cc_ref, w_ref, b_ref, o_ref):
    s = _silu(cc_ref[...])
    o_ref[...] = jnp.dot(s, w_ref[...], preferred_element_type=F32) + b_ref[...]


def _ada_table(cc, ada_w, ada_b):
    depth = ada_w.shape[0]
    tn = 1024
    return pl.pallas_call(
        _ada_kernel,
        grid=(depth, 3 * D_MODEL // tn),
        in_specs=[
            pl.BlockSpec((MOD_ROWS, D_MODEL), lambda l, j: (0, 0)),
            pl.BlockSpec((None, D_MODEL, tn), lambda l, j: (l, 0, j)),
            pl.BlockSpec((None, 1, tn), lambda l, j: (l, 0, j)),
        ],
        out_specs=pl.BlockSpec((None, MOD_ROWS, tn), lambda l, j: (l, 0, j)),
        out_shape=jax.ShapeDtypeStruct((depth, MOD_ROWS, 3 * D_MODEL), F32),
        compiler_params=pltpu.CompilerParams(
            dimension_semantics=("arbitrary", "arbitrary"), vmem_limit_bytes=VMEM_LIMIT),
        name="ada_table",
    )(cc, ada_w, ada_b.reshape(depth, 1, 3 * D_MODEL))


def _norm_mod(x, g, scale, shift):
    ms = jnp.mean(x * x, axis=-1, keepdims=True)
    y = x * lax.rsqrt(ms + EPS) * g
    return y * (1.0 + scale) + shift


def _norm_kernel(x_ref, shift_ref, scale_ref, g_ref, o_ref):
    o_ref[...] = _norm_mod(x_ref[...], g_ref[...], scale_ref[...],
                           shift_ref[...]).astype(o_ref.dtype)


def _norm_tokens(x3, mod_row_of_batch, shift, scale, g, tm):
    nb, rows, _ = x3.shape
    mod_map = lambda b, i: (mod_row_of_batch(b), 0, 0)
    return pl.pallas_call(
        _norm_kernel,
        grid=(nb, rows // tm),
        in_specs=[
            pl.BlockSpec((None, tm, D_MODEL), lambda b, i: (b, i, 0)),
            pl.BlockSpec((None, 1, D_MODEL), mod_map),
            pl.BlockSpec((None, 1, D_MODEL), mod_map),
            pl.BlockSpec((1, D_MODEL), lambda b, i: (0, 0)),
        ],
        out_specs=pl.BlockSpec((None, tm, D_MODEL), lambda b, i: (b, i, 0)),
        out_shape=jax.ShapeDtypeStruct((nb, rows, D_MODEL), BF16),
        compiler_params=pltpu.CompilerParams(
            dimension_semantics=("arbitrary", "arbitrary"), vmem_limit_bytes=VMEM_LIMIT),
        name="norm_tokens",
    )(x3, shift, scale, g)


def _inproj_kernel(n_seg, h_ref, *refs):
    w_refs = refs[:n_seg]
    o_refs = refs[n_seg:2 * n_seg]
    wb_refs = refs[2 * n_seg:]

    @pl.when(pl.program_id(1) == 0)
    def _():
        for w_ref, wb_ref in zip(w_refs, wb_refs):
            wb_ref[...] = w_ref[...].astype(BF16)

    for wb_ref, o_ref in zip(wb_refs, o_refs):
        o_ref[...] = jnp.dot(h_ref[...], wb_ref[...],
                             preferred_element_type=F32).astype(o_ref.dtype)


def _inproj(h, w, seg_width, seg_dtypes, tm, tn):
    rows = h.shape[0]
    n_seg = len(seg_dtypes)
    n_tiles = seg_width // tn
    w_specs = [pl.BlockSpec((D_MODEL, tn), functools.partial(
        lambda k, j, i: (0, k * n_tiles + j), k)) for k in range(n_seg)]
    return pl.pallas_call(
        functools.partial(_inproj_kernel, n_seg),
        grid=(n_tiles, rows // tm),
        in_specs=[pl.BlockSpec((tm, D_MODEL), lambda j, i: (i, 0))] + w_specs,
        out_specs=[pl.BlockSpec((tm, tn), lambda j, i: (i, j))] * n_seg,
        out_shape=[jax.ShapeDtypeStruct((rows, seg_width), dt) for dt in seg_dtypes],
        scratch_shapes=[pltpu.VMEM((D_MODEL, tn), BF16)] * n_seg,
        compiler_params=pltpu.CompilerParams(
            dimension_semantics=("arbitrary", "arbitrary"), vmem_limit_bytes=VMEM_LIMIT),
        name="inproj",
    )(h, *([w] * n_seg))


POOL_SLAB = 256
POOL_ROWS = 128


def _pool_kernel(taps, pad, colop_ref, icnt_ref, v_ref, gate_ref, w_ref, sc_ref, o_ref,
                 box_s, pooled_s):
    n = v_ref.shape[0]
    if pad:
        box_s[0:pad, :] = jnp.zeros((pad, POOL_GROUP), F32)
        box_s[pad + n:pad + n + pad, :] = jnp.zeros((pad, POOL_GROUP), F32)
    for gi, (lo, hi) in enumerate(taps):
        lanes = slice(gi * POOL_GROUP, (gi + 1) * POOL_GROUP)
        for s in range(n // POOL_SLAB):
            tok = slice(s * POOL_SLAB, (s + 1) * POOL_SLAB)
            box_s[pad + s * POOL_SLAB:pad + (s + 1) * POOL_SLAB, :] = jnp.dot(
                colop_ref[gi], v_ref[tok, lanes], preferred_element_type=F32)

        def block(i, carry, gi=gi, lo=lo, hi=hi, lanes=lanes):
            r0 = pl.multiple_of(i * POOL_ROWS, POOL_ROWS)
            tok = pl.ds(r0, POOL_ROWS)
            acc = box_s[pl.ds(pad + lo * GRID_W + r0, POOL_ROWS), :]
            for dlt in range(lo + 1, hi + 1):
                acc = acc + box_s[pl.ds(pad + dlt * GRID_W + r0, POOL_ROWS), :]
            pooled = acc * icnt_ref[gi, tok, :] - v_ref[tok, lanes].astype(F32)
            pooled_s[tok, :] = pooled.astype(BF16)
            return carry

        lax.fori_loop(0, n // POOL_ROWS, block, 0, unroll=2)
        mixed = jnp.dot(pooled_s[...], w_ref[gi], preferred_element_type=F32)
        o_ref[:, lanes] = (mixed * sc_ref[:, lanes]
                           * _silu(gate_ref[:, lanes].astype(F32))).astype(o_ref.dtype)


def _pool_mixer(proj, colop, icnt, taps, pool_w, pool_scale):
    nb, n, _ = proj.shape
    ng = len(POOL_WINDOWS)
    pad = max(max(-lo, hi) for lo, hi in taps) * GRID_W
    pad = -(-pad // POOL_ROWS) * POOL_ROWS
    return pl.pallas_call(
        functools.partial(_pool_kernel, taps, pad),
        grid=(nb,),
        in_specs=[
            pl.BlockSpec((ng, POOL_SLAB, POOL_SLAB), lambda b: (0, 0, 0)),
            pl.BlockSpec((ng, n, 1), lambda b: (0, 0, 0)),
            pl.BlockSpec((None, n, POOL_W), lambda b: (b, 0, 0)),
            pl.BlockSpec((None, n, POOL_W), lambda b: (b, 0, 1)),
            pl.BlockSpec((ng, POOL_GROUP, POOL_GROUP), lambda b: (0, 0, 0)),
            pl.BlockSpec((1, POOL_W), lambda b: (0, 0)),
        ],
        out_specs=pl.BlockSpec((None, n, POOL_W), lambda b: (b, 0, 0)),
        out_shape=jax.ShapeDtypeStruct((nb, n, POOL_W), BF16),
        scratch_shapes=[pltpu.VMEM((n + 2 * pad, POOL_GROUP), F32),
                        pltpu.VMEM((n, POOL_GROUP), BF16)],
        compiler_params=pltpu.CompilerParams(
            dimension_semantics=("arbitrary",), vmem_limit_bytes=VMEM_LIMIT),
        name="pool_mixer",
    )(colop, icnt, proj, proj, pool_w, pool_scale)


def _conv_kernel(x_ref, b_ref, c_ref, gate_ref, cw_ref, cb_ref, o_ref):
    n = x_ref.shape[0]
    u = c_ref[...].astype(F32) * x_ref[...].astype(F32)
    row = lax.broadcasted_iota(jnp.int32, u.shape, 0)
    u_prev = jnp.where(row == 0, 0.0, pltpu.roll(u, 1, axis=0))
    u_next = jnp.where(row == n - 1, 0.0, pltpu.roll(u, n - 1, axis=0))
    cw = cw_ref[...]
    conv = u_prev * cw[0:1, :] + u * cw[1:2, :] + u_next * cw[2:3, :] + cb_ref[...]
    o_ref[...] = (b_ref[...].astype(F32) * conv
                  * _silu(gate_ref[...].astype(F32))).astype(o_ref.dtype)


def _conv_mixer(proj, conv_w, conv_b):
    nb, n, _ = proj.shape
    tc = 256
    nblk = CONV_W // tc
    base = 2 * POOL_W // tc

    def seg(k):
        return pl.BlockSpec((None, n, tc), lambda b, j: (b, 0, base + k * nblk + j))

    return pl.pallas_call(
        _conv_kernel,
        grid=(nb, nblk),
        in_specs=[
            seg(0), seg(1), seg(2), seg(3),
            pl.BlockSpec((3, tc), lambda b, j: (0, j)),
            pl.BlockSpec((1, tc), lambda b, j: (0, j)),
        ],
        out_specs=pl.BlockSpec((None, n, tc), lambda b, j: (b, 0, j)),
        out_shape=jax.ShapeDtypeStruct((nb, n, CONV_W), BF16),
        compiler_params=pltpu.CompilerParams(
            dimension_semantics=("arbitrary", "arbitrary"), vmem_limit_bytes=VMEM_LIMIT),
        name="conv_mixer",
    )(proj, proj, proj, proj, conv_w, conv_b)


def _outproj_kernel(n_in, emit_next, *refs):
    a_refs = refs[:n_in]
    w_refs = refs[n_in:2 * n_in]
    x_ref, gate_ref, pg_ref = refs[2 * n_in:2 * n_in + 3]
    rest = refs[2 * n_in + 3:]
    if emit_next:
        shift_ref, scale_ref, g_ref, o_ref, h_ref = rest[:5]
        wb_refs = rest[5:]
    else:
        o_ref = rest[0]
        wb_refs = rest[1:]

    @pl.when((pl.program_id(0) == 0) & (pl.program_id(1) == 0))
    def _():
        for w_ref, wb_ref in zip(w_refs, wb_refs):
            wb_ref[...] = w_ref[...].astype(BF16)

    y = jnp.dot(a_refs[0][...], wb_refs[0][...], preferred_element_type=F32)
    for a_ref, wb_ref in zip(a_refs[1:], wb_refs[1:]):
        y = y + jnp.dot(a_ref[...], wb_ref[...], preferred_element_type=F32)
    ms = jnp.mean(y * y, axis=-1, keepdims=True)
    r = y * lax.rsqrt(ms + EPS) * pg_ref[...]
    x_new = x_ref[...] + gate_ref[...] * r
    o_ref[...] = x_new
    if emit_next:
        h_ref[...] = _norm_mod(x_new, g_ref[...], scale_ref[...],
                               shift_ref[...]).astype(h_ref.dtype)


def _outproj(acts, w, x3, mod_row_of_batch, gate, pg, tm, next_norm=None):
    nb, rows, _ = x3.shape
    n_in = len(acts)
    k_in = acts[0].shape[-1]
    assert all(a.shape[-1] == k_in for a in acts) and n_in * k_in == w.shape[0]
    emit_next = next_norm is not None
    mod_spec = pl.BlockSpec((None, 1, D_MODEL), lambda b, i: (mod_row_of_batch(b), 0, 0))
    vec_spec = pl.BlockSpec((1, D_MODEL), lambda b, i: (0, 0))
    tok_spec = pl.BlockSpec((None, tm, D_MODEL), lambda b, i: (b, i, 0))
    in_specs = [pl.BlockSpec((None, tm, k_in), lambda b, i: (b, i, 0))] * n_in
    in_specs += [pl.BlockSpec((k_in, D_MODEL), functools.partial(lambda k, b, i: (k, 0), k),
                              pipeline_mode=pl.Buffered(1)) for k in range(n_in)]
    in_specs += [tok_spec, mod_spec, vec_spec]
    operands = [*acts, *([w] * n_in), x3, gate, pg]
    out_specs, out_shape = tok_spec, jax.ShapeDtypeStruct((nb, rows, D_MODEL), F32)
    if emit_next:
        in_specs += [mod_spec, mod_spec, vec_spec]
        operands += list(next_norm)
        out_specs = [tok_spec, tok_spec]
        out_shape = [out_shape, jax.ShapeDtypeStruct((nb, rows, D_MODEL), BF16)]
    return pl.pallas_call(
        functools.partial(_outproj_kernel, n_in, emit_next),
        grid=(nb, rows // tm),
        in_specs=in_specs,
        out_specs=out_specs,
        out_shape=out_shape,
        scratch_shapes=[pltpu.VMEM((k_in, D_MODEL), BF16)] * n_in,
        compiler_params=pltpu.CompilerParams(
            dimension_semantics=("arbitrary", "arbitrary"), vmem_limit_bytes=VMEM_LIMIT),
        name="outproj",
    )(*operands)


_NT = (((1,), (1,)), ((), ()))
_TN = (((0,), (0,)), ((), ()))
SUBLANES = 8
SCAN_HEADS = 2
SCAN_CHUNK = 2 * CHUNK


def _chunk_cumsum(x, reverse, in_ref, out_ref):
    c, w = x.shape
    nblk = c // SUBLANES
    order = range(nblk - 1, -1, -1) if reverse else range(nblk)
    in_ref[...] = x
    pref = [None] * nblk
    acc = None
    for j in order:
        blk = in_ref[pl.ds(j, SUBLANES, stride=nblk), :]
        acc = blk if acc is None else acc + blk
        pref[j] = acc
    total = acc
    sub = lax.broadcasted_iota(jnp.int32, (SUBLANES, w), 0)
    incl = total
    for s in (1, 2, 4):
        if reverse:
            incl = incl + jnp.where(sub < SUBLANES - s,
                                    pltpu.roll(incl, SUBLANES - s, axis=0), 0.0)
        else:
            incl = incl + jnp.where(sub >= s, pltpu.roll(incl, s, axis=0), 0.0)
    before = incl - total
    for j in range(nblk):
        out_ref[pl.ds(j, SUBLANES, stride=nblk), :] = pref[j] + before
    edge = incl[0:1, :] if reverse else incl[SUBLANES - 1:SUBLANES, :]
    return out_ref[...], jnp.broadcast_to(edge, (SUBLANES, w))


def _gate_decay(zh, lb, reverse, cs_ref):
    bt = (0.5 * (1.0 - lb)) * jnp.tanh(zh)
    k = 0.5 * (1.0 - lb) - bt
    lf = jnp.log(0.5 * (1.0 + lb) + bt)
    lo, hi = lf[:CHUNK], lf[CHUNK:]
    first, second = (hi, lo) if reverse else (lo, hi)
    away, t_first = _chunk_cumsum(first, not reverse, cs_ref.at[0], cs_ref.at[1])
    toward, t_second = _chunk_cumsum(second, reverse, cs_ref.at[2], cs_ref.at[3])
    a_first = first - away
    a = jnp.concatenate([toward, a_first] if reverse else [a_first, toward], axis=0)
    k_inv = (k * jnp.exp(-a)).astype(BF16)
    return jnp.exp(a), k_inv, jnp.exp(t_first), jnp.exp(t_second)


def _hgrn_inproj_kernel(with_queries, h_ref, lbf_ref, lbb_ref, *refs):
    n_w = 5 if with_queries else 3
    n_dir = 4 if with_queries else 3
    w_refs, rest = refs[:n_w], refs[n_w:]
    per_dir = (rest[0:n_dir], rest[n_dir:2 * n_dir])
    rest = rest[2 * n_dir:]
    if with_queries:
        v_ref, g_ref = rest[:2]
        rest = rest[2:]
    else:
        v_ref = rest[0]
        rest = rest[1:]
    wb_refs, cs_s = rest[:n_w], rest[n_w]

    @pl.when(pl.program_id(1) == 0)
    def _():
        for k, (w_ref, wb_ref) in enumerate(zip(w_refs, wb_refs)):
            wk = w_ref[...]
            wb_ref[...] = (wk if k in (2, 3) else wk * 0.5).astype(BF16)

    h = h_ref[...]

    def proj(k):
        return jnp.dot(h, wb_refs[k][...], preferred_element_type=F32)

    tm, w = v_ref.shape
    n_chunks = tm // SCAN_CHUNK
    n_heads = w // HG_DK
    zhs = [proj(0), proj(1)]
    if with_queries:
        q = proj(3)
    for d, (lb_ref, reverse) in enumerate(((lbf_ref, False), (lbb_ref, True))):
        zh = zhs[d]
        lb = lb_ref[...]
        ki_ref, first_ref, second_ref = per_dir[d][-3:]
        for c in range(n_chunks):
            rows = slice(c * SCAN_CHUNK, (c + 1) * SCAN_CHUNK)
            drows = slice(c * SUBLANES, (c + 1) * SUBLANES)
            for hd in range(n_heads):
                lanes = slice(hd * HG_DK, (hd + 1) * HG_DK)
                slot = (d * n_chunks + c) * n_heads + hd
                e_a, k_inv, d_first, d_second = _gate_decay(
                    zh[rows, lanes], lb[:, lanes], reverse, cs_s.at[slot])
                if with_queries:
                    per_dir[d][0][rows, lanes] = (q[rows, lanes] * e_a).astype(BF16)
                ki_ref[rows, lanes] = k_inv
                first_ref[drows, lanes] = d_first
                second_ref[drows, lanes] = d_second
    v_ref[...] = proj(2).astype(BF16)
    if with_queries:
        g_ref[...] = proj(4).astype(BF16)


def _hgrn_inproj(h, w, lb_f, lb_b, with_queries, tm):
    rows = h.shape[0]
    tn = SCAN_HEADS * HG_DK
    groups = HG_K // tn
    n_w = 5 if with_queries else 3
    w_specs = [pl.BlockSpec((D_MODEL, tn), functools.partial(
        lambda k, j, i: (0, k * groups + j), k)) for k in range(n_w)]
    lb_spec = pl.BlockSpec((1, tn), lambda j, i: (0, j))
    dec_rows = tm // SCAN_CHUNK * SUBLANES
    tok = (pl.BlockSpec((None, tm, tn), lambda j, i: (j, i, 0)),
           jax.ShapeDtypeStruct((groups, rows, tn), BF16))
    dec = (pl.BlockSpec((None, dec_rows, tn), lambda j, i: (j, i, 0)),
           jax.ShapeDtypeStruct((groups, rows // SCAN_CHUNK * SUBLANES, tn), F32))
    one_dir = [tok, tok, dec, dec] if with_queries else [tok, dec, dec]
    outs = one_dir + one_dir + ([tok, tok] if with_queries else [tok])
    n_slots = 2 * (tm // SCAN_CHUNK) * SCAN_HEADS
    return pl.pallas_call(
        functools.partial(_hgrn_inproj_kernel, with_queries),
        grid=(groups, rows // tm),
        in_specs=[pl.BlockSpec((tm, D_MODEL), lambda j, i: (i, 0)), lb_spec, lb_spec] + w_specs,
        out_specs=[o[0] for o in outs],
        out_shape=[o[1] for o in outs],
        scratch_shapes=[pltpu.VMEM((D_MODEL, tn), BF16)] * n_w
        + [pltpu.VMEM((n_slots, 4, CHUNK, HG_DK), F32)],
        compiler_params=pltpu.CompilerParams(
            dimension_semantics=("arbitrary", "arbitrary"), vmem_limit_bytes=VMEM_LIMIT),
        name="hgrn_inproj",
    )(h, lb_f, lb_b, *([w] * n_w))


def _scan_kernel(qdf_ref, kif_ref, firstf_ref, secondf_ref,
                 qdb_ref, kib_ref, firstb_ref, secondb_ref, v_ref, g_ref,
                 ckif_ref, cfirstf_ref, csecondf_ref, ckib_ref, cfirstb_ref, csecondb_ref,
                 cv_ref, og_ref, o_ref, st_s, of_s, ob_s):
    c = SCAN_CHUNK
    n_lat = v_ref.shape[0] // c
    n_ctx = cv_ref.shape[0] // c
    n_heads = v_ref.shape[1] // HG_DK
    row = lax.broadcasted_iota(jnp.int32, (c, c), 0)
    col = lax.broadcasted_iota(jnp.int32, (c, c), 1)
    dirs = (((qdf_ref, kif_ref, firstf_ref, secondf_ref),
             (ckif_ref, cfirstf_ref, csecondf_ref), False, col <= row, of_s),
            ((qdb_ref, kib_ref, firstb_ref, secondb_ref),
             (ckib_ref, cfirstb_ref, csecondb_ref), True, col >= row, ob_s))

    def lanes(h):
        return slice(h * HG_DK, (h + 1) * HG_DK)

    def rows(chunk):
        return slice(chunk * c, (chunk + 1) * c)

    def dec_rows(chunk):
        return slice(chunk * SUBLANES, (chunk + 1) * SUBLANES)

    def state_increment(v, k_inv, d_second):
        k_dec = k_inv * d_second.astype(BF16)
        return lax.dot_general(v, k_dec, _TN, preferred_element_type=F32)

    def local(d, chunk):
        qd_ref, ki_ref, _, second_ref = dirs[d][0]
        mask = dirs[d][3]
        sl = rows(chunk)
        d_second = second_ref[dec_rows(chunk), :]
        out = []
        for h in range(n_heads):
            k_inv = ki_ref[sl, lanes(h)]
            sc = lax.dot_general(qd_ref[sl, lanes(h)], k_inv, _NT, preferred_element_type=F32)
            ds_t = state_increment(v_ref[sl, lanes(h)], k_inv, d_second[0:1, lanes(h)])
            out.append((jnp.where(mask, sc, 0.0).astype(BF16), ds_t))
        return out

    def step(d, chunk, products):
        qd_ref, _, first_ref, second_ref = dirs[d][0]
        out_s = dirs[d][4]
        sl = rows(chunk)
        d_first = first_ref[dec_rows(chunk), :]
        decay = d_first * second_ref[dec_rows(chunk), :]
        for h in range(n_heads):
            st_mid = (st_s[d, h] * d_first[0:1, lanes(h)]).astype(BF16)
            inter = lax.dot_general(qd_ref[sl, lanes(h)], st_mid, _NT,
                                    preferred_element_type=F32)
            sc, ds_t = products[h]
            out_s[sl, lanes(h)] = inter + jnp.dot(sc, v_ref[sl, lanes(h)],
                                                  preferred_element_type=F32)
            st_s[d, h] = st_s[d, h] * decay[0:1, lanes(h)] + ds_t

    st_s[...] = jnp.zeros_like(st_s)
    for j in range(n_ctx):
        for d in range(2):
            cki_ref, cfirst_ref, csecond_ref = dirs[d][1]
            cj = n_ctx - 1 - j if dirs[d][2] else j
            d_second = csecond_ref[dec_rows(cj), :]
            decay = cfirst_ref[dec_rows(cj), :] * d_second
            for h in range(n_heads):
                ds_t = state_increment(cv_ref[rows(cj), lanes(h)], cki_ref[rows(cj), lanes(h)],
                                       d_second[0:1, lanes(h)])
                st_s[d, h] = st_s[d, h] * decay[0:1, lanes(h)] + ds_t

    def chunk_of(d, idx):
        return n_lat - 1 - idx if dirs[d][2] else idx

    og = og_ref[...]
    ones = jnp.ones((HG_DV, HG_DV), BF16)

    def readout(chunk):
        sl = rows(chunk)
        for h in range(n_heads):
            o = of_s[sl, lanes(h)] + ob_s[sl, lanes(h)]
            ss = jnp.dot((o * o).astype(BF16), ones, preferred_element_type=F32)
            o = o * lax.rsqrt(ss * (1.0 / HG_DV) + EPS)
            gh = g_ref[sl, lanes(h)].astype(F32)
            o_ref[sl, lanes(h)] = (o * og[:, lanes(h)]
                                   * (gh + gh * jnp.tanh(gh))).astype(o_ref.dtype)

    assert n_lat % 2 == 0
    products = [local(d, chunk_of(d, 0)) for d in range(2)]
    for idx in range(n_lat):
        for d in range(2):
            step(d, chunk_of(d, idx), products[d])
        if idx + 1 < n_lat:
            products = [local(d, chunk_of(d, idx + 1)) for d in range(2)]
        if 2 * idx >= n_lat:
            readout(idx)
            readout(n_lat - 1 - idx)


def _hgrn2_scan(lat_ops, ctx_ops, onorm_g, nb):
    groups, rows, w = lat_ops[-1].shape
    n = rows // nb

    def blocked(a):
        per_batch = a.shape[1] // nb
        return (a.reshape(groups, nb, per_batch, w),
                pl.BlockSpec((None, None, per_batch, w), lambda b, h: (h, b, 0, 0)))

    ops, specs = zip(*[blocked(a) for a in (*lat_ops, *ctx_ops)])
    vec = pl.BlockSpec((1, w), lambda b, h: (0, h))
    return pl.pallas_call(
        _scan_kernel,
        grid=(nb, groups),
        in_specs=list(specs) + [vec],
        out_specs=pl.BlockSpec((None, n, w), lambda b, h: (b, 0, h)),
        out_shape=jax.ShapeDtypeStruct((nb, n, HG_V), BF16),
        scratch_shapes=[pltpu.VMEM((2, SCAN_HEADS, HG_DV, HG_DK), F32),
                        pltpu.VMEM((n, w), F32),
                        pltpu.VMEM((n, w), F32)],
        compiler_params=pltpu.CompilerParams(
            dimension_semantics=("arbitrary", "arbitrary"), vmem_limit_bytes=VMEM_LIMIT),
        name="hgrn2_scan",
    )(*ops, onorm_g)


def _window(n, w):
    t = np.arange(n)
    lo = np.maximum(t - w // 2, 0)
    hi = np.minimum(t + w // 2 - 1, n - 1)
    s = np.arange(n)
    inside = (s[None, :] >= lo[:, None]) & (s[None, :] <= hi[:, None])
    return inside, (hi - lo + 1)


def _pool_operators(n, on_grid):
    mats, icnts, taps = [], [], []
    for w in POOL_WINDOWS:
        if on_grid:
            in_c, cnt_c = _window(GRID_W, w)
            _, cnt_r = _window(n // GRID_W, w)
            m = np.kron(np.eye(POOL_SLAB // GRID_W, dtype=bool), in_c)
            cnt = (cnt_r[:, None] * cnt_c[None, :]).reshape(n)
            taps.append((-(w // 2), w // 2 - 1))
        else:
            assert n == POOL_SLAB
            m, cnt = _window(n, w)
            taps.append((0, 0))
        mats.append(m)
        icnts.append(1.0 / cnt.astype(np.float64))
    colop = jnp.asarray(np.stack(mats).astype(np.float32), dtype=BF16)
    icnt = jnp.asarray(np.stack(icnts).astype(np.float32)[..., None])
    return colop, icnt, tuple(taps)


def kernel(x, c, ctx, c_ctx, ada_w, ada_b, pre_g, post_g, ev_w_in, ev_pool_w, ev_pool_scale,
           ev_conv_w, ev_conv_b, ev_w_out, od_w_in, od_onorm_g, od_w_out, lb_logits):
    nb, n, d = x.shape
    nc = ctx.shape[1]
    lat_row = lambda b: b
    ctx_row = lambda b: CTX_ROW

    lb_table = jnp.cumsum(jax.nn.softmax(lb_logits.astype(F32), axis=1), axis=1)

    cc = jnp.zeros((MOD_ROWS, d), F32).at[:nb].set(c).at[CTX_ROW].set(c_ctx)
    mod = _ada_table(cc, ada_w, ada_b)
    mod = mod.reshape(2, MOD_ROWS, 3, 1, d)
    shift = [mod[l, :, 0] for l in range(2)]
    scale = [mod[l, :, 1] for l in range(2)]
    gate = [mod[l, :, 2] for l in range(2)]

    ctx_flat = ctx.reshape(1, nb * nc, d)

    pool_w = ev_pool_w[0].astype(BF16)
    pool_scale = ev_pool_scale[0].reshape(1, POOL_W)
    conv_w = ev_conv_w[0]
    conv_b = ev_conv_b[0].reshape(1, CONV_W)
    pre0 = pre_g[0].reshape(1, d)
    post0 = post_g[0].reshape(1, d)

    pre1 = pre_g[1].reshape(1, d)
    post1 = post_g[1].reshape(1, d)
    norm1 = (shift[1], scale[1], pre1)

    def even_mixer(tokens3, mod_row, on_grid):
        b3, n3, _ = tokens3.shape
        h = _norm_tokens(tokens3, mod_row, shift[0], scale[0], pre0, TM_NORM)
        proj, = _inproj(h.reshape(b3 * n3, d), ev_w_in[0], EVEN_IN, [BF16], TM_IN, 1024)
        n_tok = n if on_grid else nc
        proj = proj.reshape(nb, n_tok, EVEN_IN)
        colop, icnt, taps = _pool_operators(n_tok, on_grid)
        a_out = _pool_mixer(proj, colop, icnt, taps, pool_w, pool_scale)
        b_out = _conv_mixer(proj, conv_w, conv_b)
        acts = [a_out.reshape(b3, n3, POOL_W), b_out.reshape(b3, n3, CONV_W)]
        return _outproj(acts, ev_w_out[0], tokens3, mod_row, gate[0], post0, TM_OUT, norm1)

    x1, h1 = even_mixer(x, lat_row, True)
    _, hc1 = even_mixer(ctx_flat, ctx_row, False)

    lb_f = lb_table[0, 1].reshape(1, HG_K)
    lb_b = lb_table[1, 1].reshape(1, HG_K)
    lat_ops = _hgrn_inproj(h1.reshape(nb * n, d), od_w_in[0], lb_f, lb_b, True, TM_IN)
    ctx_ops = _hgrn_inproj(hc1.reshape(nb * nc, d), od_w_in[0], lb_f, lb_b, False, TM_IN)
    o = _hgrn2_scan(lat_ops, ctx_ops, od_onorm_g[0].reshape(1, HG_V), nb)
    return _outproj([o], od_w_out[0], x1, lat_row, gate[1], post1, TM_OUT)
```

```python
import functools

import numpy as np
import jax
import jax.numpy as jnp
from jax import lax
from jax.experimental import pallas as pl
from jax.experimental.pallas import tpu as pltpu

F32 = jnp.float32
BF16 = jnp.bfloat16

D_MODEL = 2048
BATCH = 4
SEQ = 2048
CTX_LEN = 256
GRID_W = 64
EPS = 1e-6

POOL_WINDOWS = (2, 4, 8, 16)
POOL_W = D_MODEL // 2
POOL_GROUP = POOL_W // len(POOL_WINDOWS)
CONV_W = D_MODEL // 2
EVEN_IN = 2 * POOL_W + 4 * CONV_W

HG_DK = 128
HG_HEADS = D_MODEL // HG_DK
HG_DV = D_MODEL // HG_HEADS
HG_K = HG_HEADS * HG_DK
HG_V = HG_HEADS * HG_DV
ODD_IN = 3 * HG_K + 2 * HG_V
CHUNK = 64

MOD_ROWS = 8
CTX_ROW = BATCH
VMEM_LIMIT = 56 * 1024 * 1024
TM_NORM = 1024
TM_IN0 = 2048
TM_IN = 1024
TM_OUT = 512


def _silu(v):
    hv = 0.5 * v
    return hv + hv * jnp.tanh(hv)


def _ada_kernel(cc_ref, w_ref, b_ref, o_ref):
    s = _silu(cc_ref[...])
    o_ref[...] = jnp.dot(s, w_ref[...], preferred_element_type=F32) + b_ref[...]


def _ada_table(cc, ada_w, ada_b):
    depth = ada_w.shape[0]
    tn = 1024
    return pl.pallas_call(
        _ada_kernel,
        grid=(depth, 3 * D_MODEL // tn),
        in_specs=[
            pl.BlockSpec((MOD_ROWS, D_MODEL), lambda l, j: (0, 0)),
            pl.BlockSpec((None, D_MODEL, tn), lambda l, j: (l, 0, j)),
            pl.BlockSpec((None, 1, tn), lambda l, j: (l, 0, j)),
        ],
        out_specs=pl.BlockSpec((None, MOD_ROWS, tn), lambda l, j: (l, 0, j)),
        out_shape=jax.ShapeDtypeStruct((depth, MOD_ROWS, 3 * D_MODEL), F32),
        compiler_params=pltpu.CompilerParams(
            dimension_semantics=("arbitrary", "arbitrary"), vmem_limit_bytes=VMEM_LIMIT),
        name="ada_table",
    )(cc, ada_w, ada_b.reshape(depth, 1, 3 * D_MODEL))


def _norm_mod(x, g, scale, shift):
    ms = jnp.mean(x * x, axis=-1, keepdims=True)
    y = x * lax.rsqrt(ms + EPS) * g
    return y * (1.0 + scale) + shift


def _norm_kernel(x_ref, shift_ref, scale_ref, g_ref, o_ref):
    o_ref[...] = _norm_mod(x_ref[...], g_ref[...], scale_ref[...],
                           shift_ref[...]).astype(o_ref.dtype)


def _norm_tokens(x3, mod_row_of_batch, shift, scale, g, tm):
    nb, rows, _ = x3.shape
    mod_map = lambda b, i: (mod_row_of_batch(b), 0, 0)
    return pl.pallas_call(
        _norm_kernel,
        grid=(nb, rows // tm),
        in_specs=[
            pl.BlockSpec((None, tm, D_MODEL), lambda b, i: (b, i, 0)),
            pl.BlockSpec((None, 1, D_MODEL), mod_map),
            pl.BlockSpec((None, 1, D_MODEL), mod_map),
            pl.BlockSpec((1, D_MODEL), lambda b, i: (0, 0)),
        ],
        out_specs=pl.BlockSpec((None, tm, D_MODEL), lambda b, i: (b, i, 0)),
        out_shape=jax.ShapeDtypeStruct((nb, rows, D_MODEL), BF16),
        compiler_params=pltpu.CompilerParams(
            dimension_semantics=("arbitrary", "arbitrary"), vmem_limit_bytes=VMEM_LIMIT),
        name="norm_tokens",
    )(x3, shift, scale, g)


def _inproj_kernel(n_seg, h_ref, *refs):
    w_refs = refs[:n_seg]
    o_refs = refs[n_seg:2 * n_seg]
    wb_refs = refs[2 * n_seg:]

    @pl.when(pl.program_id(1) == 0)
    def _():
        for w_ref, wb_ref in zip(w_refs, wb_refs):
            wb_ref[...] = w_ref[...].astype(BF16)

    for wb_ref, o_ref in zip(wb_refs, o_refs):
        o_ref[...] = jnp.dot(h_ref[...], wb_ref[...],
                             preferred_element_type=F32).astype(o_ref.dtype)


def _inproj(h, w, seg_width, seg_dtypes, tm, tn):
    rows = h.shape[0]
    n_seg = len(seg_dtypes)
    n_tiles = seg_width // tn
    w_specs = [pl.BlockSpec((D_MODEL, tn), functools.partial(
        lambda k, j, i: (0, k * n_tiles + j), k)) for k in range(n_seg)]
    return pl.pallas_call(
        functools.partial(_inproj_kernel, n_seg),
        grid=(n_tiles, rows // tm),
        in_specs=[pl.BlockSpec((tm, D_MODEL), lambda j, i: (i, 0))] + w_specs,
        out_specs=[pl.BlockSpec((tm, tn), lambda j, i: (i, j))] * n_seg,
        out_shape=[jax.ShapeDtypeStruct((rows, seg_width), dt) for dt in seg_dtypes],
        scratch_shapes=[pltpu.VMEM((D_MODEL, tn), BF16)] * n_seg,
        compiler_params=pltpu.CompilerParams(
            dimension_semantics=("arbitrary", "arbitrary"), vmem_limit_bytes=VMEM_LIMIT),
        name="inproj",
    )(h, *([w] * n_seg))


POOL_SLAB = 256
POOL_ROWS = 128


def _pool_kernel(taps, pad, colop_ref, icnt_ref, v_ref, gate_ref, w_ref, sc_ref, o_ref,
                 box_s, pooled_s):
    n = v_ref.shape[0]
    for gi, (lo, hi) in enumerate(taps):
        lanes = slice(gi * POOL_GROUP, (gi + 1) * POOL_GROUP)
        if pad:
            box_s[gi, 0:pad, :] = jnp.zeros((pad, POOL_GROUP), F32)
            box_s[gi, pad + n:pad + n + pad, :] = jnp.zeros((pad, POOL_GROUP), F32)
        for s in range(n // POOL_SLAB):
            tok = slice(s * POOL_SLAB, (s + 1) * POOL_SLAB)
            box_s[gi, pad + s * POOL_SLAB:pad + (s + 1) * POOL_SLAB, :] = jnp.dot(
                colop_ref[gi], v_ref[tok, lanes], preferred_element_type=F32)

        for r0 in range(0, n, POOL_ROWS):
            tok = slice(r0, r0 + POOL_ROWS)
            first = pad + lo * GRID_W + r0
            acc = box_s[gi, first:first + POOL_ROWS, :]
            for dlt in range(1, hi - lo + 1):
                acc = acc + box_s[gi, first + dlt * GRID_W:first + dlt * GRID_W + POOL_ROWS, :]
            pooled = acc * icnt_ref[gi, tok, :] - v_ref[tok, lanes].astype(F32)
            pooled_s[gi, tok, :] = pooled.astype(BF16)
        mixed = jnp.dot(pooled_s[gi], w_ref[gi], preferred_element_type=F32)
        o_ref[:, lanes] = (mixed * sc_ref[:, lanes]
                           * _silu(gate_ref[:, lanes].astype(F32))).astype(o_ref.dtype)


def _pool_mixer(proj, colop, icnt, taps, pool_w, pool_scale):
    nb, n, _ = proj.shape
    ng = len(POOL_WINDOWS)
    pad = max(max(-lo, hi) for lo, hi in taps) * GRID_W
    pad = -(-pad // POOL_ROWS) * POOL_ROWS
    return pl.pallas_call(
        functools.partial(_pool_kernel, taps, pad),
        grid=(nb,),
        in_specs=[
            pl.BlockSpec((ng, POOL_SLAB, POOL_SLAB), lambda b: (0, 0, 0)),
            pl.BlockSpec((ng, n, 1), lambda b: (0, 0, 0)),
            pl.BlockSpec((None, n, POOL_W), lambda b: (b, 0, 0)),
            pl.BlockSpec((None, n, POOL_W), lambda b: (b, 0, 1)),
            pl.BlockSpec((ng, POOL_GROUP, POOL_GROUP), lambda b: (0, 0, 0)),
            pl.BlockSpec((1, POOL_W), lambda b: (0, 0)),
        ],
        out_specs=pl.BlockSpec((None, n, POOL_W), lambda b: (b, 0, 0)),
        out_shape=jax.ShapeDtypeStruct((nb, n, POOL_W), BF16),
        scratch_shapes=[pltpu.VMEM((ng, n + 2 * pad, POOL_GROUP), F32),
                        pltpu.VMEM((ng, n, POOL_GROUP), BF16)],
        compiler_params=pltpu.CompilerParams(
            dimension_semantics=("arbitrary",), vmem_limit_bytes=VMEM_LIMIT),
        name="pool_mixer",
    )(colop, icnt, proj, proj, pool_w, pool_scale)


def _conv_kernel(x_ref, b_ref, c_ref, gate_ref, cw_ref, cb_ref, o_ref):
    n = x_ref.shape[0]
    u = c_ref[...].astype(F32) * x_ref[...].astype(F32)
    row = lax.broadcasted_iota(jnp.int32, u.shape, 0)
    u_prev = jnp.where(row == 0, 0.0, pltpu.roll(u, 1, axis=0))
    u_next = jnp.where(row == n - 1, 0.0, pltpu.roll(u, n - 1, axis=0))
    cw = cw_ref[...]
    conv = u_prev * cw[0:1, :] + u * cw[1:2, :] + u_next * cw[2:3, :] + cb_ref[...]
    o_ref[...] = (b_ref[...].astype(F32) * conv
                  * _silu(gate_ref[...].astype(F32))).astype(o_ref.dtype)


def _conv_mixer(proj, conv_w, conv_b):
    nb, n, _ = proj.shape
    tc = 256
    nblk = CONV_W // tc
    base = 2 * POOL_W // tc

    def seg(k):
        return pl.BlockSpec((None, n, tc), lambda b, j: (b, 0, base + k * nblk + j))

    return pl.pallas_call(
        _conv_kernel,
        grid=(nb, nblk),
        in_specs=[
            seg(0), seg(1), seg(2), seg(3),
            pl.BlockSpec((3, tc), lambda b, j: (0, j)),
            pl.BlockSpec((1, tc), lambda b, j: (0, j)),
        ],
        out_specs=pl.BlockSpec((None, n, tc), lambda b, j: (b, 0, j)),
        out_shape=jax.ShapeDtypeStruct((nb, n, CONV_W), BF16),
        compiler_params=pltpu.CompilerParams(
            dimension_semantics=("arbitrary", "arbitrary"), vmem_limit_bytes=VMEM_LIMIT),
        name="conv_mixer",
    )(proj, proj, proj, proj, conv_w, conv_b)


def _outproj_kernel(n_in, emit_next, *refs):
    a_refs = refs[:n_in]
    w_refs = refs[n_in:2 * n_in]
    x_ref, gate_ref, pg_ref = refs[2 * n_in:2 * n_in + 3]
    rest = refs[2 * n_in + 3:]
    if emit_next:
        shift_ref, scale_ref, g_ref, o_ref, h_ref = rest[:5]
        wb_refs = rest[5:]
    else:
        o_ref = rest[0]
        wb_refs = rest[1:]

    @pl.when((pl.program_id(0) == 0) & (pl.program_id(1) == 0))
    def _():
        for w_ref, wb_ref in zip(w_refs, wb_refs):
            wb_ref[...] = w_ref[...].astype(BF16)

    y = jnp.dot(a_refs[0][...], wb_refs[0][...], preferred_element_type=F32)
    for a_ref, wb_ref in zip(a_refs[1:], wb_refs[1:]):
        y = y + jnp.dot(a_ref[...], wb_ref[...], preferred_element_type=F32)
    ms = jnp.mean(y * y, axis=-1, keepdims=True)
    r = y * lax.rsqrt(ms + EPS) * pg_ref[...]
    x_new = x_ref[...] + gate_ref[...] * r
    o_ref[...] = x_new
    if emit_next:
        h_ref[...] = _norm_mod(x_new, g_ref[...], scale_ref[...],
                               shift_ref[...]).astype(h_ref.dtype)


def _outproj(acts, w, x3, mod_row_of_batch, gate, pg, tm, next_norm=None):
    nb, rows, _ = x3.shape
    n_in = len(acts)
    k_in = acts[0].shape[-1]
    assert all(a.shape[-1] == k_in for a in acts) and n_in * k_in == w.shape[0]
    emit_next = next_norm is not None
    mod_spec = pl.BlockSpec((None, 1, D_MODEL), lambda b, i: (mod_row_of_batch(b), 0, 0))
    vec_spec = pl.BlockSpec((1, D_MODEL), lambda b, i: (0, 0))
    tok_spec = pl.BlockSpec((None, tm, D_MODEL), lambda b, i: (b, i, 0))
    in_specs = [pl.BlockSpec((None, tm, k_in), lambda b, i: (b, i, 0))] * n_in
    in_specs += [pl.BlockSpec((k_in, D_MODEL), functools.partial(lambda k, b, i: (k, 0), k),
                              pipeline_mode=pl.Buffered(1)) for k in range(n_in)]
    in_specs += [tok_spec, mod_spec, vec_spec]
    operands = [*acts, *([w] * n_in), x3, gate, pg]
    out_specs, out_shape = tok_spec, jax.ShapeDtypeStruct((nb, rows, D_MODEL), F32)
    if emit_next:
        in_specs += [mod_spec, mod_spec, vec_spec]
        operands += list(next_norm)
        out_specs = [tok_spec, tok_spec]
        out_shape = [out_shape, jax.ShapeDtypeStruct((nb, rows, D_MODEL), BF16)]
    return pl.pallas_call(
        functools.partial(_outproj_kernel, n_in, emit_next),
        grid=(nb, rows // tm),
        in_specs=in_specs,
        out_specs=out_specs,
        out_shape=out_shape,
        scratch_shapes=[pltpu.VMEM((k_in, D_MODEL), BF16)] * n_in,
        compiler_params=pltpu.CompilerParams(
            dimension_semantics=("arbitrary", "arbitrary"), vmem_limit_bytes=VMEM_LIMIT),
        name="outproj",
    )(*operands)


_NT = (((1,), (1,)), ((), ()))
_TN = (((0,), (0,)), ((), ()))
SUBLANES = 8
SCAN_HEADS = 2
SCAN_CHUNK = 2 * CHUNK


def _chunk_cumsum(x, reverse, in_ref, out_ref):
    c, w = x.shape
    nblk = c // SUBLANES
    order = range(nblk - 1, -1, -1) if reverse else range(nblk)
    in_ref[...] = x
    pref = [None] * nblk
    acc = None
    for j in order:
        blk = in_ref[pl.ds(j, SUBLANES, stride=nblk), :]
        acc = blk if acc is None else acc + blk
        pref[j] = acc
    total = acc
    sub = lax.broadcasted_iota(jnp.int32, (SUBLANES, w), 0)
    incl = total
    for s in (1, 2, 4):
        if reverse:
            incl = incl + jnp.where(sub < SUBLANES - s,
                                    pltpu.roll(incl, SUBLANES - s, axis=0), 0.0)
        else:
            incl = incl + jnp.where(sub >= s, pltpu.roll(incl, s, axis=0), 0.0)
    before = incl - total
    for j in range(nblk):
        out_ref[pl.ds(j, SUBLANES, stride=nblk), :] = pref[j] + before
    edge = incl[0:1, :] if reverse else incl[SUBLANES - 1:SUBLANES, :]
    return out_ref[...], jnp.broadcast_to(edge, (SUBLANES, w))


def _gate_decay(zh, lb, reverse, cs_ref):
    bt = (0.5 * (1.0 - lb)) * jnp.tanh(zh)
    k = 0.5 * (1.0 - lb) - bt
    lf = jnp.log(0.5 * (1.0 + lb) + bt)
    lo, hi = lf[:CHUNK], lf[CHUNK:]
    first, second = (hi, lo) if reverse else (lo, hi)
    away, t_first = _chunk_cumsum(first, not reverse, cs_ref.at[0], cs_ref.at[1])
    toward, t_second = _chunk_cumsum(second, reverse, cs_ref.at[2], cs_ref.at[3])
    a_first = first - away
    a = jnp.concatenate([toward, a_first] if reverse else [a_first, toward], axis=0)
    k_inv = (k * jnp.exp(-a)).astype(BF16)
    return jnp.exp(a), k_inv, jnp.exp(t_first), jnp.exp(t_second)


def _hgrn_inproj_kernel(with_queries, h_ref, lbf_ref, lbb_ref, *refs):
    n_w = 5 if with_queries else 3
    n_dir = 4 if with_queries else 3
    w_refs, rest = refs[:n_w], refs[n_w:]
    per_dir = (rest[0:n_dir], rest[n_dir:2 * n_dir])
    rest = rest[2 * n_dir:]
    if with_queries:
        v_ref, g_ref = rest[:2]
        rest = rest[2:]
    else:
        v_ref = rest[0]
        rest = rest[1:]
    wb_refs, cs_s = rest[:n_w], rest[n_w]

    @pl.when(pl.program_id(1) == 0)
    def _():
        for k, (w_ref, wb_ref) in enumerate(zip(w_refs, wb_refs)):
            wk = w_ref[...]
            wb_ref[...] = (wk if k in (2, 3) else wk * 0.5).astype(BF16)

    h = h_ref[...]

    def proj(k):
        return jnp.dot(h, wb_refs[k][...], preferred_element_type=F32)

    tm, w = v_ref.shape
    n_chunks = tm // SCAN_CHUNK
    n_heads = w // HG_DK
    zhs = [proj(0), proj(1)]
    if with_queries:
        q = proj(3)
    for d, (lb_ref, reverse) in enumerate(((lbf_ref, False), (lbb_ref, True))):
        zh = zhs[d]
        lb = lb_ref[...]
        ki_ref, first_ref, second_ref = per_dir[d][-3:]
        for c in range(n_chunks):
            rows = slice(c * SCAN_CHUNK, (c + 1) * SCAN_CHUNK)
            drows = slice(c * SUBLANES, (c + 1) * SUBLANES)
            for hd in range(n_heads):
                lanes = slice(hd * HG_DK, (hd + 1) * HG_DK)
                slot = (d * n_chunks + c) * n_heads + hd
                e_a, k_inv, d_first, d_second = _gate_decay(
                    zh[rows, lanes], lb[:, lanes], reverse, cs_s.at[slot])
                if with_queries:
                    per_dir[d][0][rows, lanes] = (q[rows, lanes] * e_a).astype(BF16)
                ki_ref[rows, lanes] = k_inv
                first_ref[drows, lanes] = d_first
                second_ref[drows, lanes] = d_second
    v_ref[...] = proj(2).astype(BF16)
    if with_queries:
        g_ref[...] = proj(4).astype(BF16)


def _hgrn_inproj(h, w, lb_f, lb_b, with_queries, tm):
    rows = h.shape[0]
    tn = SCAN_HEADS * HG_DK
    groups = HG_K // tn
    n_w = 5 if with_queries else 3
    w_specs = [pl.BlockSpec((D_MODEL, tn), functools.partial(
        lambda k, j, i: (0, k * groups + j), k)) for k in range(n_w)]
    lb_spec = pl.BlockSpec((1, tn), lambda j, i: (0, j))
    dec_rows = tm // SCAN_CHUNK * SUBLANES
    tok = (pl.BlockSpec((None, tm, tn), lambda j, i: (j, i, 0)),
           jax.ShapeDtypeStruct((groups, rows, tn), BF16))
    dec = (pl.BlockSpec((None, dec_rows, tn), lambda j, i: (j, i, 0)),
           jax.ShapeDtypeStruct((groups, rows // SCAN_CHUNK * SUBLANES, tn), F32))
    one_dir = [tok, tok, dec, dec] if with_queries else [tok, dec, dec]
    outs = one_dir + one_dir + ([tok, tok] if with_queries else [tok])
    n_slots = 2 * (tm // SCAN_CHUNK) * SCAN_HEADS
    return pl.pallas_call(
        functools.partial(_hgrn_inproj_kernel, with_queries),
        grid=(groups, rows // tm),
        in_specs=[pl.BlockSpec((tm, D_MODEL), lambda j, i: (i, 0)), lb_spec, lb_spec] + w_specs,
        out_specs=[o[0] for o in outs],
        out_shape=[o[1] for o in outs],
        scratch_shapes=[pltpu.VMEM((D_MODEL, tn), BF16)] * n_w
        + [pltpu.VMEM((n_slots, 4, CHUNK, HG_DK), F32)],
        compiler_params=pltpu.CompilerParams(
            dimension_semantics=("arbitrary", "arbitrary"), vmem_limit_bytes=VMEM_LIMIT),
        name="hgrn_inproj",
    )(h, lb_f, lb_b, *([w] * n_w))


def _scan_kernel(qdf_ref, kif_ref, firstf_ref, secondf_ref,
                 qdb_ref, kib_ref, firstb_ref, secondb_ref, v_ref, g_ref,
                 ckif_ref, cfirstf_ref, csecondf_ref, ckib_ref, cfirstb_ref, csecondb_ref,
                 cv_ref, og_ref, o_ref, st_s, of_s, ob_s):
    c = SCAN_CHUNK
    n_lat = v_ref.shape[0] // c
    n_ctx = cv_ref.shape[0] // c
    n_heads = v_ref.shape[1] // HG_DK
    row = lax.broadcasted_iota(jnp.int32, (c, c), 0)
    col = lax.broadcasted_iota(jnp.int32, (c, c), 1)
    dirs = (((qdf_ref, kif_ref, firstf_ref, secondf_ref),
             (ckif_ref, cfirstf_ref, csecondf_ref), False, col <= row, of_s),
            ((qdb_ref, kib_ref, firstb_ref, secondb_ref),
             (ckib_ref, cfirstb_ref, csecondb_ref), True, col >= row, ob_s))

    def lanes(h):
        return slice(h * HG_DK, (h + 1) * HG_DK)

    def rows(chunk):
        return slice(chunk * c, (chunk + 1) * c)

    def dec_rows(chunk):
        return slice(chunk * SUBLANES, (chunk + 1) * SUBLANES)

    def state_increment(v, k_inv, d_second):
        k_dec = k_inv * d_second.astype(BF16)
        return lax.dot_general(v, k_dec, _TN, preferred_element_type=F32)

    def local(d, chunk):
        qd_ref, ki_ref, _, second_ref = dirs[d][0]
        mask = dirs[d][3]
        sl = rows(chunk)
        d_second = second_ref[dec_rows(chunk), :]
        out = []
        for h in range(n_heads):
            k_inv = ki_ref[sl, lanes(h)]
            sc = lax.dot_general(qd_ref[sl, lanes(h)], k_inv, _NT, preferred_element_type=F32)
            ds_t = state_increment(v_ref[sl, lanes(h)], k_inv, d_second[0:1, lanes(h)])
            out.append((jnp.where(mask, sc, 0.0).astype(BF16), ds_t))
        return out

    def step(d, chunk, products):
        qd_ref, _, first_ref, second_ref = dirs[d][0]
        out_s = dirs[d][4]
        sl = rows(chunk)
        d_first = first_ref[dec_rows(chunk), :]
        decay = d_first * second_ref[dec_rows(chunk), :]
        for h in range(n_heads):
            st_mid = (st_s[d, h] * d_first[0:1, lanes(h)]).astype(BF16)
            inter = lax.dot_general(qd_ref[sl, lanes(h)], st_mid, _NT,
                                    preferred_element_type=F32)
            sc, ds_t = products[h]
            out_s[sl, lanes(h)] = inter + jnp.dot(sc, v_ref[sl, lanes(h)],
                                                  preferred_element_type=F32)
            st_s[d, h] = st_s[d, h] * decay[0:1, lanes(h)] + ds_t

    st_s[...] = jnp.zeros_like(st_s)
    for j in range(n_ctx):
        for d in range(2):
            cki_ref, cfirst_ref, csecond_ref = dirs[d][1]
            cj = n_ctx - 1 - j if dirs[d][2] else j
            d_second = csecond_ref[dec_rows(cj), :]
            decay = cfirst_ref[dec_rows(cj), :] * d_second
            for h in range(n_heads):
                ds_t = state_increment(cv_ref[rows(cj), lanes(h)], cki_ref[rows(cj), lanes(h)],
                                       d_second[0:1, lanes(h)])
                st_s[d, h] = st_s[d, h] * decay[0:1, lanes(h)] + ds_t

    def chunk_of(d, idx):
        return n_lat - 1 - idx if dirs[d][2] else idx

    og = og_ref[...]
    ones = jnp.ones((HG_DV, HG_DV), BF16)

    def readout(chunk):
        sl = rows(chunk)
        for h in range(n_heads):
            o = of_s[sl, lanes(h)] + ob_s[sl, lanes(h)]
            ss = jnp.dot((o * o).astype(BF16), ones, preferred_element_type=F32)
            o = o * lax.rsqrt(ss * (1.0 / HG_DV) + EPS)
            gh = g_ref[sl, lanes(h)].astype(F32)
            o_ref[sl, lanes(h)] = (o * og[:, lanes(h)]
                                   * (gh + gh * jnp.tanh(gh))).astype(o_ref.dtype)

    assert n_lat % 2 == 0
    products = [local(d, chunk_of(d, 0)) for d in range(2)]
    for idx in range(n_lat):
        for d in range(2):
            step(d, chunk_of(d, idx), products[d])
        if idx + 1 < n_lat:
            products = [local(d, chunk_of(d, idx + 1)) for d in range(2)]
        if 2 * idx >= n_lat:
            readout(idx)
            readout(n_lat - 1 - idx)


def _hgrn2_scan(lat_ops, ctx_ops, onorm_g, nb):
    groups, rows, w = lat_ops[-1].shape
    n = rows // nb

    def blocked(a):
        per_batch = a.shape[1] // nb
        return (a.reshape(groups, nb, per_batch, w),
                pl.BlockSpec((None, None, per_batch, w), lambda b, h: (h, b, 0, 0)))

    ops, specs = zip(*[blocked(a) for a in (*lat_ops, *ctx_ops)])
    vec = pl.BlockSpec((1, w), lambda b, h: (0, h))
    return pl.pallas_call(
        _scan_kernel,
        grid=(nb, groups),
        in_specs=list(specs) + [vec],
        out_specs=pl.BlockSpec((None, n, w), lambda b, h: (b, 0, h)),
        out_shape=jax.ShapeDtypeStruct((nb, n, HG_V), BF16),
        scratch_shapes=[pltpu.VMEM((2, SCAN_HEADS, HG_DV, HG_DK), F32),
                        pltpu.VMEM((n, w), F32),
                        pltpu.VMEM((n, w), F32)],
        compiler_params=pltpu.CompilerParams(
            dimension_semantics=("arbitrary", "arbitrary"), vmem_limit_bytes=VMEM_LIMIT),
        name="hgrn2_scan",
    )(*ops, onorm_g)


def _window(n, w):
    t = np.arange(n)
    lo = np.maximum(t - w // 2, 0)
    hi = np.minimum(t + w // 2 - 1, n - 1)
    s = np.arange(n)
    inside = (s[None, :] >= lo[:, None]) & (s[None, :] <= hi[:, None])
    return inside, (hi - lo + 1)


def _pool_operators(n, on_grid):
    mats, icnts, taps = [], [], []
    for w in POOL_WINDOWS:
        if on_grid:
            in_c, cnt_c = _window(GRID_W, w)
            _, cnt_r = _window(n // GRID_W, w)
            m = np.kron(np.eye(POOL_SLAB // GRID_W, dtype=bool), in_c)
            cnt = (cnt_r[:, None] * cnt_c[None, :]).reshape(n)
            taps.append((-(w // 2), w // 2 - 1))
        else:
            assert n == POOL_SLAB
            m, cnt = _window(n, w)
            taps.append((0, 0))
        mats.append(m)
        icnts.append(1.0 / cnt.astype(np.float64))
    colop = jnp.asarray(np.stack(mats).astype(np.float32), dtype=BF16)
    icnt = jnp.asarray(np.stack(icnts).astype(np.float32)[..., None])
    return colop, icnt, tuple(taps)


def kernel(x, c, ctx, c_ctx, ada_w, ada_b, pre_g, post_g, ev_w_in, ev_pool_w, ev_pool_scale,
           ev_conv_w, ev_conv_b, ev_w_out, od_w_in, od_onorm_g, od_w_out, lb_logits):
    nb, n, d = x.shape
    nc = ctx.shape[1]
    lat_row = lambda b: b
    ctx_row = lambda b: CTX_ROW

    lb_table = jnp.cumsum(jax.nn.softmax(lb_logits.astype(F32), axis=1), axis=1)

    cc = jnp.zeros((MOD_ROWS, d), F32).at[:nb].set(c).at[CTX_ROW].set(c_ctx)
    mod = _ada_table(cc, ada_w, ada_b)
    mod = mod.reshape(2, MOD_ROWS, 3, 1, d)
    shift = [mod[l, :, 0] for l in range(2)]
    scale = [mod[l, :, 1] for l in range(2)]
    gate = [mod[l, :, 2] for l in range(2)]

    ctx_flat = ctx.reshape(1, nb * nc, d)

    pool_w = ev_pool_w[0].astype(BF16)
    pool_scale = ev_pool_scale[0].reshape(1, POOL_W)
    conv_w = ev_conv_w[0]
    conv_b = ev_conv_b[0].reshape(1, CONV_W)
    pre0 = pre_g[0].reshape(1, d)
    post0 = post_g[0].reshape(1, d)

    pre1 = pre_g[1].reshape(1, d)
    post1 = post_g[1].reshape(1, d)
    norm1 = (shift[1], scale[1], pre1)

    def even_mixer(tokens3, mod_row, on_grid):
        b3, n3, _ = tokens3.shape
        h = _norm_tokens(tokens3, mod_row, shift[0], scale[0], pre0, TM_NORM)
        proj, = _inproj(h.reshape(b3 * n3, d), ev_w_in[0], EVEN_IN, [BF16],
                        min(TM_IN0, b3 * n3), 1024)
        n_tok = n if on_grid else nc
        proj = proj.reshape(nb, n_tok, EVEN_IN)
        colop, icnt, taps = _pool_operators(n_tok, on_grid)
        a_out = _pool_mixer(proj, colop, icnt, taps, pool_w, pool_scale)
        b_out = _conv_mixer(proj, conv_w, conv_b)
        acts = [a_out.reshape(b3, n3, POOL_W), b_out.reshape(b3, n3, CONV_W)]
        return _outproj(acts, ev_w_out[0], tokens3, mod_row, gate[0], post0, TM_OUT, norm1)

    x1, h1 = even_mixer(x, lat_row, True)
    _, hc1 = even_mixer(ctx_flat, ctx_row, False)

    lb_f = lb_table[0, 1].reshape(1, HG_K)
    lb_b = lb_table[1, 1].reshape(1, HG_K)
    lat_ops = _hgrn_inproj(h1.reshape(nb * n, d), od_w_in[0], lb_f, lb_b, True, TM_IN)
    ctx_ops = _hgrn_inproj(hc1.reshape(nb * nc, d), od_w_in[0], lb_f, lb_b, False, TM_IN)
    o = _hgrn2_scan(lat_ops, ctx_ops, od_onorm_g[0].reshape(1, HG_V), nb)
    return _outproj([o], od_w_out[0], x1, lat_row, gate[1], post1, TM_OUT)
```

```python
import functools

import numpy as np
import jax
import jax.numpy as jnp
from jax import lax
from jax.experimental import pallas as pl
from jax.experimental.pallas import tpu as pltpu

F32 = jnp.float32
BF16 = jnp.bfloat16

D_MODEL = 2048
BATCH = 4
SEQ = 2048
CTX_LEN = 256
GRID_W = 64
EPS = 1e-6

POOL_WINDOWS = (2, 4, 8, 16)
POOL_W = D_MODEL // 2
POOL_GROUP = POOL_W // len(POOL_WINDOWS)
CONV_W = D_MODEL // 2
EVEN_IN = 2 * POOL_W + 4 * CONV_W

HG_DK = 128
HG_HEADS = D_MODEL // HG_DK
HG_DV = D_MODEL // HG_HEADS
HG_K = HG_HEADS * HG_DK
HG_V = HG_HEADS * HG_DV
ODD_IN = 3 * HG_K + 2 * HG_V
CHUNK = 64

MOD_ROWS = 8
CTX_ROW = BATCH
VMEM_LIMIT = 56 * 1024 * 1024
TM_NORM = 1024
TM_IN0 = 2048
TM_IN = 1024
TM_OUT = 512


def _silu(v):
    hv = 0.5 * v
    return hv + hv * jnp.tanh(hv)


def _ada_kernel(cc_ref, w_ref, b_ref, o_ref):
    s = _silu(cc_ref[...])
    o_ref[...] = jnp.dot(s, w_ref[...], preferred_element_type=F32) + b_ref[...]


def _ada_table(cc, ada_w, ada_b):
    depth = ada_w.shape[0]
    tn = 1024
    return pl.pallas_call(
        _ada_kernel,
        grid=(depth, 3 * D_MODEL // tn),
        in_specs=[
            pl.BlockSpec((MOD_ROWS, D_MODEL), lambda l, j: (0, 0)),
            pl.BlockSpec((None, D_MODEL, tn), lambda l, j: (l, 0, j)),
            pl.BlockSpec((None, 1, tn), lambda l, j: (l, 0, j)),
        ],
        out_specs=pl.BlockSpec((None, MOD_ROWS, tn), lambda l, j: (l, 0, j)),
        out_shape=jax.ShapeDtypeStruct((depth, MOD_ROWS, 3 * D_MODEL), F32),
        compiler_params=pltpu.CompilerParams(
            dimension_semantics=("arbitrary", "arbitrary"), vmem_limit_bytes=VMEM_LIMIT),
        name="ada_table",
    )(cc, ada_w, ada_b.reshape(depth, 1, 3 * D_MODEL))


def _norm_mod(x, g, scale, shift):
    ms = jnp.mean(x * x, axis=-1, keepdims=True)
    y = x * lax.rsqrt(ms + EPS) * g
    return y * (1.0 + scale) + shift


def _norm_kernel(x_ref, shift_ref, scale_ref, g_ref, o_ref):
    o_ref[...] = _norm_mod(x_ref[...], g_ref[...], scale_ref[...],
                           shift_ref[...]).astype(o_ref.dtype)


def _norm_tokens(x3, mod_row_of_batch, shift, scale, g, tm):
    nb, rows, _ = x3.shape
    mod_map = lambda b, i: (mod_row_of_batch(b), 0, 0)
    return pl.pallas_call(
        _norm_kernel,
        grid=(nb, rows // tm),
        in_specs=[
            pl.BlockSpec((None, tm, D_MODEL), lambda b, i: (b, i, 0)),
            pl.BlockSpec((None, 1, D_MODEL), mod_map),
            pl.BlockSpec((None, 1, D_MODEL), mod_map),
            pl.BlockSpec((1, D_MODEL), lambda b, i: (0, 0)),
        ],
        out_specs=pl.BlockSpec((None, tm, D_MODEL), lambda b, i: (b, i, 0)),
        out_shape=jax.ShapeDtypeStruct((nb, rows, D_MODEL), BF16),
        compiler_params=pltpu.CompilerParams(
            dimension_semantics=("arbitrary", "arbitrary"), vmem_limit_bytes=VMEM_LIMIT),
        name="norm_tokens",
    )(x3, shift, scale, g)


def _inproj_kernel(n_seg, h_ref, *refs):
    w_refs = refs[:n_seg]
    o_refs = refs[n_seg:2 * n_seg]
    wb_refs = refs[2 * n_seg:]

    @pl.when(pl.program_id(1) == 0)
    def _():
        for w_ref, wb_ref in zip(w_refs, wb_refs):
            wb_ref[...] = w_ref[...].astype(BF16)

    for wb_ref, o_ref in zip(wb_refs, o_refs):
        o_ref[...] = jnp.dot(h_ref[...], wb_ref[...],
                             preferred_element_type=F32).astype(o_ref.dtype)


def _inproj(h, w, seg_width, seg_dtypes, tm, tn):
    rows = h.shape[0]
    n_seg = len(seg_dtypes)
    n_tiles = seg_width // tn
    w_specs = [pl.BlockSpec((D_MODEL, tn), functools.partial(
        lambda k, j, i: (0, k * n_tiles + j), k)) for k in range(n_seg)]
    return pl.pallas_call(
        functools.partial(_inproj_kernel, n_seg),
        grid=(n_tiles, rows // tm),
        in_specs=[pl.BlockSpec((tm, D_MODEL), lambda j, i: (i, 0))] + w_specs,
        out_specs=[pl.BlockSpec((tm, tn), lambda j, i: (i, j))] * n_seg,
        out_shape=[jax.ShapeDtypeStruct((rows, seg_width), dt) for dt in seg_dtypes],
        scratch_shapes=[pltpu.VMEM((D_MODEL, tn), BF16)] * n_seg,
        compiler_params=pltpu.CompilerParams(
            dimension_semantics=("arbitrary", "arbitrary"), vmem_limit_bytes=VMEM_LIMIT),
        name="inproj",
    )(h, *([w] * n_seg))


POOL_SLAB = 256
POOL_ROWS = 128


def _pool_kernel(taps, pad, colop_ref, icnt_ref, v_ref, gate_ref, w_ref, sc_ref, o_ref,
                 box_s, pooled_s):
    n = v_ref.shape[0]
    for gi, (lo, hi) in enumerate(taps):
        lanes = slice(gi * POOL_GROUP, (gi + 1) * POOL_GROUP)
        if pad:
            box_s[gi, 0:pad, :] = jnp.zeros((pad, POOL_GROUP), F32)
            box_s[gi, pad + n:pad + n + pad, :] = jnp.zeros((pad, POOL_GROUP), F32)
        for s in range(n // POOL_SLAB):
            tok = slice(s * POOL_SLAB, (s + 1) * POOL_SLAB)
            box_s[gi, pad + s * POOL_SLAB:pad + (s + 1) * POOL_SLAB, :] = jnp.dot(
                colop_ref[gi], v_ref[tok, lanes], preferred_element_type=F32)

        for r0 in range(0, n, POOL_ROWS):
            tok = slice(r0, r0 + POOL_ROWS)
            first = pad + lo * GRID_W + r0
            acc = box_s[gi, first:first + POOL_ROWS, :]
            for dlt in range(1, hi - lo + 1):
                acc = acc + box_s[gi, first + dlt * GRID_W:first + dlt * GRID_W + POOL_ROWS, :]
            pooled = acc * icnt_ref[gi, tok, :] - v_ref[tok, lanes].astype(F32)
            pooled_s[gi, tok, :] = pooled.astype(BF16)
        mixed = jnp.dot(pooled_s[gi], w_ref[gi], preferred_element_type=F32)
        o_ref[:, lanes] = (mixed * sc_ref[:, lanes]
                           * _silu(gate_ref[:, lanes].astype(F32))).astype(o_ref.dtype)


def _pool_mixer(proj, colop, icnt, taps, pool_w, pool_scale):
    nb, n, _ = proj.shape
    ng = len(POOL_WINDOWS)
    pad = max(max(-lo, hi) for lo, hi in taps) * GRID_W
    pad = -(-pad // POOL_ROWS) * POOL_ROWS
    return pl.pallas_call(
        functools.partial(_pool_kernel, taps, pad),
        grid=(nb,),
        in_specs=[
            pl.BlockSpec((ng, POOL_SLAB, POOL_SLAB), lambda b: (0, 0, 0)),
            pl.BlockSpec((ng, n, 1), lambda b: (0, 0, 0)),
            pl.BlockSpec((None, n, POOL_W), lambda b: (b, 0, 0)),
            pl.BlockSpec((None, n, POOL_W), lambda b: (b, 0, 1)),
            pl.BlockSpec((ng, POOL_GROUP, POOL_GROUP), lambda b: (0, 0, 0)),
            pl.BlockSpec((1, POOL_W), lambda b: (0, 0)),
        ],
        out_specs=pl.BlockSpec((None, n, POOL_W), lambda b: (b, 0, 0)),
        out_shape=jax.ShapeDtypeStruct((nb, n, POOL_W), BF16),
        scratch_shapes=[pltpu.VMEM((ng, n + 2 * pad, POOL_GROUP), F32),
                        pltpu.VMEM((ng, n, POOL_GROUP), BF16)],
        compiler_params=pltpu.CompilerParams(
            dimension_semantics=("arbitrary",), vmem_limit_bytes=VMEM_LIMIT),
        name="pool_mixer",
    )(colop, icnt, proj, proj, pool_w, pool_scale)


def _conv_kernel(x_ref, b_ref, c_ref, gate_ref, cw_ref, cb_ref, o_ref):
    n = x_ref.shape[0]
    u = c_ref[...].astype(F32) * x_ref[...].astype(F32)
    row = lax.broadcasted_iota(jnp.int32, u.shape, 0)
    u_prev = jnp.where(row == 0, 0.0, pltpu.roll(u, 1, axis=0))
    u_next = jnp.where(row == n - 1, 0.0, pltpu.roll(u, n - 1, axis=0))
    cw = cw_ref[...]
    conv = u_prev * cw[0:1, :] + u * cw[1:2, :] + u_next * cw[2:3, :] + cb_ref[...]
    o_ref[...] = (b_ref[...].astype(F32) * conv
                  * _silu(gate_ref[...].astype(F32))).astype(o_ref.dtype)


def _conv_mixer(proj, conv_w, conv_b):
    nb, n, _ = proj.shape
    tc = 256
    nblk = CONV_W // tc
    base = 2 * POOL_W // tc

    def seg(k):
        return pl.BlockSpec((None, n, tc), lambda b, j: (b, 0, base + k * nblk + j))

    return pl.pallas_call(
        _conv_kernel,
        grid=(nb, nblk),
        in_specs=[
            seg(0), seg(1), seg(2), seg(3),
            pl.BlockSpec((3, tc), lambda b, j: (0, j)),
            pl.BlockSpec((1, tc), lambda b, j: (0, j)),
        ],
        out_specs=pl.BlockSpec((None, n, tc), lambda b, j: (b, 0, j)),
        out_shape=jax.ShapeDtypeStruct((nb, n, CONV_W), BF16),
        compiler_params=pltpu.CompilerParams(
            dimension_semantics=("arbitrary", "arbitrary"), vmem_limit_bytes=VMEM_LIMIT),
        name="conv_mixer",
    )(proj, proj, proj, proj, conv_w, conv_b)


def _outproj_kernel(n_in, emit_next, *refs):
    a_refs = refs[:n_in]
    w_refs = refs[n_in:2 * n_in]
    x_ref, gate_ref, pg_ref = refs[2 * n_in:2 * n_in + 3]
    rest = refs[2 * n_in + 3:]
    if emit_next:
        shift_ref, scale_ref, g_ref, o_ref, h_ref = rest[:5]
        wb_refs = rest[5:]
    else:
        o_ref = rest[0]
        wb_refs = rest[1:]

    @pl.when((pl.program_id(0) == 0) & (pl.program_id(1) == 0))
    def _():
        for w_ref, wb_ref in zip(w_refs, wb_refs):
            wb_ref[...] = w_ref[...].astype(BF16)

    y = jnp.dot(a_refs[0][...], wb_refs[0][...], preferred_element_type=F32)
    for a_ref, wb_ref in zip(a_refs[1:], wb_refs[1:]):
        y = y + jnp.dot(a_ref[...], wb_ref[...], preferred_element_type=F32)
    ms = jnp.mean(y * y, axis=-1, keepdims=True)
    r = y * lax.rsqrt(ms + EPS) * pg_ref[...]
    x_new = x_ref[...] + gate_ref[...] * r
    o_ref[...] = x_new
    if emit_next:
        h_ref[...] = _norm_mod(x_new, g_ref[...], scale_ref[...],
                               shift_ref[...]).astype(h_ref.dtype)


def _outproj(acts, w, x3, mod_row_of_batch, gate, pg, tm, next_norm=None):
    nb, rows, _ = x3.shape
    n_in = len(acts)
    k_in = acts[0].shape[-1]
    assert all(a.shape[-1] == k_in for a in acts) and n_in * k_in == w.shape[0]
    emit_next = next_norm is not None
    mod_spec = pl.BlockSpec((None, 1, D_MODEL), lambda b, i: (mod_row_of_batch(b), 0, 0))
    vec_spec = pl.BlockSpec((1, D_MODEL), lambda b, i: (0, 0))
    tok_spec = pl.BlockSpec((None, tm, D_MODEL), lambda b, i: (b, i, 0))
    in_specs = [pl.BlockSpec((None, tm, k_in), lambda b, i: (b, i, 0))] * n_in
    in_specs += [pl.BlockSpec((k_in, D_MODEL), functools.partial(lambda k, b, i: (k, 0), k),
                              pipeline_mode=pl.Buffered(1)) for k in range(n_in)]
    in_specs += [tok_spec, mod_spec, vec_spec]
    operands = [*acts, *([w] * n_in), x3, gate, pg]
    out_specs, out_shape = tok_spec, jax.ShapeDtypeStruct((nb, rows, D_MODEL), F32)
    if emit_next:
        in_specs += [mod_spec, mod_spec, vec_spec]
        operands += list(next_norm)
        out_specs = [tok_spec, tok_spec]
        out_shape = [out_shape, jax.ShapeDtypeStruct((nb, rows, D_MODEL), BF16)]
    return pl.pallas_call(
        functools.partial(_outproj_kernel, n_in, emit_next),
        grid=(nb, rows // tm),
        in_specs=in_specs,
        out_specs=out_specs,
        out_shape=out_shape,
        scratch_shapes=[pltpu.VMEM((k_in, D_MODEL), BF16)] * n_in,
        compiler_params=pltpu.CompilerParams(
            dimension_semantics=("arbitrary", "arbitrary"), vmem_limit_bytes=VMEM_LIMIT),
        name="outproj",
    )(*operands)


_NT = (((1,), (1,)), ((), ()))
_TN = (((0,), (0,)), ((), ()))
SUBLANES = 8
SCAN_HEADS = 2
SCAN_GROUPS = 2
SCAN_CHUNK = 2 * CHUNK


def _chunk_cumsum(x, reverse, in_ref, out_ref):
    c, w = x.shape
    nblk = c // SUBLANES
    order = range(nblk - 1, -1, -1) if reverse else range(nblk)
    in_ref[...] = x
    pref = [None] * nblk
    acc = None
    for j in order:
        blk = in_ref[pl.ds(j, SUBLANES, stride=nblk), :]
        acc = blk if acc is None else acc + blk
        pref[j] = acc
    total = acc
    sub = lax.broadcasted_iota(jnp.int32, (SUBLANES, w), 0)
    incl = total
    for s in (1, 2, 4):
        if reverse:
            incl = incl + jnp.where(sub < SUBLANES - s,
                                    pltpu.roll(incl, SUBLANES - s, axis=0), 0.0)
        else:
            incl = incl + jnp.where(sub >= s, pltpu.roll(incl, s, axis=0), 0.0)
    before = incl - total
    for j in range(nblk):
        out_ref[pl.ds(j, SUBLANES, stride=nblk), :] = pref[j] + before
    edge = incl[0:1, :] if reverse else incl[SUBLANES - 1:SUBLANES, :]
    return out_ref[...], jnp.broadcast_to(edge, (SUBLANES, w))


def _gate_decay(zh, lb, reverse, cs_ref):
    bt = (0.5 * (1.0 - lb)) * jnp.tanh(zh)
    k = 0.5 * (1.0 - lb) - bt
    lf = jnp.log(0.5 * (1.0 + lb) + bt)
    lo, hi = lf[:CHUNK], lf[CHUNK:]
    first, second = (hi, lo) if reverse else (lo, hi)
    away, t_first = _chunk_cumsum(first, not reverse, cs_ref.at[0], cs_ref.at[1])
    toward, t_second = _chunk_cumsum(second, reverse, cs_ref.at[2], cs_ref.at[3])
    a_first = first - away
    a = jnp.concatenate([toward, a_first] if reverse else [a_first, toward], axis=0)
    k_inv = (k * jnp.exp(-a)).astype(BF16)
    return jnp.exp(a), k_inv, jnp.exp(t_first), jnp.exp(t_second)


def _hgrn_inproj_kernel(with_queries, h_ref, lbf_ref, lbb_ref, *refs):
    n_w = 5 if with_queries else 3
    n_dir = 4 if with_queries else 3
    w_refs, rest = refs[:n_w], refs[n_w:]
    per_dir = (rest[0:n_dir], rest[n_dir:2 * n_dir])
    rest = rest[2 * n_dir:]
    if with_queries:
        v_ref, g_ref = rest[:2]
        rest = rest[2:]
    else:
        v_ref = rest[0]
        rest = rest[1:]
    wb_refs, cs_s = rest[:n_w], rest[n_w]

    @pl.when(pl.program_id(1) == 0)
    def _():
        for k, (w_ref, wb_ref) in enumerate(zip(w_refs, wb_refs)):
            wk = w_ref[...]
            wb_ref[...] = (wk if k in (2, 3) else wk * 0.5).astype(BF16)

    h = h_ref[...]

    def proj(k):
        return jnp.dot(h, wb_refs[k][...], preferred_element_type=F32)

    tm, w = v_ref.shape
    n_chunks = tm // SCAN_CHUNK
    n_heads = w // HG_DK
    zhs = [proj(0), proj(1)]
    if with_queries:
        q = proj(3)
    for d, (lb_ref, reverse) in enumerate(((lbf_ref, False), (lbb_ref, True))):
        zh = zhs[d]
        lb = lb_ref[...]
        ki_ref, first_ref, second_ref = per_dir[d][-3:]
        for c in range(n_chunks):
            rows = slice(c * SCAN_CHUNK, (c + 1) * SCAN_CHUNK)
            drows = slice(c * SUBLANES, (c + 1) * SUBLANES)
            for hd in range(n_heads):
                lanes = slice(hd * HG_DK, (hd + 1) * HG_DK)
                slot = (d * n_chunks + c) * n_heads + hd
                e_a, k_inv, d_first, d_second = _gate_decay(
                    zh[rows, lanes], lb[:, lanes], reverse, cs_s.at[slot])
                if with_queries:
                    per_dir[d][0][rows, lanes] = (q[rows, lanes] * e_a).astype(BF16)
                ki_ref[rows, lanes] = k_inv
                first_ref[drows, lanes] = d_first
                second_ref[drows, lanes] = d_second
    v_ref[...] = proj(2).astype(BF16)
    if with_queries:
        g_ref[...] = proj(4).astype(BF16)


def _hgrn_inproj(h, w, lb_f, lb_b, with_queries, tm):
    rows = h.shape[0]
    tn = SCAN_HEADS * HG_DK
    groups = HG_K // tn
    n_w = 5 if with_queries else 3
    w_specs = [pl.BlockSpec((D_MODEL, tn), functools.partial(
        lambda k, j, i: (0, k * groups + j), k)) for k in range(n_w)]
    lb_spec = pl.BlockSpec((1, tn), lambda j, i: (0, j))
    dec_rows = tm // SCAN_CHUNK * SUBLANES
    tok = (pl.BlockSpec((None, tm, tn), lambda j, i: (j, i, 0)),
           jax.ShapeDtypeStruct((groups, rows, tn), BF16))
    dec = (pl.BlockSpec((None, dec_rows, tn), lambda j, i: (j, i, 0)),
           jax.ShapeDtypeStruct((groups, rows // SCAN_CHUNK * SUBLANES, tn), F32))
    one_dir = [tok, tok, dec, dec] if with_queries else [tok, dec, dec]
    outs = one_dir + one_dir + ([tok, tok] if with_queries else [tok])
    n_slots = 2 * (tm // SCAN_CHUNK) * SCAN_HEADS
    return pl.pallas_call(
        functools.partial(_hgrn_inproj_kernel, with_queries),
        grid=(groups, rows // tm),
        in_specs=[pl.BlockSpec((tm, D_MODEL), lambda j, i: (i, 0)), lb_spec, lb_spec] + w_specs,
        out_specs=[o[0] for o in outs],
        out_shape=[o[1] for o in outs],
        scratch_shapes=[pltpu.VMEM((D_MODEL, tn), BF16)] * n_w
        + [pltpu.VMEM((n_slots, 4, CHUNK, HG_DK), F32)],
        compiler_params=pltpu.CompilerParams(
            dimension_semantics=("arbitrary", "arbitrary"), vmem_limit_bytes=VMEM_LIMIT),
        name="hgrn_inproj",
    )(h, lb_f, lb_b, *([w] * n_w))


def _scan_kernel(qdf_ref, kif_ref, firstf_ref, secondf_ref,
                 qdb_ref, kib_ref, firstb_ref, secondb_ref, v_ref, g_ref,
                 ckif_ref, cfirstf_ref, csecondf_ref, ckib_ref, cfirstb_ref, csecondb_ref,
                 cv_ref, og_ref, o_ref, st_s, of_s, ob_s):
    c = SCAN_CHUNK
    n_lat = v_ref.shape[1] // c
    n_ctx = cv_ref.shape[1] // c
    heads_per_group = v_ref.shape[2] // HG_DK
    n_heads = v_ref.shape[0] * heads_per_group
    row = lax.broadcasted_iota(jnp.int32, (c, c), 0)
    col = lax.broadcasted_iota(jnp.int32, (c, c), 1)
    dirs = (((qdf_ref, kif_ref, firstf_ref, secondf_ref),
             (ckif_ref, cfirstf_ref, csecondf_ref), False, col <= row, of_s),
            ((qdb_ref, kib_ref, firstb_ref, secondb_ref),
             (ckib_ref, cfirstb_ref, csecondb_ref), True, col >= row, ob_s))

    def lanes(h):
        return slice(h * HG_DK, (h + 1) * HG_DK)

    def rows(chunk):
        return slice(chunk * c, (chunk + 1) * c)

    def tokens(ref, h, chunk):
        return ref[h // heads_per_group, rows(chunk), lanes(h % heads_per_group)]

    def decay_row(ref, h, chunk):
        return ref[h // heads_per_group, chunk * SUBLANES:chunk * SUBLANES + 1,
                   lanes(h % heads_per_group)]

    def state_increment(v, k_inv, d_second):
        k_dec = k_inv * d_second.astype(BF16)
        return lax.dot_general(v, k_dec, _TN, preferred_element_type=F32)

    def local(d, chunk):
        qd_ref, ki_ref, _, second_ref = dirs[d][0]
        mask = dirs[d][3]
        out = []
        for h in range(n_heads):
            k_inv = tokens(ki_ref, h, chunk)
            sc = lax.dot_general(tokens(qd_ref, h, chunk), k_inv, _NT,
                                 preferred_element_type=F32)
            ds_t = state_increment(tokens(v_ref, h, chunk), k_inv,
                                   decay_row(second_ref, h, chunk))
            out.append((jnp.where(mask, sc, 0.0).astype(BF16), ds_t))
        return out

    def step(d, chunk, products):
        qd_ref, _, first_ref, second_ref = dirs[d][0]
        out_s = dirs[d][4]
        for h in range(n_heads):
            d_first = decay_row(first_ref, h, chunk)
            st_mid = (st_s[d, h] * d_first).astype(BF16)
            inter = lax.dot_general(tokens(qd_ref, h, chunk), st_mid, _NT,
                                    preferred_element_type=F32)
            sc, ds_t = products[h]
            out_s[rows(chunk), lanes(h)] = inter + jnp.dot(
                sc, tokens(v_ref, h, chunk), preferred_element_type=F32)
            st_s[d, h] = st_s[d, h] * (d_first * decay_row(second_ref, h, chunk)) + ds_t

    st_s[...] = jnp.zeros_like(st_s)
    for j in range(n_ctx):
        for d in range(2):
            cki_ref, cfirst_ref, csecond_ref = dirs[d][1]
            cj = n_ctx - 1 - j if dirs[d][2] else j
            for h in range(n_heads):
                d_second = decay_row(csecond_ref, h, cj)
                ds_t = state_increment(tokens(cv_ref, h, cj), tokens(cki_ref, h, cj), d_second)
                st_s[d, h] = st_s[d, h] * (decay_row(cfirst_ref, h, cj) * d_second) + ds_t

    def chunk_of(d, idx):
        return n_lat - 1 - idx if dirs[d][2] else idx

    og = og_ref[...]
    ones = jnp.ones((HG_DV, HG_DV), BF16)

    def readout(chunk):
        sl = rows(chunk)
        for h in range(n_heads):
            o = of_s[sl, lanes(h)] + ob_s[sl, lanes(h)]
            ss = jnp.dot((o * o).astype(BF16), ones, preferred_element_type=F32)
            o = o * lax.rsqrt(ss * (1.0 / HG_DV) + EPS)
            gh = tokens(g_ref, h, chunk).astype(F32)
            o_ref[sl, lanes(h)] = (o * og[:, lanes(h)]
                                   * (gh + gh * jnp.tanh(gh))).astype(o_ref.dtype)

    assert n_lat % 2 == 0
    products = [local(d, chunk_of(d, 0)) for d in range(2)]
    for idx in range(n_lat):
        for d in range(2):
            step(d, chunk_of(d, idx), products[d])
        if idx + 1 < n_lat:
            products = [local(d, chunk_of(d, idx + 1)) for d in range(2)]
        if 2 * idx >= n_lat:
            readout(idx)
            readout(n_lat - 1 - idx)


def _hgrn2_scan(lat_ops, ctx_ops, onorm_g, nb):
    groups, rows, w = lat_ops[-1].shape
    n = rows // nb

    def blocked(a):
        per_batch = a.shape[1] // nb
        return (a.reshape(groups, nb, per_batch, w),
                pl.BlockSpec((SCAN_GROUPS, None, per_batch, w), lambda b, h: (h, b, 0, 0)))

    ops, specs = zip(*[blocked(a) for a in (*lat_ops, *ctx_ops)])
    wide = SCAN_GROUPS * w
    vec = pl.BlockSpec((1, wide), lambda b, h: (0, h))
    return pl.pallas_call(
        _scan_kernel,
        grid=(nb, groups // SCAN_GROUPS),
        in_specs=list(specs) + [vec],
        out_specs=pl.BlockSpec((None, n, wide), lambda b, h: (b, 0, h)),
        out_shape=jax.ShapeDtypeStruct((nb, n, HG_V), BF16),
        scratch_shapes=[pltpu.VMEM((2, wide // HG_DK, HG_DV, HG_DK), F32),
                        pltpu.VMEM((n, wide), F32),
                        pltpu.VMEM((n, wide), F32)],
        compiler_params=pltpu.CompilerParams(
            dimension_semantics=("arbitrary", "arbitrary"), vmem_limit_bytes=VMEM_LIMIT),
        name="hgrn2_scan",
    )(*ops, onorm_g)


def _window(n, w):
    t = np.arange(n)
    lo = np.maximum(t - w // 2, 0)
    hi = np.minimum(t + w // 2 - 1, n - 1)
    s = np.arange(n)
    inside = (s[None, :] >= lo[:, None]) & (s[None, :] <= hi[:, None])
    return inside, (hi - lo + 1)


def _pool_operators(n, on_grid):
    mats, icnts, taps = [], [], []
    for w in POOL_WINDOWS:
        if on_grid:
            in_c, cnt_c = _window(GRID_W, w)
            _, cnt_r = _window(n // GRID_W, w)
            m = np.kron(np.eye(POOL_SLAB // GRID_W, dtype=bool), in_c)
            cnt = (cnt_r[:, None] * cnt_c[None, :]).reshape(n)
            taps.append((-(w // 2), w // 2 - 1))
        else:
            assert n == POOL_SLAB
            m, cnt = _window(n, w)
            taps.append((0, 0))
        mats.append(m)
        icnts.append(1.0 / cnt.astype(np.float64))
    colop = jnp.asarray(np.stack(mats).astype(np.float32), dtype=BF16)
    icnt = jnp.asarray(np.stack(icnts).astype(np.float32)[..., None])
    return colop, icnt, tuple(taps)


def kernel(x, c, ctx, c_ctx, ada_w, ada_b, pre_g, post_g, ev_w_in, ev_pool_w, ev_pool_scale,
           ev_conv_w, ev_conv_b, ev_w_out, od_w_in, od_onorm_g, od_w_out, lb_logits):
    nb, n, d = x.shape
    nc = ctx.shape[1]
    lat_row = lambda b: b
    ctx_row = lambda b: CTX_ROW

    lb_table = jnp.cumsum(jax.nn.softmax(lb_logits.astype(F32), axis=1), axis=1)

    cc = jnp.zeros((MOD_ROWS, d), F32).at[:nb].set(c).at[CTX_ROW].set(c_ctx)
    mod = _ada_table(cc, ada_w, ada_b)
    mod = mod.reshape(2, MOD_ROWS, 3, 1, d)
    shift = [mod[l, :, 0] for l in range(2)]
    scale = [mod[l, :, 1] for l in range(2)]
    gate = [mod[l, :, 2] for l in range(2)]

    ctx_flat = ctx.reshape(1, nb * nc, d)

    pool_w = ev_pool_w[0].astype(BF16)
    pool_scale = ev_pool_scale[0].reshape(1, POOL_W)
    conv_w = ev_conv_w[0]
    conv_b = ev_conv_b[0].reshape(1, CONV_W)
    pre0 = pre_g[0].reshape(1, d)
    post0 = post_g[0].reshape(1, d)

    pre1 = pre_g[1].reshape(1, d)
    post1 = post_g[1].reshape(1, d)
    norm1 = (shift[1], scale[1], pre1)

    def even_mixer(tokens3, mod_row, on_grid):
        b3, n3, _ = tokens3.shape
        h = _norm_tokens(tokens3, mod_row, shift[0], scale[0], pre0, TM_NORM)
        proj, = _inproj(h.reshape(b3 * n3, d), ev_w_in[0], EVEN_IN, [BF16],
                        min(TM_IN0, b3 * n3), 1024)
        n_tok = n if on_grid else nc
        proj = proj.reshape(nb, n_tok, EVEN_IN)
        colop, icnt, taps = _pool_operators(n_tok, on_grid)
        a_out = _pool_mixer(proj, colop, icnt, taps, pool_w, pool_scale)
        b_out = _conv_mixer(proj, conv_w, conv_b)
        acts = [a_out.reshape(b3, n3, POOL_W), b_out.reshape(b3, n3, CONV_W)]
        return _outproj(acts, ev_w_out[0], tokens3, mod_row, gate[0], post0, TM_OUT, norm1)

    x1, h1 = even_mixer(x, lat_row, True)
    _, hc1 = even_mixer(ctx_flat, ctx_row, False)

    lb_f = lb_table[0, 1].reshape(1, HG_K)
    lb_b = lb_table[1, 1].reshape(1, HG_K)
    lat_ops = _hgrn_inproj(h1.reshape(nb * n, d), od_w_in[0], lb_f, lb_b, True, TM_IN)
    ctx_ops = _hgrn_inproj(hc1.reshape(nb * nc, d), od_w_in[0], lb_f, lb_b, False, TM_IN)
    o = _hgrn2_scan(lat_ops, ctx_ops, od_onorm_g[0].reshape(1, HG_V), nb)
    return _outproj([o], od_w_out[0], x1, lat_row, gate[1], post1, TM_OUT)
```

```python
import functools

import numpy as np
import jax
import jax.numpy as jnp
from jax import lax
from jax.experimental import pallas as pl
from jax.experimental.pallas import tpu as pltpu

F32 = jnp.float32
BF16 = jnp.bfloat16

D_MODEL = 2048
BATCH = 4
GRID_W = 64
EPS = 1e-6

POOL_WINDOWS = (2, 4, 8, 16)
POOL_W = D_MODEL // 2
POOL_GROUP = POOL_W // len(POOL_WINDOWS)
CONV_W = D_MODEL // 2
EVEN_IN = 2 * POOL_W + 4 * CONV_W

HG_DK = 128
HG_HEADS = D_MODEL // HG_DK
HG_DV = D_MODEL // HG_HEADS
HG_K = HG_HEADS * HG_DK
HG_V = HG_HEADS * HG_DV
ODD_IN = 3 * HG_K + 2 * HG_V
CHUNK = 64

MOD_ROWS = 8
CTX_ROW = BATCH
V7X_VMEM_BYTES = 64 * 1024 * 1024
VMEM_LIMIT = V7X_VMEM_BYTES * 7 // 8
TM_NORM = 1024
TM_IN0 = 2048
TM_IN = 1024
TM_OUT = 512
TN_ADA = 1024
TN_IN0 = 1024
TC_CONV = 256


def _silu(v):
    hv = 0.5 * v
    return hv + hv * jnp.tanh(hv)


def _ada_kernel(cc_ref, w_ref, b_ref, o_ref):
    s = _silu(cc_ref[...])
    o_ref[...] = jnp.dot(s, w_ref[...], preferred_element_type=F32) + b_ref[...]


def _ada_table(cc, ada_w, ada_b):
    depth = ada_w.shape[0]
    tn = TN_ADA
    return pl.pallas_call(
        _ada_kernel,
        grid=(depth, 3 * D_MODEL // tn),
        in_specs=[
            pl.BlockSpec((MOD_ROWS, D_MODEL), lambda l, j: (0, 0)),
            pl.BlockSpec((None, D_MODEL, tn), lambda l, j: (l, 0, j)),
            pl.BlockSpec((None, 1, tn), lambda l, j: (l, 0, j)),
        ],
        out_specs=pl.BlockSpec((None, MOD_ROWS, tn), lambda l, j: (l, 0, j)),
        out_shape=jax.ShapeDtypeStruct((depth, MOD_ROWS, 3 * D_MODEL), F32),
        compiler_params=pltpu.CompilerParams(
            dimension_semantics=("arbitrary", "arbitrary"), vmem_limit_bytes=VMEM_LIMIT),
        name="ada_table",
    )(cc, ada_w, ada_b.reshape(depth, 1, 3 * D_MODEL))


def _norm_mod(x, g, scale, shift):
    ms = jnp.mean(x * x, axis=-1, keepdims=True)
    y = x * lax.rsqrt(ms + EPS) * g
    return y * (1.0 + scale) + shift


def _norm_kernel(x_ref, shift_ref, scale_ref, g_ref, o_ref):
    o_ref[...] = _norm_mod(x_ref[...], g_ref[...], scale_ref[...],
                           shift_ref[...]).astype(o_ref.dtype)


def _norm_tokens(x3, mod_row_of_batch, shift, scale, g, tm):
    nb, rows, _ = x3.shape
    mod_map = lambda b, i: (mod_row_of_batch(b), 0, 0)
    return pl.pallas_call(
        _norm_kernel,
        grid=(nb, rows // tm),
        in_specs=[
            pl.BlockSpec((None, tm, D_MODEL), lambda b, i: (b, i, 0)),
            pl.BlockSpec((None, 1, D_MODEL), mod_map),
            pl.BlockSpec((None, 1, D_MODEL), mod_map),
            pl.BlockSpec((1, D_MODEL), lambda b, i: (0, 0)),
        ],
        out_specs=pl.BlockSpec((None, tm, D_MODEL), lambda b, i: (b, i, 0)),
        out_shape=jax.ShapeDtypeStruct((nb, rows, D_MODEL), BF16),
        compiler_params=pltpu.CompilerParams(
            dimension_semantics=("arbitrary", "arbitrary"), vmem_limit_bytes=VMEM_LIMIT),
        name="norm_tokens",
    )(x3, shift, scale, g)


def _inproj_kernel(n_seg, h_ref, *refs):
    w_refs = refs[:n_seg]
    o_refs = refs[n_seg:2 * n_seg]
    wb_refs = refs[2 * n_seg:]

    @pl.when(pl.program_id(1) == 0)
    def _():
        for w_ref, wb_ref in zip(w_refs, wb_refs):
            wb_ref[...] = w_ref[...].astype(BF16)

    for wb_ref, o_ref in zip(wb_refs, o_refs):
        o_ref[...] = jnp.dot(h_ref[...], wb_ref[...],
                             preferred_element_type=F32).astype(o_ref.dtype)


def _inproj(h, w, seg_width, seg_dtypes, tm, tn):
    rows = h.shape[0]
    n_seg = len(seg_dtypes)
    n_tiles = seg_width // tn
    w_specs = [pl.BlockSpec((D_MODEL, tn), functools.partial(
        lambda k, j, i: (0, k * n_tiles + j), k)) for k in range(n_seg)]
    return pl.pallas_call(
        functools.partial(_inproj_kernel, n_seg),
        grid=(n_tiles, rows // tm),
        in_specs=[pl.BlockSpec((tm, D_MODEL), lambda j, i: (i, 0))] + w_specs,
        out_specs=[pl.BlockSpec((tm, tn), lambda j, i: (i, j))] * n_seg,
        out_shape=[jax.ShapeDtypeStruct((rows, seg_width), dt) for dt in seg_dtypes],
        scratch_shapes=[pltpu.VMEM((D_MODEL, tn), BF16)] * n_seg,
        compiler_params=pltpu.CompilerParams(
            dimension_semantics=("arbitrary", "arbitrary"), vmem_limit_bytes=VMEM_LIMIT),
        name="inproj",
    )(h, *([w] * n_seg))


POOL_SLAB = 256
POOL_ROWS = 128


def _pool_kernel(taps, pad, colop_ref, icnt_ref, v_ref, gate_ref, w_ref, sc_ref, o_ref,
                 box_s, pooled_s):
    n = v_ref.shape[0]
    for gi, (lo, hi) in enumerate(taps):
        lanes = slice(gi * POOL_GROUP, (gi + 1) * POOL_GROUP)
        if pad:
            box_s[gi, 0:pad, :] = jnp.zeros((pad, POOL_GROUP), F32)
            box_s[gi, pad + n:pad + n + pad, :] = jnp.zeros((pad, POOL_GROUP), F32)
        for s in range(n // POOL_SLAB):
            tok = slice(s * POOL_SLAB, (s + 1) * POOL_SLAB)
            box_s[gi, pad + s * POOL_SLAB:pad + (s + 1) * POOL_SLAB, :] = jnp.dot(
                colop_ref[gi], v_ref[tok, lanes], preferred_element_type=F32)

        for r0 in range(0, n, POOL_ROWS):
            tok = slice(r0, r0 + POOL_ROWS)
            first = pad + lo * GRID_W + r0
            acc = box_s[gi, first:first + POOL_ROWS, :]
            for dlt in range(1, hi - lo + 1):
                acc = acc + box_s[gi, first + dlt * GRID_W:first + dlt * GRID_W + POOL_ROWS, :]
            pooled = acc * icnt_ref[gi, tok, :] - v_ref[tok, lanes].astype(F32)
            pooled_s[gi, tok, :] = pooled.astype(BF16)
        mixed = jnp.dot(pooled_s[gi], w_ref[gi], preferred_element_type=F32)
        o_ref[:, lanes] = (mixed * sc_ref[:, lanes]
                           * _silu(gate_ref[:, lanes].astype(F32))).astype(o_ref.dtype)


def _pool_mixer(proj, colop, icnt, taps, pool_w, pool_scale):
    nb, n, _ = proj.shape
    ng = len(POOL_WINDOWS)
    pad = max(max(-lo, hi) for lo, hi in taps) * GRID_W
    pad = -(-pad // POOL_ROWS) * POOL_ROWS
    return pl.pallas_call(
        functools.partial(_pool_kernel, taps, pad),
        grid=(nb,),
        in_specs=[
            pl.BlockSpec((ng, POOL_SLAB, POOL_SLAB), lambda b: (0, 0, 0)),
            pl.BlockSpec((ng, n, 1), lambda b: (0, 0, 0)),
            pl.BlockSpec((None, n, POOL_W), lambda b: (b, 0, 0)),
            pl.BlockSpec((None, n, POOL_W), lambda b: (b, 0, 1)),
            pl.BlockSpec((ng, POOL_GROUP, POOL_GROUP), lambda b: (0, 0, 0)),
            pl.BlockSpec((1, POOL_W), lambda b: (0, 0)),
        ],
        out_specs=pl.BlockSpec((None, n, POOL_W), lambda b: (b, 0, 0)),
        out_shape=jax.ShapeDtypeStruct((nb, n, POOL_W), BF16),
        scratch_shapes=[pltpu.VMEM((ng, n + 2 * pad, POOL_GROUP), F32),
                        pltpu.VMEM((ng, n, POOL_GROUP), BF16)],
        compiler_params=pltpu.CompilerParams(
            dimension_semantics=("arbitrary",), vmem_limit_bytes=VMEM_LIMIT),
        name="pool_mixer",
    )(colop, icnt, proj, proj, pool_w, pool_scale)


def _conv_kernel(x_ref, b_ref, c_ref, gate_ref, cw_ref, cb_ref, o_ref):
    n = x_ref.shape[0]
    u = c_ref[...].astype(F32) * x_ref[...].astype(F32)
    row = lax.broadcasted_iota(jnp.int32, u.shape, 0)
    u_prev = jnp.where(row == 0, 0.0, pltpu.roll(u, 1, axis=0))
    u_next = jnp.where(row == n - 1, 0.0, pltpu.roll(u, n - 1, axis=0))
    cw = cw_ref[...]
    conv = u_prev * cw[0:1, :] + u * cw[1:2, :] + u_next * cw[2:3, :] + cb_ref[...]
    o_ref[...] = (b_ref[...].astype(F32) * conv
                  * _silu(gate_ref[...].astype(F32))).astype(o_ref.dtype)


def _conv_mixer(proj, conv_w, conv_b):
    nb, n, _ = proj.shape
    tc = TC_CONV
    nblk = CONV_W // tc
    base = 2 * POOL_W // tc

    def seg(k):
        return pl.BlockSpec((None, n, tc), lambda b, j: (b, 0, base + k * nblk + j))

    return pl.pallas_call(
        _conv_kernel,
        grid=(nb, nblk),
        in_specs=[
            seg(0), seg(1), seg(2), seg(3),
            pl.BlockSpec((3, tc), lambda b, j: (0, j)),
            pl.BlockSpec((1, tc), lambda b, j: (0, j)),
        ],
        out_specs=pl.BlockSpec((None, n, tc), lambda b, j: (b, 0, j)),
        out_shape=jax.ShapeDtypeStruct((nb, n, CONV_W), BF16),
        compiler_params=pltpu.CompilerParams(
            dimension_semantics=("arbitrary", "arbitrary"), vmem_limit_bytes=VMEM_LIMIT),
        name="conv_mixer",
    )(proj, proj, proj, proj, conv_w, conv_b)


def _outproj_kernel(n_in, emit_next, *refs):
    a_refs = refs[:n_in]
    w_refs = refs[n_in:2 * n_in]
    x_ref, gate_ref, pg_ref = refs[2 * n_in:2 * n_in + 3]
    rest = refs[2 * n_in + 3:]
    if emit_next:
        shift_ref, scale_ref, g_ref, o_ref, h_ref = rest[:5]
        wb_refs = rest[5:]
    else:
        o_ref = rest[0]
        wb_refs = rest[1:]

    @pl.when((pl.program_id(0) == 0) & (pl.program_id(1) == 0))
    def _():
        for w_ref, wb_ref in zip(w_refs, wb_refs):
            wb_ref[...] = w_ref[...].astype(BF16)

    y = jnp.dot(a_refs[0][...], wb_refs[0][...], preferred_element_type=F32)
    for a_ref, wb_ref in zip(a_refs[1:], wb_refs[1:]):
        y = y + jnp.dot(a_ref[...], wb_ref[...], preferred_element_type=F32)
    ms = jnp.mean(y * y, axis=-1, keepdims=True)
    r = y * lax.rsqrt(ms + EPS) * pg_ref[...]
    x_new = x_ref[...] + gate_ref[...] * r
    o_ref[...] = x_new
    if emit_next:
        h_ref[...] = _norm_mod(x_new, g_ref[...], scale_ref[...],
                               shift_ref[...]).astype(h_ref.dtype)


def _outproj(acts, w, x3, mod_row_of_batch, gate, pg, tm, next_norm=None):
    nb, rows, _ = x3.shape
    n_in = len(acts)
    k_in = acts[0].shape[-1]
    assert all(a.shape[-1] == k_in for a in acts) and n_in * k_in == w.shape[0]
    emit_next = next_norm is not None
    mod_spec = pl.BlockSpec((None, 1, D_MODEL), lambda b, i: (mod_row_of_batch(b), 0, 0))
    vec_spec = pl.BlockSpec((1, D_MODEL), lambda b, i: (0, 0))
    tok_spec = pl.BlockSpec((None, tm, D_MODEL), lambda b, i: (b, i, 0))
    in_specs = [pl.BlockSpec((None, tm, k_in), lambda b, i: (b, i, 0))] * n_in
    in_specs += [pl.BlockSpec((k_in, D_MODEL), functools.partial(lambda k, b, i: (k, 0), k),
                              pipeline_mode=pl.Buffered(1)) for k in range(n_in)]
    in_specs += [tok_spec, mod_spec, vec_spec]
    operands = [*acts, *([w] * n_in), x3, gate, pg]
    out_specs, out_shape = tok_spec, jax.ShapeDtypeStruct((nb, rows, D_MODEL), F32)
    if emit_next:
        in_specs += [mod_spec, mod_spec, vec_spec]
        operands += list(next_norm)
        out_specs = [tok_spec, tok_spec]
        out_shape = [out_shape, jax.ShapeDtypeStruct((nb, rows, D_MODEL), BF16)]
    return pl.pallas_call(
        functools.partial(_outproj_kernel, n_in, emit_next),
        grid=(nb, rows // tm),
        in_specs=in_specs,
        out_specs=out_specs,
        out_shape=out_shape,
        scratch_shapes=[pltpu.VMEM((k_in, D_MODEL), BF16)] * n_in,
        compiler_params=pltpu.CompilerParams(
            dimension_semantics=("arbitrary", "arbitrary"), vmem_limit_bytes=VMEM_LIMIT),
        name="outproj",
    )(*operands)


_NT = (((1,), (1,)), ((), ()))
_TN = (((0,), (0,)), ((), ()))
SUBLANES = 8
SCAN_HEADS = 2
SCAN_GROUPS = 2
HGRN_PIECE = 128
SCAN_CHUNK = 2 * CHUNK


def _chunk_cumsum(x, reverse, in_ref, out_ref):
    c, w = x.shape
    nblk = c // SUBLANES
    order = range(nblk - 1, -1, -1) if reverse else range(nblk)
    in_ref[...] = x
    pref = [None] * nblk
    acc = None
    for j in order:
        blk = in_ref[pl.ds(j, SUBLANES, stride=nblk), :]
        acc = blk if acc is None else acc + blk
        pref[j] = acc
    total = acc
    sub = lax.broadcasted_iota(jnp.int32, (SUBLANES, w), 0)
    incl = total
    for s in (1, 2, 4):
        if reverse:
            incl = incl + jnp.where(sub < SUBLANES - s,
                                    pltpu.roll(incl, SUBLANES - s, axis=0), 0.0)
        else:
            incl = incl + jnp.where(sub >= s, pltpu.roll(incl, s, axis=0), 0.0)
    before = incl - total
    for j in range(nblk):
        out_ref[pl.ds(j, SUBLANES, stride=nblk), :] = pref[j] + before
    edge = incl[0:1, :] if reverse else incl[SUBLANES - 1:SUBLANES, :]
    return out_ref[...], jnp.broadcast_to(edge, (SUBLANES, w))


def _gate_decay(zh, lb, reverse, cs_ref):
    bt = (0.5 * (1.0 - lb)) * jnp.tanh(zh)
    k = 0.5 * (1.0 - lb) - bt
    lf = jnp.log(0.5 * (1.0 + lb) + bt)
    lo, hi = lf[:CHUNK], lf[CHUNK:]
    first, second = (hi, lo) if reverse else (lo, hi)
    away, t_first = _chunk_cumsum(first, not reverse, cs_ref.at[0], cs_ref.at[1])
    toward, t_second = _chunk_cumsum(second, reverse, cs_ref.at[2], cs_ref.at[3])
    a_first = first - away
    a = jnp.concatenate([toward, a_first] if reverse else [a_first, toward], axis=0)
    k_inv = (k * jnp.exp(-a)).astype(BF16)
    return jnp.exp(a), k_inv, jnp.exp(t_first), jnp.exp(t_second)


def _hgrn_inproj_kernel(with_queries, h_ref, lbf_ref, lbb_ref, *refs):
    n_w = 5 if with_queries else 3
    n_dir = 4 if with_queries else 3
    w_refs, rest = refs[:n_w], refs[n_w:]
    per_dir = (rest[0:n_dir], rest[n_dir:2 * n_dir])
    rest = rest[2 * n_dir:]
    if with_queries:
        v_ref, g_ref = rest[:2]
        rest = rest[2:]
    else:
        v_ref = rest[0]
        rest = rest[1:]
    wb_refs, cs_s = rest[:n_w], rest[n_w]

    @pl.when(pl.program_id(1) == 0)
    def _():
        for k, (w_ref, wb_ref) in enumerate(zip(w_refs, wb_refs)):
            wk = w_ref[...]
            wb_ref[...] = (wk if k in (2, 3) else wk * 0.5).astype(BF16)

    tm, w = v_ref.shape
    n_heads = w // HG_DK
    chunks_per_piece = HGRN_PIECE // SCAN_CHUNK
    for p in range(tm // HGRN_PIECE):
        piece = slice(p * HGRN_PIECE, (p + 1) * HGRN_PIECE)
        h = h_ref[piece, :]

        def proj(k):
            return jnp.dot(h, wb_refs[k][...], preferred_element_type=F32)

        zhs = [proj(0), proj(1)]
        if with_queries:
            q = proj(3)
        v_ref[piece, :] = proj(2).astype(BF16)
        if with_queries:
            g_ref[piece, :] = proj(4).astype(BF16)
        for d, (lb_ref, reverse) in enumerate(((lbf_ref, False), (lbb_ref, True))):
            lb = lb_ref[...]
            ki_ref, first_ref, second_ref = per_dir[d][-3:]
            for cp in range(chunks_per_piece):
                c = p * chunks_per_piece + cp
                rows = slice(c * SCAN_CHUNK, (c + 1) * SCAN_CHUNK)
                prows = slice(cp * SCAN_CHUNK, (cp + 1) * SCAN_CHUNK)
                drows = slice(c * SUBLANES, (c + 1) * SUBLANES)
                for hd in range(n_heads):
                    lanes = slice(hd * HG_DK, (hd + 1) * HG_DK)
                    slot = (d * (tm // SCAN_CHUNK) + c) * n_heads + hd
                    e_a, k_inv, d_first, d_second = _gate_decay(
                        zhs[d][prows, lanes], lb[:, lanes], reverse, cs_s.at[slot])
                    if with_queries:
                        per_dir[d][0][rows, lanes] = (q[prows, lanes] * e_a).astype(BF16)
                    ki_ref[rows, lanes] = k_inv
                    first_ref[drows, lanes] = d_first
                    second_ref[drows, lanes] = d_second


def _hgrn_inproj(h, w, lb_f, lb_b, with_queries, tm):
    rows = h.shape[0]
    tn = SCAN_HEADS * HG_DK
    groups = HG_K // tn
    n_w = 5 if with_queries else 3
    w_specs = [pl.BlockSpec((D_MODEL, tn), functools.partial(
        lambda k, j, i: (0, k * groups + j), k)) for k in range(n_w)]
    lb_spec = pl.BlockSpec((1, tn), lambda j, i: (0, j))
    dec_rows = tm // SCAN_CHUNK * SUBLANES
    tok = (pl.BlockSpec((None, tm, tn), lambda j, i: (j, i, 0)),
           jax.ShapeDtypeStruct((groups, rows, tn), BF16))
    dec = (pl.BlockSpec((None, dec_rows, tn), lambda j, i: (j, i, 0)),
           jax.ShapeDtypeStruct((groups, rows // SCAN_CHUNK * SUBLANES, tn), F32))
    one_dir = [tok, tok, dec, dec] if with_queries else [tok, dec, dec]
    outs = one_dir + one_dir + ([tok, tok] if with_queries else [tok])
    n_slots = 2 * (tm // SCAN_CHUNK) * SCAN_HEADS
    return pl.pallas_call(
        functools.partial(_hgrn_inproj_kernel, with_queries),
        grid=(groups, rows // tm),
        in_specs=[pl.BlockSpec((tm, D_MODEL), lambda j, i: (i, 0)), lb_spec, lb_spec] + w_specs,
        out_specs=[o[0] for o in outs],
        out_shape=[o[1] for o in outs],
        scratch_shapes=[pltpu.VMEM((D_MODEL, tn), BF16)] * n_w
        + [pltpu.VMEM((n_slots, 4, CHUNK, HG_DK), F32)],
        compiler_params=pltpu.CompilerParams(
            dimension_semantics=("arbitrary", "arbitrary"), vmem_limit_bytes=VMEM_LIMIT),
        name="hgrn_inproj",
    )(h, lb_f, lb_b, *([w] * n_w))


def _scan_kernel(qdf_ref, kif_ref, firstf_ref, secondf_ref,
                 qdb_ref, kib_ref, firstb_ref, secondb_ref, v_ref, g_ref,
                 ckif_ref, cfirstf_ref, csecondf_ref, ckib_ref, cfirstb_ref, csecondb_ref,
                 cv_ref, og_ref, o_ref, st_s, of_s, ob_s):
    c = SCAN_CHUNK
    n_lat = v_ref.shape[1] // c
    n_ctx = cv_ref.shape[1] // c
    heads_per_group = v_ref.shape[2] // HG_DK
    n_heads = v_ref.shape[0] * heads_per_group
    row = lax.broadcasted_iota(jnp.int32, (c, c), 0)
    col = lax.broadcasted_iota(jnp.int32, (c, c), 1)
    dirs = (((qdf_ref, kif_ref, firstf_ref, secondf_ref),
             (ckif_ref, cfirstf_ref, csecondf_ref), False, col <= row, of_s),
            ((qdb_ref, kib_ref, firstb_ref, secondb_ref),
             (ckib_ref, cfirstb_ref, csecondb_ref), True, col >= row, ob_s))

    def lanes(h):
        return slice(h * HG_DK, (h + 1) * HG_DK)

    def rows(chunk):
        return slice(chunk * c, (chunk + 1) * c)

    def tokens(ref, h, chunk):
        return ref[h // heads_per_group, rows(chunk), lanes(h % heads_per_group)]

    def decay_row(ref, h, chunk):
        return ref[h // heads_per_group, chunk * SUBLANES:chunk * SUBLANES + 1,
                   lanes(h % heads_per_group)]

    def state_increment(v, k_inv, d_second):
        k_dec = k_inv * d_second.astype(BF16)
        return lax.dot_general(v, k_dec, _TN, preferred_element_type=F32)

    def local(d, chunk):
        qd_ref, ki_ref, _, second_ref = dirs[d][0]
        mask = dirs[d][3]
        out = []
        for h in range(n_heads):
            k_inv = tokens(ki_ref, h, chunk)
            sc = lax.dot_general(tokens(qd_ref, h, chunk), k_inv, _NT,
                                 preferred_element_type=F32)
            ds_t = state_increment(tokens(v_ref, h, chunk), k_inv,
                                   decay_row(second_ref, h, chunk))
            out.append((jnp.where(mask, sc, 0.0).astype(BF16), ds_t))
        return out

    def step(d, chunk, products):
        qd_ref, _, first_ref, second_ref = dirs[d][0]
        out_s = dirs[d][4]
        for h in range(n_heads):
            d_first = decay_row(first_ref, h, chunk)
            st_mid = (st_s[d, h] * d_first).astype(BF16)
            inter = lax.dot_general(tokens(qd_ref, h, chunk), st_mid, _NT,
                                    preferred_element_type=F32)
            sc, ds_t = products[h]
            out_s[rows(chunk), lanes(h)] = inter + jnp.dot(
                sc, tokens(v_ref, h, chunk), preferred_element_type=F32)
            st_s[d, h] = st_s[d, h] * (d_first * decay_row(second_ref, h, chunk)) + ds_t

    st_s[...] = jnp.zeros_like(st_s)
    for j in range(n_ctx):
        for d in range(2):
            cki_ref, cfirst_ref, csecond_ref = dirs[d][1]
            cj = n_ctx - 1 - j if dirs[d][2] else j
            for h in range(n_heads):
                d_second = decay_row(csecond_ref, h, cj)
                ds_t = state_increment(tokens(cv_ref, h, cj), tokens(cki_ref, h, cj), d_second)
                st_s[d, h] = st_s[d, h] * (decay_row(cfirst_ref, h, cj) * d_second) + ds_t

    def chunk_of(d, idx):
        return n_lat - 1 - idx if dirs[d][2] else idx

    og = og_ref[...]
    ones = jnp.ones((HG_DV, HG_DV), BF16)

    def readout(chunk):
        sl = rows(chunk)
        for h in range(n_heads):
            o = of_s[sl, lanes(h)] + ob_s[sl, lanes(h)]
            ss = jnp.dot((o * o).astype(BF16), ones, preferred_element_type=F32)
            o = o * lax.rsqrt(ss * (1.0 / HG_DV) + EPS)
            gh = tokens(g_ref, h, chunk).astype(F32)
            o_ref[sl, lanes(h)] = (o * og[:, lanes(h)]
                                   * (gh + gh * jnp.tanh(gh))).astype(o_ref.dtype)

    assert n_lat % 2 == 0
    products = [local(d, chunk_of(d, 0)) for d in range(2)]
    for idx in range(n_lat):
        for d in range(2):
            step(d, chunk_of(d, idx), products[d])
        if idx + 1 < n_lat:
            products = [local(d, chunk_of(d, idx + 1)) for d in range(2)]
        if 2 * idx >= n_lat:
            readout(idx)
            readout(n_lat - 1 - idx)


def _hgrn2_scan(lat_ops, ctx_ops, onorm_g, nb):
    groups, rows, w = lat_ops[-1].shape
    n = rows // nb

    def blocked(a):
        per_batch = a.shape[1] // nb
        return (a.reshape(groups, nb, per_batch, w),
                pl.BlockSpec((SCAN_GROUPS, None, per_batch, w), lambda b, h: (h, b, 0, 0)))

    ops, specs = zip(*[blocked(a) for a in (*lat_ops, *ctx_ops)])
    wide = SCAN_GROUPS * w
    vec = pl.BlockSpec((1, wide), lambda b, h: (0, h))
    return pl.pallas_call(
        _scan_kernel,
        grid=(nb, groups // SCAN_GROUPS),
        in_specs=list(specs) + [vec],
        out_specs=pl.BlockSpec((None, n, wide), lambda b, h: (b, 0, h)),
        out_shape=jax.ShapeDtypeStruct((nb, n, HG_V), BF16),
        scratch_shapes=[pltpu.VMEM((2, wide // HG_DK, HG_DV, HG_DK), F32),
                        pltpu.VMEM((n, wide), F32),
                        pltpu.VMEM((n, wide), F32)],
        compiler_params=pltpu.CompilerParams(
            dimension_semantics=("arbitrary", "arbitrary"), vmem_limit_bytes=VMEM_LIMIT),
        name="hgrn2_scan",
    )(*ops, onorm_g)


def _window(n, w):
    t = np.arange(n)
    lo = np.maximum(t - w // 2, 0)
    hi = np.minimum(t + w // 2 - 1, n - 1)
    s = np.arange(n)
    inside = (s[None, :] >= lo[:, None]) & (s[None, :] <= hi[:, None])
    return inside, (hi - lo + 1)


def _pool_operators(n, on_grid):
    mats, icnts, taps = [], [], []
    for w in POOL_WINDOWS:
        if on_grid:
            in_c, cnt_c = _window(GRID_W, w)
            _, cnt_r = _window(n // GRID_W, w)
            m = np.kron(np.eye(POOL_SLAB // GRID_W, dtype=bool), in_c)
            cnt = (cnt_r[:, None] * cnt_c[None, :]).reshape(n)
            taps.append((-(w // 2), w // 2 - 1))
        else:
            assert n == POOL_SLAB
            m, cnt = _window(n, w)
            taps.append((0, 0))
        mats.append(m)
        icnts.append(1.0 / cnt.astype(np.float64))
    colop = jnp.asarray(np.stack(mats).astype(np.float32), dtype=BF16)
    icnt = jnp.asarray(np.stack(icnts).astype(np.float32)[..., None])
    return colop, icnt, tuple(taps)


def kernel(x, c, ctx, c_ctx, ada_w, ada_b, pre_g, post_g, ev_w_in, ev_pool_w, ev_pool_scale,
           ev_conv_w, ev_conv_b, ev_w_out, od_w_in, od_onorm_g, od_w_out, lb_logits):
    nb, n, d = x.shape
    nc = ctx.shape[1]
    lat_row = lambda b: b
    ctx_row = lambda b: CTX_ROW

    lb_table = jnp.cumsum(jax.nn.softmax(lb_logits.astype(F32), axis=1), axis=1)

    cc = jnp.zeros((MOD_ROWS, d), F32).at[:nb].set(c).at[CTX_ROW].set(c_ctx)
    mod = _ada_table(cc, ada_w, ada_b)
    mod = mod.reshape(2, MOD_ROWS, 3, 1, d)
    shift = [mod[l, :, 0] for l in range(2)]
    scale = [mod[l, :, 1] for l in range(2)]
    gate = [mod[l, :, 2] for l in range(2)]

    ctx_flat = ctx.reshape(1, nb * nc, d)

    pool_w = ev_pool_w[0].astype(BF16)
    pool_scale = ev_pool_scale[0].reshape(1, POOL_W)
    conv_w = ev_conv_w[0]
    conv_b = ev_conv_b[0].reshape(1, CONV_W)
    pre0 = pre_g[0].reshape(1, d)
    post0 = post_g[0].reshape(1, d)

    pre1 = pre_g[1].reshape(1, d)
    post1 = post_g[1].reshape(1, d)
    norm1 = (shift[1], scale[1], pre1)

    def even_mixer(tokens3, mod_row, on_grid):
        b3, n3, _ = tokens3.shape
        h = _norm_tokens(tokens3, mod_row, shift[0], scale[0], pre0, TM_NORM)
        proj, = _inproj(h.reshape(b3 * n3, d), ev_w_in[0], EVEN_IN, [BF16],
                        min(TM_IN0, b3 * n3), TN_IN0)
        n_tok = n if on_grid else nc
        proj = proj.reshape(nb, n_tok, EVEN_IN)
        colop, icnt, taps = _pool_operators(n_tok, on_grid)
        a_out = _pool_mixer(proj, colop, icnt, taps, pool_w, pool_scale)
        b_out = _conv_mixer(proj, conv_w, conv_b)
        acts = [a_out.reshape(b3, n3, POOL_W), b_out.reshape(b3, n3, CONV_W)]
        return _outproj(acts, ev_w_out[0], tokens3, mod_row, gate[0], post0, TM_OUT, norm1)

    x1, h1 = even_mixer(x, lat_row, True)
    _, hc1 = even_mixer(ctx_flat, ctx_row, False)

    lb_f = lb_table[0, 1].reshape(1, HG_K)
    lb_b = lb_table[1, 1].reshape(1, HG_K)
    lat_ops = _hgrn_inproj(h1.reshape(nb * n, d), od_w_in[0], lb_f, lb_b, True, TM_IN)
    ctx_ops = _hgrn_inproj(hc1.reshape(nb * nc, d), od_w_in[0], lb_f, lb_b, False, TM_IN)
    o = _hgrn2_scan(lat_ops, ctx_ops, od_onorm_g[0].reshape(1, HG_V), nb)
    return _outproj([o], od_w_out[0], x1, lat_row, gate[1], post1, TM_OUT)
```

```python
import functools

import numpy as np
import jax
import jax.numpy as jnp
from jax import lax
from jax.experimental import pallas as pl
from jax.experimental.pallas import tpu as pltpu

F32 = jnp.float32
BF16 = jnp.bfloat16

D_MODEL = 2048
BATCH = 4
GRID_W = 64
EPS = 1e-6

POOL_WINDOWS = (2, 4, 8, 16)
POOL_W = D_MODEL // 2
POOL_GROUP = POOL_W // len(POOL_WINDOWS)
CONV_W = D_MODEL // 2
EVEN_IN = 2 * POOL_W + 4 * CONV_W

HG_DK = 128
HG_HEADS = D_MODEL // HG_DK
HG_DV = D_MODEL // HG_HEADS
HG_K = HG_HEADS * HG_DK
HG_V = HG_HEADS * HG_DV
ODD_IN = 3 * HG_K + 2 * HG_V
CHUNK = 64

MOD_ROWS = 8
CTX_ROW = BATCH
V7X_VMEM_BYTES = 64 * 1024 * 1024
VMEM_LIMIT = V7X_VMEM_BYTES * 7 // 8
TM_NORM = 1024
TM_IN0 = 2048
TM_IN = 1024
TM_OUT = 512
TN_ADA = 1024
TN_IN0 = 1024
TC_CONV = 256


def _silu(v):
    hv = 0.5 * v
    return hv + hv * jnp.tanh(hv)


def _ada_kernel(cc_ref, w_ref, b_ref, o_ref):
    s = _silu(cc_ref[...])
    o_ref[...] = jnp.dot(s, w_ref[...], preferred_element_type=F32) + b_ref[...]


def _ada_table(cc, ada_w, ada_b):
    depth = ada_w.shape[0]
    tn = TN_ADA
    return pl.pallas_call(
        _ada_kernel,
        grid=(depth, 3 * D_MODEL // tn),
        in_specs=[
            pl.BlockSpec((MOD_ROWS, D_MODEL), lambda l, j: (0, 0)),
            pl.BlockSpec((None, D_MODEL, tn), lambda l, j: (l, 0, j)),
            pl.BlockSpec((None, 1, tn), lambda l, j: (l, 0, j)),
        ],
        out_specs=pl.BlockSpec((None, MOD_ROWS, tn), lambda l, j: (l, 0, j)),
        out_shape=jax.ShapeDtypeStruct((depth, MOD_ROWS, 3 * D_MODEL), F32),
        compiler_params=pltpu.CompilerParams(
            dimension_semantics=("arbitrary", "arbitrary"), vmem_limit_bytes=VMEM_LIMIT),
        name="ada_table",
    )(cc, ada_w, ada_b.reshape(depth, 1, 3 * D_MODEL))


def _norm_mod(x, g, scale, shift):
    ms = jnp.mean(x * x, axis=-1, keepdims=True)
    y = x * lax.rsqrt(ms + EPS) * g
    return y * (1.0 + scale) + shift


def _norm_kernel(x_ref, shift_ref, scale_ref, g_ref, o_ref):
    o_ref[...] = _norm_mod(x_ref[...], g_ref[...], scale_ref[...],
                           shift_ref[...]).astype(o_ref.dtype)


def _norm_tokens(x3, mod_row_of_batch, shift, scale, g, tm):
    nb, rows, _ = x3.shape
    mod_map = lambda b, i: (mod_row_of_batch(b), 0, 0)
    return pl.pallas_call(
        _norm_kernel,
        grid=(nb, rows // tm),
        in_specs=[
            pl.BlockSpec((None, tm, D_MODEL), lambda b, i: (b, i, 0)),
            pl.BlockSpec((None, 1, D_MODEL), mod_map),
            pl.BlockSpec((None, 1, D_MODEL), mod_map),
            pl.BlockSpec((1, D_MODEL), lambda b, i: (0, 0)),
        ],
        out_specs=pl.BlockSpec((None, tm, D_MODEL), lambda b, i: (b, i, 0)),
        out_shape=jax.ShapeDtypeStruct((nb, rows, D_MODEL), BF16),
        compiler_params=pltpu.CompilerParams(
            dimension_semantics=("arbitrary", "arbitrary"), vmem_limit_bytes=VMEM_LIMIT),
        name="norm_tokens",
    )(x3, shift, scale, g)


def _inproj_kernel(n_seg, h_ref, *refs):
    w_refs = refs[:n_seg]
    o_refs = refs[n_seg:2 * n_seg]
    wb_refs = refs[2 * n_seg:]

    @pl.when(pl.program_id(1) == 0)
    def _():
        for w_ref, wb_ref in zip(w_refs, wb_refs):
            wb_ref[...] = w_ref[...].astype(BF16)

    for wb_ref, o_ref in zip(wb_refs, o_refs):
        o_ref[...] = jnp.dot(h_ref[...], wb_ref[...],
                             preferred_element_type=F32).astype(o_ref.dtype)


def _inproj(h, w, seg_width, seg_dtypes, tm, tn):
    rows = h.shape[0]
    n_seg = len(seg_dtypes)
    n_tiles = seg_width // tn
    w_specs = [pl.BlockSpec((D_MODEL, tn), functools.partial(
        lambda k, j, i: (0, k * n_tiles + j), k)) for k in range(n_seg)]
    return pl.pallas_call(
        functools.partial(_inproj_kernel, n_seg),
        grid=(n_tiles, rows // tm),
        in_specs=[pl.BlockSpec((tm, D_MODEL), lambda j, i: (i, 0))] + w_specs,
        out_specs=[pl.BlockSpec((tm, tn), lambda j, i: (i, j))] * n_seg,
        out_shape=[jax.ShapeDtypeStruct((rows, seg_width), dt) for dt in seg_dtypes],
        scratch_shapes=[pltpu.VMEM((D_MODEL, tn), BF16)] * n_seg,
        compiler_params=pltpu.CompilerParams(
            dimension_semantics=("arbitrary", "arbitrary"), vmem_limit_bytes=VMEM_LIMIT),
        name="inproj",
    )(h, *([w] * n_seg))


POOL_SLAB = 256
POOL_ROWS = 128


def _pool_kernel(taps, pad, colop_ref, icnt_ref, v_ref, gate_ref, w_ref, sc_ref, o_ref,
                 box_s, pooled_s):
    n = v_ref.shape[0]
    for gi, (lo, hi) in enumerate(taps):
        lanes = slice(gi * POOL_GROUP, (gi + 1) * POOL_GROUP)
        if pad:
            box_s[gi, 0:pad, :] = jnp.zeros((pad, POOL_GROUP), F32)
            box_s[gi, pad + n:pad + n + pad, :] = jnp.zeros((pad, POOL_GROUP), F32)
        for s in range(n // POOL_SLAB):
            tok = slice(s * POOL_SLAB, (s + 1) * POOL_SLAB)
            box_s[gi, pad + s * POOL_SLAB:pad + (s + 1) * POOL_SLAB, :] = jnp.dot(
                colop_ref[gi], v_ref[tok, lanes], preferred_element_type=F32)

        for r0 in range(0, n, POOL_ROWS):
            tok = slice(r0, r0 + POOL_ROWS)
            first = pad + lo * GRID_W + r0
            acc = box_s[gi, first:first + POOL_ROWS, :]
            for dlt in range(1, hi - lo + 1):
                acc = acc + box_s[gi, first + dlt * GRID_W:first + dlt * GRID_W + POOL_ROWS, :]
            pooled = acc * icnt_ref[gi, tok, :] - v_ref[tok, lanes].astype(F32)
            pooled_s[gi, tok, :] = pooled.astype(BF16)
        mixed = jnp.dot(pooled_s[gi], w_ref[gi], preferred_element_type=F32)
        o_ref[:, lanes] = (mixed * sc_ref[:, lanes]
                           * _silu(gate_ref[:, lanes].astype(F32))).astype(o_ref.dtype)


def _pool_mixer(proj, colop, icnt, taps, pool_w, pool_scale):
    nb, n, _ = proj.shape
    ng = len(POOL_WINDOWS)
    pad = max(max(-lo, hi) for lo, hi in taps) * GRID_W
    pad = -(-pad // POOL_ROWS) * POOL_ROWS
    return pl.pallas_call(
        functools.partial(_pool_kernel, taps, pad),
        grid=(nb,),
        in_specs=[
            pl.BlockSpec((ng, POOL_SLAB, POOL_SLAB), lambda b: (0, 0, 0)),
            pl.BlockSpec((ng, n, 1), lambda b: (0, 0, 0)),
            pl.BlockSpec((None, n, POOL_W), lambda b: (b, 0, 0)),
            pl.BlockSpec((None, n, POOL_W), lambda b: (b, 0, 1)),
            pl.BlockSpec((ng, POOL_GROUP, POOL_GROUP), lambda b: (0, 0, 0)),
            pl.BlockSpec((1, POOL_W), lambda b: (0, 0)),
        ],
        out_specs=pl.BlockSpec((None, n, POOL_W), lambda b: (b, 0, 0)),
        out_shape=jax.ShapeDtypeStruct((nb, n, POOL_W), BF16),
        scratch_shapes=[pltpu.VMEM((ng, n + 2 * pad, POOL_GROUP), F32),
                        pltpu.VMEM((ng, n, POOL_GROUP), BF16)],
        compiler_params=pltpu.CompilerParams(
            dimension_semantics=("arbitrary",), vmem_limit_bytes=VMEM_LIMIT),
        name="pool_mixer",
    )(colop, icnt, proj, proj, pool_w, pool_scale)


def _conv_kernel(x_ref, b_ref, c_ref, gate_ref, cw_ref, cb_ref, o_ref):
    n = x_ref.shape[0]
    u = c_ref[...].astype(F32) * x_ref[...].astype(F32)
    row = lax.broadcasted_iota(jnp.int32, u.shape, 0)
    u_prev = jnp.where(row == 0, 0.0, pltpu.roll(u, 1, axis=0))
    u_next = jnp.where(row == n - 1, 0.0, pltpu.roll(u, n - 1, axis=0))
    cw = cw_ref[...]
    conv = u_prev * cw[0:1, :] + u * cw[1:2, :] + u_next * cw[2:3, :] + cb_ref[...]
    o_ref[...] = (b_ref[...].astype(F32) * conv
                  * _silu(gate_ref[...].astype(F32))).astype(o_ref.dtype)


def _conv_mixer(proj, conv_w, conv_b):
    nb, n, _ = proj.shape
    tc = TC_CONV
    nblk = CONV_W // tc
    base = 2 * POOL_W // tc

    def seg(k):
        return pl.BlockSpec((None, n, tc), lambda b, j: (b, 0, base + k * nblk + j))

    return pl.pallas_call(
        _conv_kernel,
        grid=(nb, nblk),
        in_specs=[
            seg(0), seg(1), seg(2), seg(3),
            pl.BlockSpec((3, tc), lambda b, j: (0, j)),
            pl.BlockSpec((1, tc), lambda b, j: (0, j)),
        ],
        out_specs=pl.BlockSpec((None, n, tc), lambda b, j: (b, 0, j)),
        out_shape=jax.ShapeDtypeStruct((nb, n, CONV_W), BF16),
        compiler_params=pltpu.CompilerParams(
            dimension_semantics=("arbitrary", "arbitrary"), vmem_limit_bytes=VMEM_LIMIT),
        name="conv_mixer",
    )(proj, proj, proj, proj, conv_w, conv_b)


def _outproj_kernel(n_in, emit_next, *refs):
    a_refs = refs[:n_in]
    w_refs = refs[n_in:2 * n_in]
    x_ref, gate_ref, pg_ref = refs[2 * n_in:2 * n_in + 3]
    rest = refs[2 * n_in + 3:]
    if emit_next:
        shift_ref, scale_ref, g_ref, o_ref, h_ref = rest[:5]
        wb_refs = rest[5:]
    else:
        o_ref = rest[0]
        wb_refs = rest[1:]

    @pl.when((pl.program_id(0) == 0) & (pl.program_id(1) == 0))
    def _():
        for w_ref, wb_ref in zip(w_refs, wb_refs):
            wb_ref[...] = w_ref[...].astype(BF16)

    y = jnp.dot(a_refs[0][...], wb_refs[0][...], preferred_element_type=F32)
    for a_ref, wb_ref in zip(a_refs[1:], wb_refs[1:]):
        y = y + jnp.dot(a_ref[...], wb_ref[...], preferred_element_type=F32)
    ms = jnp.mean(y * y, axis=-1, keepdims=True)
    r = y * lax.rsqrt(ms + EPS) * pg_ref[...]
    x_new = x_ref[...] + gate_ref[...] * r
    o_ref[...] = x_new
    if emit_next:
        h_ref[...] = _norm_mod(x_new, g_ref[...], scale_ref[...],
                               shift_ref[...]).astype(h_ref.dtype)


def _outproj(acts, w, x3, mod_row_of_batch, gate, pg, tm, next_norm=None):
    nb, rows, _ = x3.shape
    n_in = len(acts)
    k_in = acts[0].shape[-1]
    assert all(a.shape[-1] == k_in for a in acts) and n_in * k_in == w.shape[0]
    emit_next = next_norm is not None
    mod_spec = pl.BlockSpec((None, 1, D_MODEL), lambda b, i: (mod_row_of_batch(b), 0, 0))
    vec_spec = pl.BlockSpec((1, D_MODEL), lambda b, i: (0, 0))
    tok_spec = pl.BlockSpec((None, tm, D_MODEL), lambda b, i: (b, i, 0))
    in_specs = [pl.BlockSpec((None, tm, k_in), lambda b, i: (b, i, 0))] * n_in
    in_specs += [pl.BlockSpec((k_in, D_MODEL), functools.partial(lambda k, b, i: (k, 0), k),
                              pipeline_mode=pl.Buffered(1)) for k in range(n_in)]
    in_specs += [tok_spec, mod_spec, vec_spec]
    operands = [*acts, *([w] * n_in), x3, gate, pg]
    out_specs, out_shape = tok_spec, jax.ShapeDtypeStruct((nb, rows, D_MODEL), F32)
    if emit_next:
        in_specs += [mod_spec, mod_spec, vec_spec]
        operands += list(next_norm)
        out_specs = [tok_spec, tok_spec]
        out_shape = [out_shape, jax.ShapeDtypeStruct((nb, rows, D_MODEL), BF16)]
    return pl.pallas_call(
        functools.partial(_outproj_kernel, n_in, emit_next),
        grid=(nb, rows // tm),
        in_specs=in_specs,
        out_specs=out_specs,
        out_shape=out_shape,
        scratch_shapes=[pltpu.VMEM((k_in, D_MODEL), BF16)] * n_in,
        compiler_params=pltpu.CompilerParams(
            dimension_semantics=("arbitrary", "arbitrary"), vmem_limit_bytes=VMEM_LIMIT),
        name="outproj",
    )(*operands)


_NT = (((1,), (1,)), ((), ()))
_TN = (((0,), (0,)), ((), ()))
SUBLANES = 8
SCAN_HEADS = 2
SCAN_GROUPS = 2
SCAN_DECAY_SLOTS = 32
SCAN_CHUNK = 2 * CHUNK


def _chunk_cumsum(x, reverse, in_ref, out_ref):
    c, w = x.shape
    nblk = c // SUBLANES
    order = range(nblk - 1, -1, -1) if reverse else range(nblk)
    in_ref[...] = x
    pref = [None] * nblk
    acc = None
    for j in order:
        blk = in_ref[pl.ds(j, SUBLANES, stride=nblk), :]
        acc = blk if acc is None else acc + blk
        pref[j] = acc
    total = acc
    sub = lax.broadcasted_iota(jnp.int32, (SUBLANES, w), 0)
    incl = total
    for s in (1, 2, 4):
        if reverse:
            incl = incl + jnp.where(sub < SUBLANES - s,
                                    pltpu.roll(incl, SUBLANES - s, axis=0), 0.0)
        else:
            incl = incl + jnp.where(sub >= s, pltpu.roll(incl, s, axis=0), 0.0)
    before = incl - total
    for j in range(nblk):
        out_ref[pl.ds(j, SUBLANES, stride=nblk), :] = pref[j] + before
    edge = incl[0:1, :] if reverse else incl[SUBLANES - 1:SUBLANES, :]
    return out_ref[...], jnp.broadcast_to(edge, (SUBLANES, w))


def _gate_decay(zh, lb, reverse, cs_ref):
    bt = (0.5 * (1.0 - lb)) * jnp.tanh(zh)
    k = 0.5 * (1.0 - lb) - bt
    lf = jnp.log(0.5 * (1.0 + lb) + bt)
    lo, hi = lf[:CHUNK], lf[CHUNK:]
    first, second = (hi, lo) if reverse else (lo, hi)
    away, t_first = _chunk_cumsum(first, not reverse, cs_ref.at[0], cs_ref.at[1])
    toward, t_second = _chunk_cumsum(second, reverse, cs_ref.at[2], cs_ref.at[3])
    a_first = first - away
    a = jnp.concatenate([toward, a_first] if reverse else [a_first, toward], axis=0)
    k_inv = (k * jnp.exp(-a)).astype(BF16)
    return jnp.exp(a), k_inv, jnp.exp(t_first), jnp.exp(t_second)


def _hgrn_inproj_kernel(n_seg, h_ref, *refs):
    w_refs = refs[:n_seg]
    o_refs = refs[n_seg:2 * n_seg]
    wb_refs = refs[2 * n_seg:]

    @pl.when(pl.program_id(1) == 0)
    def _():
        for k, (w_ref, wb_ref) in enumerate(zip(w_refs, wb_refs)):
            wk = w_ref[...]
            wb_ref[...] = (wk if k in (2, 3) else wk * 0.5).astype(BF16)

    for wb_ref, o_ref in zip(wb_refs, o_refs):
        o_ref[...] = jnp.dot(h_ref[...], wb_ref[...],
                             preferred_element_type=F32).astype(o_ref.dtype)


def _hgrn_inproj(h, w, with_queries, tm):
    rows = h.shape[0]
    tn = SCAN_HEADS * HG_DK
    groups = HG_K // tn
    dtypes = [F32, F32, BF16, BF16, BF16] if with_queries else [F32, F32, BF16]
    n_seg = len(dtypes)
    w_specs = [pl.BlockSpec((D_MODEL, tn), functools.partial(
        lambda k, j, i: (0, k * groups + j), k)) for k in range(n_seg)]
    return pl.pallas_call(
        functools.partial(_hgrn_inproj_kernel, n_seg),
        grid=(groups, rows // tm),
        in_specs=[pl.BlockSpec((tm, D_MODEL), lambda j, i: (i, 0))] + w_specs,
        out_specs=[pl.BlockSpec((None, tm, tn), lambda j, i: (j, i, 0))] * n_seg,
        out_shape=[jax.ShapeDtypeStruct((groups, rows, tn), dt) for dt in dtypes],
        scratch_shapes=[pltpu.VMEM((D_MODEL, tn), BF16)] * n_seg,
        compiler_params=pltpu.CompilerParams(
            dimension_semantics=("arbitrary", "arbitrary"), vmem_limit_bytes=VMEM_LIMIT),
        name="hgrn_inproj",
    )(h, *([w] * n_seg))


def _scan_kernel(zf_ref, zb_ref, v_ref, q_ref, g_ref, czf_ref, czb_ref, cv_ref,
                 lbf_ref, lbb_ref, og_ref, o_ref, cs_s, st_s, of_s, ob_s):
    c = SCAN_CHUNK
    n_lat = v_ref.shape[1] // c
    n_ctx = cv_ref.shape[1] // c
    heads_per_group = v_ref.shape[2] // HG_DK
    n_heads = v_ref.shape[0] * heads_per_group
    row = lax.broadcasted_iota(jnp.int32, (c, c), 0)
    col = lax.broadcasted_iota(jnp.int32, (c, c), 1)
    dirs = ((zf_ref, czf_ref, lbf_ref[...], False, col <= row, of_s),
            (zb_ref, czb_ref, lbb_ref[...], True, col >= row, ob_s))
    n_decay_calls = [0]

    def lanes(h):
        return slice(h * HG_DK, (h + 1) * HG_DK)

    def rows(chunk):
        return slice(chunk * c, (chunk + 1) * c)

    def tokens(ref, h, chunk):
        return ref[h // heads_per_group, rows(chunk), lanes(h % heads_per_group)]

    def decays(d, z_ref, h, chunk):
        _, _, lb, reverse, _, _ = dirs[d]
        slot = n_decay_calls[0] % cs_s.shape[0]
        n_decay_calls[0] += 1
        return _gate_decay(tokens(z_ref, h, chunk), lb[:, lanes(h)], reverse, cs_s.at[slot])

    def state_increment(v, k_inv, d_second):
        k_dec = k_inv * d_second[0:1, :].astype(BF16)
        return lax.dot_general(v, k_dec, _TN, preferred_element_type=F32)

    def local(d, chunk):
        z_ref, mask = dirs[d][0], dirs[d][4]
        out = []
        for h in range(n_heads):
            e_a, k_inv, d_first, d_second = decays(d, z_ref, h, chunk)
            q_dec = (tokens(q_ref, h, chunk).astype(F32) * e_a).astype(BF16)
            sc = lax.dot_general(q_dec, k_inv, _NT, preferred_element_type=F32)
            ds_t = state_increment(tokens(v_ref, h, chunk), k_inv, d_second)
            out.append((q_dec, jnp.where(mask, sc, 0.0).astype(BF16), ds_t,
                        d_first[0:1, :], d_second[0:1, :]))
        return out

    def step(d, chunk, products):
        out_s = dirs[d][5]
        for h in range(n_heads):
            q_dec, sc, ds_t, d_first, d_second = products[h]
            st_mid = (st_s[d, h] * d_first).astype(BF16)
            inter = lax.dot_general(q_dec, st_mid, _NT, preferred_element_type=F32)
            out_s[rows(chunk), lanes(h)] = inter + jnp.dot(
                sc, tokens(v_ref, h, chunk), preferred_element_type=F32)
            st_s[d, h] = st_s[d, h] * (d_first * d_second) + ds_t

    st_s[...] = jnp.zeros_like(st_s)
    for j in range(n_ctx):
        for d in range(2):
            cj = n_ctx - 1 - j if dirs[d][3] else j
            for h in range(n_heads):
                _, k_inv, d_first, d_second = decays(d, dirs[d][1], h, cj)
                ds_t = state_increment(tokens(cv_ref, h, cj), k_inv, d_second)
                st_s[d, h] = st_s[d, h] * (d_first[0:1, :] * d_second[0:1, :]) + ds_t

    def chunk_of(d, idx):
        return n_lat - 1 - idx if dirs[d][3] else idx

    og = og_ref[...]
    ones = jnp.ones((HG_DV, HG_DV), BF16)

    def readout(chunk):
        sl = rows(chunk)
        for h in range(n_heads):
            o = of_s[sl, lanes(h)] + ob_s[sl, lanes(h)]
            ss = jnp.dot((o * o).astype(BF16), ones, preferred_element_type=F32)
            o = o * lax.rsqrt(ss * (1.0 / HG_DV) + EPS)
            gh = tokens(g_ref, h, chunk).astype(F32)
            o_ref[sl, lanes(h)] = (o * og[:, lanes(h)]
                                   * (gh + gh * jnp.tanh(gh))).astype(o_ref.dtype)

    assert n_lat % 2 == 0
    products = [local(d, chunk_of(d, 0)) for d in range(2)]
    for idx in range(n_lat):
        for d in range(2):
            step(d, chunk_of(d, idx), products[d])
        if idx + 1 < n_lat:
            products = [local(d, chunk_of(d, idx + 1)) for d in range(2)]
        if 2 * idx >= n_lat:
            readout(idx)
            readout(n_lat - 1 - idx)


def _hgrn2_scan(lat_ops, ctx_ops, lb_f, lb_b, onorm_g, nb):
    groups, rows, w = lat_ops[-1].shape
    n = rows // nb

    def blocked(a):
        per_batch = a.shape[1] // nb
        return (a.reshape(groups, nb, per_batch, w),
                pl.BlockSpec((SCAN_GROUPS, None, per_batch, w), lambda b, h: (h, b, 0, 0)))

    ops, specs = zip(*[blocked(a) for a in (*lat_ops, *ctx_ops)])
    wide = SCAN_GROUPS * w
    vec = pl.BlockSpec((1, wide), lambda b, h: (0, h))
    return pl.pallas_call(
        _scan_kernel,
        grid=(nb, groups // SCAN_GROUPS),
        in_specs=list(specs) + [vec] * 3,
        out_specs=pl.BlockSpec((None, n, wide), lambda b, h: (b, 0, h)),
        out_shape=jax.ShapeDtypeStruct((nb, n, HG_V), BF16),
        scratch_shapes=[pltpu.VMEM((SCAN_DECAY_SLOTS, 4, CHUNK, HG_DK), F32),
                        pltpu.VMEM((2, wide // HG_DK, HG_DV, HG_DK), F32),
                        pltpu.VMEM((n, wide), F32),
                        pltpu.VMEM((n, wide), F32)],
        compiler_params=pltpu.CompilerParams(
            dimension_semantics=("arbitrary", "arbitrary"), vmem_limit_bytes=VMEM_LIMIT),
        name="hgrn2_scan",
    )(*ops, lb_f, lb_b, onorm_g)


def _window(n, w):
    t = np.arange(n)
    lo = np.maximum(t - w // 2, 0)
    hi = np.minimum(t + w // 2 - 1, n - 1)
    s = np.arange(n)
    inside = (s[None, :] >= lo[:, None]) & (s[None, :] <= hi[:, None])
    return inside, (hi - lo + 1)


def _pool_operators(n, on_grid):
    mats, icnts, taps = [], [], []
    for w in POOL_WINDOWS:
        if on_grid:
            in_c, cnt_c = _window(GRID_W, w)
            _, cnt_r = _window(n // GRID_W, w)
            m = np.kron(np.eye(POOL_SLAB // GRID_W, dtype=bool), in_c)
            cnt = (cnt_r[:, None] * cnt_c[None, :]).reshape(n)
            taps.append((-(w // 2), w // 2 - 1))
        else:
            assert n == POOL_SLAB
            m, cnt = _window(n, w)
            taps.append((0, 0))
        mats.append(m)
        icnts.append(1.0 / cnt.astype(np.float64))
    colop = jnp.asarray(np.stack(mats).astype(np.float32), dtype=BF16)
    icnt = jnp.asarray(np.stack(icnts).astype(np.float32)[..., None])
    return colop, icnt, tuple(taps)


def kernel(x, c, ctx, c_ctx, ada_w, ada_b, pre_g, post_g, ev_w_in, ev_pool_w, ev_pool_scale,
           ev_conv_w, ev_conv_b, ev_w_out, od_w_in, od_onorm_g, od_w_out, lb_logits):
    nb, n, d = x.shape
    nc = ctx.shape[1]
    lat_row = lambda b: b
    ctx_row = lambda b: CTX_ROW

    lb_table = jnp.cumsum(jax.nn.softmax(lb_logits.astype(F32), axis=1), axis=1)

    cc = jnp.zeros((MOD_ROWS, d), F32).at[:nb].set(c).at[CTX_ROW].set(c_ctx)
    mod = _ada_table(cc, ada_w, ada_b)
    mod = mod.reshape(2, MOD_ROWS, 3, 1, d)
    shift = [mod[l, :, 0] for l in range(2)]
    scale = [mod[l, :, 1] for l in range(2)]
    gate = [mod[l, :, 2] for l in range(2)]

    ctx_flat = ctx.reshape(1, nb * nc, d)

    pool_w = ev_pool_w[0].astype(BF16)
    pool_scale = ev_pool_scale[0].reshape(1, POOL_W)
    conv_w = ev_conv_w[0]
    conv_b = ev_conv_b[0].reshape(1, CONV_W)
    pre0 = pre_g[0].reshape(1, d)
    post0 = post_g[0].reshape(1, d)

    pre1 = pre_g[1].reshape(1, d)
    post1 = post_g[1].reshape(1, d)
    norm1 = (shift[1], scale[1], pre1)

    def even_mixer(tokens3, mod_row, on_grid):
        b3, n3, _ = tokens3.shape
        h = _norm_tokens(tokens3, mod_row, shift[0], scale[0], pre0, TM_NORM)
        proj, = _inproj(h.reshape(b3 * n3, d), ev_w_in[0], EVEN_IN, [BF16],
                        min(TM_IN0, b3 * n3), TN_IN0)
        n_tok = n if on_grid else nc
        proj = proj.reshape(nb, n_tok, EVEN_IN)
        colop, icnt, taps = _pool_operators(n_tok, on_grid)
        a_out = _pool_mixer(proj, colop, icnt, taps, pool_w, pool_scale)
        b_out = _conv_mixer(proj, conv_w, conv_b)
        acts = [a_out.reshape(b3, n3, POOL_W), b_out.reshape(b3, n3, CONV_W)]
        return _outproj(acts, ev_w_out[0], tokens3, mod_row, gate[0], post0, TM_OUT, norm1)

    x1, h1 = even_mixer(x, lat_row, True)
    _, hc1 = even_mixer(ctx_flat, ctx_row, False)

    lb_f = lb_table[0, 1].reshape(1, HG_K)
    lb_b = lb_table[1, 1].reshape(1, HG_K)
    lat_ops = _hgrn_inproj(h1.reshape(nb * n, d), od_w_in[0], True, TM_IN)
    ctx_ops = _hgrn_inproj(hc1.reshape(nb * nc, d), od_w_in[0], False, TM_IN)
    o = _hgrn2_scan(lat_ops, ctx_ops, lb_f, lb_b, od_onorm_g[0].reshape(1, HG_V), nb)
    return _outproj([o], od_w_out[0], x1, lat_row, gate[1], post1, TM_OUT)
```
